```python
import jax, jax.numpy as jnp
from jax import lax
import numpy as np

D_MODEL = 2048
BATCH = 16
SEQ = 256
DEPTH = 2
DEC_BATCH = 4
DEC_SEQ = 1024
PAST_LEN = 512

GRID_W = 64
N_MIXERS = 2
N_RET_LAYERS = (DEPTH + 1) // 2
N_NA_LAYERS = DEPTH // 2
RET_HEADS = 8
RET_DK = D_MODEL // RET_HEADS
RET_DV = 2 * RET_DK
RET_CHUNK = 64
NA_HEADS = 16
NA_DH = D_MODEL // NA_HEADS
WIN_H = 8
WIN_W = 16
Q_BLOCK = 128
N_GROUPS = 4
N_EXP_PER_GROUP = 8
N_EXPERTS = N_GROUPS * N_EXP_PER_GROUP
TOP_K_IN_GROUP = 2
D_EXPERT = D_MODEL // 4
ROPE_BASE = 10000.0
ALPHA = (2.0 * DEPTH) ** 0.25
BETA = (8.0 * DEPTH) ** -0.25
LN_EPS = 1e-5
F32 = jnp.float32
NEG_INF = -1e30

kernel_name = 'hybrid_retention_natten_hmoe_step'


def layer_norm(x, g, b):
    xf = x.astype(F32)
    mu = jnp.mean(xf, axis=-1, keepdims=True)
    var = jnp.mean(jnp.square(xf - mu), axis=-1, keepdims=True)
    y = (xf - mu) * lax.rsqrt(var + LN_EPS) * g.astype(F32) + b.astype(F32)
    return y.astype(x.dtype)


def modulation(c, w_mod, b_mod):
    m = jax.nn.silu(c) @ w_mod + b_mod
    return jnp.split(m[:, None, :], 6, axis=-1)


def heads_split(t, n_heads, d):
    b, n, _ = t.shape
    return t.reshape(b, n, n_heads, d).transpose(0, 2, 1, 3)


def axial_rope(x):
    b, h, n, dk = x.shape
    nf = dk // 4
    t = jnp.arange(n)
    rows = (t // GRID_W).astype(F32)
    cols = (t % GRID_W).astype(F32)
    inv_freq = ROPE_BASE ** (-jnp.arange(nf, dtype=F32) / nf)
    ang = jnp.stack([rows[:, None] * inv_freq, cols[:, None] * inv_freq], axis=1)
    cos, sin = jnp.cos(ang), jnp.sin(ang)
    xr = x.astype(F32).reshape(b, h, n, 2, 2, nf)
    x1, x2 = xr[..., 0, :], xr[..., 1, :]
    out = jnp.stack([x1 * cos - x2 * sin, x2 * cos + x1 * sin], axis=-2)
    return out.reshape(x.shape).astype(x.dtype)


def retention_chunkwise(q, k, v, log_g, s0):
    b, h, n, dk = q.shape
    dv = v.shape[-1]
    cs = RET_CHUNK
    nc = n // cs
    qc = q.reshape(b, h, nc, cs, dk)
    kc = k.reshape(b, h, nc, cs, dk)
    vc = v.reshape(b, h, nc, cs, dv)
    pos = jnp.arange(cs, dtype=F32)
    diff = pos[:, None] - pos[None, :]
    dmask = jnp.where(diff[None] >= 0, jnp.exp(jnp.maximum(diff, 0.0)[None] * log_g[:, None, None]), 0.0)
    inner = jnp.einsum('bhcid,bhcjd->bhcij', qc, kc) * dmask[None, :, None]
    o_inner = jnp.einsum('bhcij,bhcjv->bhciv', inner, vc)
    q_decay = jnp.exp((pos + 1.0)[None, :] * log_g[:, None])
    k_decay = jnp.exp((cs - 1.0 - pos)[None, :] * log_g[:, None])
    c_decay = jnp.exp(cs * log_g)[None, :, None, None]
    kv = jnp.einsum('bhcjd,bhcjv->cbhdv', kc * k_decay[None, :, None, :, None], vc)

    def step(s, kv_c):
        return c_decay * s + kv_c, s

    s_fin, s_prev = lax.scan(step, s0, kv)
    o_cross = jnp.einsum('bhcid,cbhdv->bhciv', qc * q_decay[None, :, None, :, None], s_prev)
    return (o_inner + o_cross).reshape(b, h, n, dv), s_fin


def head_group_norm(o, g, b):
    mu = jnp.mean(o, axis=-1, keepdims=True)
    var = jnp.mean(jnp.square(o - mu), axis=-1, keepdims=True)
    return (o - mu) * lax.rsqrt(var + LN_EPS) * g.astype(F32)[None, :, None, :] + b.astype(F32)[None, :, None, :]


def retention_mixer(h, s_init, w_in, decay_logit, gn_g, gn_b, w_out, use_rope):
    hk = RET_HEADS * RET_DK
    hv = RET_HEADS * RET_DV
    proj = h @ w_in
    q = heads_split(proj[..., :hk], RET_HEADS, RET_DK)
    k = heads_split(proj[..., hk:2 * hk], RET_HEADS, RET_DK)
    v = heads_split(proj[..., 2 * hk:2 * hk + hv], RET_HEADS, RET_DV).astype(F32)
    gf = heads_split(proj[..., 2 * hk + hv:2 * hk + 2 * hv], RET_HEADS, RET_DV).astype(F32)
    gb = heads_split(proj[..., 2 * hk + 2 * hv:], RET_HEADS, RET_DV).astype(F32)
    if use_rope:
        q = axial_rope(q)
        k = axial_rope(k)
    q = q.astype(F32)
    k = k.astype(F32) * (RET_DK ** -0.5)
    log_g = -jax.nn.softplus(-decay_logit.astype(F32))
    s0 = s_init.astype(F32)
    o_f, s_f = retention_chunkwise(q, k, v, log_g[0], s0[:, 0])
    o_b, s_b = retention_chunkwise(q[:, :, ::-1], k[:, :, ::-1], v[:, :, ::-1], log_g[1], s0[:, 1])
    o_b = o_b[:, :, ::-1]
    y = head_group_norm(o_f, gn_g, gn_b) * jax.nn.silu(gf) + head_group_norm(o_b, gn_g, gn_b) * jax.nn.silu(gb)
    b, n, _ = h.shape
    y = y.transpose(0, 2, 1, 3).reshape(b, n, hv).astype(h.dtype)
    return y @ w_out, jnp.stack([s_f, s_b], axis=1)


def dense_attention(q, k, v):
    b, h, s, dh = q.shape
    nb = s // Q_BLOCK
    qb = q.reshape(b, h, nb, Q_BLOCK, dh).transpose(2, 0, 1, 3, 4)

    def blk(qi):
        sc = jnp.einsum('bhqd,bhkd->bhqk', qi, k).astype(F32)
        p = jax.nn.softmax(sc, axis=-1).astype(v.dtype)
        return jnp.einsum('bhqk,bhkd->bhqd', p, v)

    o = lax.map(blk, qb)
    return o.transpose(1, 0, 3, 2, 4).reshape(b, s, h * dh)


def na_context(h, w_in, w_out):
    d = D_MODEL
    proj = h @ w_in
    q = heads_split(proj[..., :d], NA_HEADS, NA_DH) * (NA_DH ** -0.5)
    k = heads_split(proj[..., d:2 * d], NA_HEADS, NA_DH)
    v = heads_split(proj[..., 2 * d:], NA_HEADS, NA_DH)
    o = dense_attention(q, k, v)
    return o @ w_out, k, v


def na_latent(h, ck, cv, w_in, rpb, w_out):
    b, n, _ = h.shape
    d = D_MODEL
    rows = n // GRID_W
    kh = min(WIN_H, rows)
    kw = WIN_W
    proj = h @ w_in
    q = heads_split(proj[..., :d], NA_HEADS, NA_DH) * (NA_DH ** -0.5)
    k = heads_split(proj[..., d:2 * d], NA_HEADS, NA_DH)
    v = heads_split(proj[..., 2 * d:], NA_HEADS, NA_DH)
    qg = q.reshape(b, NA_HEADS, rows, GRID_W, NA_DH)
    kg = k.reshape(b, NA_HEADS, rows, GRID_W, NA_DH)
    vg = v.reshape(b, NA_HEADS, rows, GRID_W, NA_DH)
    col = jnp.arange(GRID_W)
    c0 = jnp.clip(col - kw // 2, 0, GRID_W - kw)
    col_ok = (col[None, :] >= c0[:, None]) & (col[None, :] < c0[:, None] + kw)
    dc_idx = jnp.clip(col[None, :] - col[:, None], -(WIN_W - 1), WIN_W - 1) + WIN_W - 1
    rpb_cols = rpb.astype(F32)[:, :, dc_idx]

    def row_block(r):
        r0 = jnp.clip(r - kh // 2, 0, rows - kh)
        kb = lax.dynamic_slice_in_dim(kg, r0, kh, axis=2)
        vb = lax.dynamic_slice_in_dim(vg, r0, kh, axis=2)
        qr = lax.dynamic_index_in_dim(qg, r, axis=2, keepdims=False)
        dr_idx = r0 + jnp.arange(kh) - r + WIN_H - 1
        bias = jnp.take(rpb_cols, dr_idx, axis=1).transpose(0, 2, 1, 3)
        s_loc = jnp.einsum('bhqd,bhrkd->bhqrk', qr, kb).astype(F32) + bias[None]
        s_loc = jnp.where(col_ok[:, None, :], s_loc, NEG_INF).reshape(b, NA_HEADS, GRID_W, kh * GRID_W)
        s_ctx = jnp.einsum('bhqd,bhpd->bhqp', qr, ck).astype(F32)
        p = jax.nn.softmax(jnp.concatenate([s_loc, s_ctx], axis=-1), axis=-1).astype(v.dtype)
        p_loc = p[..., :kh * GRID_W].reshape(b, NA_HEADS, GRID_W, kh, GRID_W)
        p_ctx = p[..., kh * GRID_W:]
        return jnp.einsum('bhqrk,bhrkd->bhqd', p_loc, vb) + jnp.einsum('bhqp,bhpd->bhqd', p_ctx, cv)

    o = lax.map(row_block, jnp.arange(rows))
    o = o.transpose(1, 0, 3, 2, 4).reshape(b, n, d)
    return o @ w_out


def hier_moe(h, w_rg, b_rg, w_re, b_re, w_gate, w_up, w_down):
    b, n, d = h.shape
    x = h.reshape(b * n, d)
    g_logits = (x @ w_rg).astype(F32) + b_rg.astype(F32)
    p_group = jax.nn.softmax(g_logits, axis=-1)
    g_sel = jnp.argmax(g_logits, axis=-1)
    p_g = jnp.take_along_axis(p_group, g_sel[:, None], axis=1)
    e_logits = jnp.einsum('td,gde->tge', x, w_re).astype(F32) + b_re.astype(F32)
    e_sel = jnp.take_along_axis(e_logits, g_sel[:, None, None], axis=1)[:, 0]
    top_p, top_i = lax.top_k(jax.nn.softmax(e_sel, axis=-1), TOP_K_IN_GROUP)
    top_p = top_p / jnp.sum(top_p, axis=-1, keepdims=True)
    expert_ids = g_sel[:, None] * N_EXP_PER_GROUP + top_i
    combine = jnp.sum(jax.nn.one_hot(expert_ids, N_EXPERTS, dtype=F32) * (p_g * top_p)[..., None], axis=1)
    hid = jax.nn.silu(jnp.einsum('td,edf->tef', x, w_gate)) * jnp.einsum('td,edf->tef', x, w_up)
    hid = hid * combine[:, :, None].astype(hid.dtype)
    y = jnp.einsum('tef,efd->td', hid, w_down)
    return y.reshape(b, n, d)


def setup_inputs(seed: int = 0) -> dict:
    key = jax.random.key(seed)
    ks = jax.random.split(key, 32)

    def nrm(k, shape, scale):
        return jax.random.normal(k, shape, F32) * scale

    d = D_MODEL
    hk = RET_HEADS * RET_DK
    hv = RET_HEADS * RET_DV
    gam = 1.0 - 2.0 ** (-5.0 - np.arange(RET_HEADS, dtype=np.float32))
    base_logit = jnp.asarray(np.log(gam / (1.0 - gam)).astype(np.float32))
    return {
        'x_prompt': nrm(ks[0], (BATCH, SEQ, d), 1.0),
        'x_sample': nrm(ks[1], (DEC_BATCH, DEC_SEQ, d), 1.0),
        'state_ret': nrm(ks[2], (DEC_BATCH, N_RET_LAYERS, 2, RET_HEADS, RET_DK, RET_DV), 0.5),
        'cache_na_k': nrm(ks[3], (DEC_BATCH, N_NA_LAYERS, NA_HEADS, PAST_LEN, NA_DH), 1.0),
        'cache_na_v': nrm(ks[4], (DEC_BATCH, N_NA_LAYERS, NA_HEADS, PAST_LEN, NA_DH), 1.0),
        'c': nrm(ks[5], (DEC_BATCH, d), 1.0),
        'c_ctx': nrm(ks[6], (d,), 1.0),
        'w_mod': nrm(ks[7], (DEPTH, d, 6 * d), 0.5 * d ** -0.5),
        'b_mod': nrm(ks[8], (DEPTH, 6 * d), 0.02),
        'ln1_g': 1.0 + nrm(ks[9], (DEPTH, d), 0.01),
        'ln1_b': nrm(ks[10], (DEPTH, d), 0.01),
        'ln2_g': 1.0 + nrm(ks[11], (DEPTH, d), 0.01),
        'ln2_b': nrm(ks[12], (DEPTH, d), 0.01),
        'w_ret_in': nrm(ks[13], (N_RET_LAYERS, d, 2 * hk + 3 * hv), d ** -0.5),
        'ret_decay_logit': base_logit[None, None, :] + nrm(ks[14], (N_RET_LAYERS, 2, RET_HEADS), 0.01),
        'ret_gn_g': 1.0 + nrm(ks[15], (N_RET_LAYERS, RET_HEADS, RET_DV), 0.01),
        'ret_gn_b': nrm(ks[16], (N_RET_LAYERS, RET_HEADS, RET_DV), 0.01),
        'w_ret_out': nrm(ks[17], (N_RET_LAYERS, hv, d), BETA * hv ** -0.5),
        'w_na_in': nrm(ks[18], (N_NA_LAYERS, d, 3 * d), d ** -0.5),
        'na_rpb': nrm(ks[19], (N_NA_LAYERS, NA_HEADS, 2 * WIN_H - 1, 2 * WIN_W - 1), 0.1),
        'w_na_out': nrm(ks[20], (N_NA_LAYERS, d, d), BETA * d ** -0.5),
        'w_rg': nrm(ks[21], (DEPTH, d, N_GROUPS), d ** -0.5),
        'b_rg': nrm(ks[22], (DEPTH, N_GROUPS), 0.01),
        'w_re': nrm(ks[23], (DEPTH, N_GROUPS, d, N_EXP_PER_GROUP), d ** -0.5),
        'b_re': nrm(ks[24], (DEPTH, N_GROUPS, N_EXP_PER_GROUP), 0.01),
        'w_gate': nrm(ks[25], (DEPTH, N_EXPERTS, d, D_EXPERT), d ** -0.5),
        'w_up': nrm(ks[26], (DEPTH, N_EXPERTS, d, D_EXPERT), d ** -0.5),
        'w_down': nrm(ks[27], (DEPTH, N_EXPERTS, D_EXPERT, d), BETA * D_EXPERT ** -0.5),
    }


def reference(x_prompt, x_sample, state_ret, cache_na_k, cache_na_v, c, c_ctx, w_mod, b_mod, ln1_g, ln1_b, ln2_g, ln2_b, w_ret_in, ret_decay_logit, ret_gn_g, ret_gn_b, w_ret_out, w_na_in, na_rpb, w_na_out, w_rg, b_rg, w_re, b_re, w_gate, w_up, w_down):
    xp, xs = x_prompt, x_sample
    bp = xp.shape[0]
    new_ret, new_k, new_v = [], [], []
    for l in range(DEPTH):
        j = l // N_MIXERS
        mp_ = modulation(c_ctx[None, :], w_mod[l], b_mod[l])
        ms_ = modulation(c, w_mod[l], b_mod[l])
        hp = xp * (1.0 + mp_[1]) + mp_[0]
        hs = xs * (1.0 + ms_[1]) + ms_[0]
        if l % N_MIXERS == 0:
            s_zero = jnp.zeros((bp, 2, RET_HEADS, RET_DK, RET_DV), F32)
            op, sp = retention_mixer(hp, s_zero, w_ret_in[j], ret_decay_logit[j], ret_gn_g[j], ret_gn_b[j], w_ret_out[j], False)
            os_, _ = retention_mixer(hs, state_ret[:, j], w_ret_in[j], ret_decay_logit[j], ret_gn_g[j], ret_gn_b[j], w_ret_out[j], True)
            new_ret.append(sp)
        else:
            op, kp, vp = na_context(hp, w_na_in[j], w_na_out[j])
            os_ = na_latent(hs, cache_na_k[:, j], cache_na_v[:, j], w_na_in[j], na_rpb[j], w_na_out[j])
            new_k.append(kp)
            new_v.append(vp)
        xp = layer_norm(ALPHA * xp + mp_[2] * op, ln1_g[l], ln1_b[l])
        xs = layer_norm(ALPHA * xs + ms_[2] * os_, ln1_g[l], ln1_b[l])
        hp = xp * (1.0 + mp_[4]) + mp_[3]
        hs = xs * (1.0 + ms_[4]) + ms_[3]
        yp = hier_moe(hp, w_rg[l], b_rg[l], w_re[l], b_re[l], w_gate[l], w_up[l], w_down[l])
        ys = hier_moe(hs, w_rg[l], b_rg[l], w_re[l], b_re[l], w_gate[l], w_up[l], w_down[l])
        xp = layer_norm(ALPHA * xp + mp_[5] * yp, ln2_g[l], ln2_b[l])
        xs = layer_norm(ALPHA * xs + ms_[5] * ys, ln2_g[l], ln2_b[l])
    return (xp, xs, jnp.stack(new_ret, axis=1), jnp.stack(new_k, axis=1), jnp.stack(new_v, axis=1))
```

```python
import functools

import jax
import jax.numpy as jnp
from jax import lax
from jax.experimental import pallas as pl
from jax.experimental.pallas import tpu as pltpu

F32 = jnp.float32
BF16 = jnp.bfloat16

GRID_W = 64
WIN_H = 8
WIN_W = 16
N_GROUPS = 4
N_EXP_PER_GROUP = 8
N_EXPERTS = N_GROUPS * N_EXP_PER_GROUP
ROPE_BASE = 10000.0
LN_EPS = 1e-5
NEG_INF = -1e30

V7X_VMEM_BYTES = 64 * 1024 * 1024
V7X_LANES = 128
MOD_ROWS = 8

RET_CHUNK = 256
ATTN_QB = 256
MOE_TM = 256
COMB_TM = 256


def _cparams(sem, vmem_bytes):
    limit = int(min(max(vmem_bytes, 16 * 1024 * 1024), V7X_VMEM_BYTES - 6 * 1024 * 1024))
    return pltpu.CompilerParams(dimension_semantics=sem, vmem_limit_bytes=limit)


def _silu(x):
    return x / (1.0 + jnp.exp(-x))


def _dot(a, b):
    return jnp.dot(a, b, preferred_element_type=F32)


def _dot_nt(a, b):
    return lax.dot_general(a, b, (((1,), (1,)), ((), ())), preferred_element_type=F32)


def _layer_norm(z, g, b):
    mu = jnp.mean(z, axis=-1, keepdims=True)
    zc = z - mu
    var = jnp.mean(zc * zc, axis=-1, keepdims=True)
    return zc * lax.rsqrt(var + LN_EPS) * g + b


def _mod_kernel(c_ref, w_ref, b_ref, o_ref):
    s = _silu(c_ref[...]).astype(BF16)
    o_ref[0] = _dot(s, w_ref[0].astype(BF16)) + b_ref[0]


def _modulation(cc, w_mod, b_mod):
    depth, d, n6 = w_mod.shape
    tn = 1024
    return pl.pallas_call(
        _mod_kernel,
        grid=(depth, n6 // tn),
        in_specs=[
            pl.BlockSpec((MOD_ROWS, d), lambda l, j: (0, 0)),
            pl.BlockSpec((1, d, tn), lambda l, j: (l, 0, j)),
            pl.BlockSpec((1, 1, tn), lambda l, j: (l, 0, j)),
        ],
        out_specs=pl.BlockSpec((1, MOD_ROWS, tn), lambda l, j: (l, 0, j)),
        out_shape=jax.ShapeDtypeStruct((depth, MOD_ROWS, n6), F32),
        compiler_params=_cparams(("arbitrary", "arbitrary"), 3 * d * tn * 4 + (4 << 20)),
    )(cc, w_mod, b_mod.reshape(depth, 1, n6))


def _inproj_kernel(x_ref, sc_ref, sh_ref, w_ref, o_ref, xb_ref):
    @pl.when(pl.program_id(1) == 0)
    def _():
        xb_ref[...] = (x_ref[...] * (1.0 + sc_ref[0]) + sh_ref[0]).astype(BF16)

    o_ref[...] = _dot(xb_ref[...], w_ref[...].astype(BF16))


def _inproj(x, mods3, mod_row, layer, w_all, w_idx, tm=1024, tn=512):
    m, d = x.shape
    n = w_all.shape[-1]
    base = layer * MOD_ROWS
    return pl.pallas_call(
        _inproj_kernel,
        grid=(m // tm, n // tn),
        in_specs=[
            pl.BlockSpec((tm, d), lambda i, j: (i, 0)),
            pl.BlockSpec((1, 1, d), lambda i, j: (base + mod_row(i, tm), 0, 1)),
            pl.BlockSpec((1, 1, d), lambda i, j: (base + mod_row(i, tm), 0, 0)),
            pl.BlockSpec((None, d, tn), lambda i, j: (w_idx, 0, j)),
        ],
        out_specs=pl.BlockSpec((tm, tn), lambda i, j: (i, j)),
        out_shape=jax.ShapeDtypeStruct((m, n), F32),
        scratch_shapes=[pltpu.VMEM((tm, d), BF16)],
        compiler_params=_cparams(
            ("arbitrary", "arbitrary"),
            2 * tm * d * 4 + 3 * d * tn * 4 + 2 * tm * tn * 4 + tm * d * 2 + (4 << 20)),
    )(x, mods3, mods3, w_all)


def _outproj_ln_kernel(alpha, y_ref, w_ref, x_ref, gate_ref, g_ref, b_ref, o_ref, acc_ref):
    k = pl.program_id(1)

    @pl.when(k == 0)
    def _():
        acc_ref[...] = jnp.zeros_like(acc_ref)

    acc_ref[...] += _dot(y_ref[...], w_ref[...].astype(BF16))

    @pl.when(k == pl.num_programs(1) - 1)
    def _():
        z = alpha * x_ref[...] + gate_ref[0] * acc_ref[...]
        o_ref[...] = _layer_norm(z, g_ref[0], b_ref[0])


def _outproj_ln(y, w_all, w_idx, x, mods3, mod_row, layer, ln_g, ln_b, alpha, tm=512, tk=512):
    m, kdim = y.shape
    d = x.shape[1]
    base = layer * MOD_ROWS
    return pl.pallas_call(
        functools.partial(_outproj_ln_kernel, alpha),
        grid=(m // tm, kdim // tk),
        in_specs=[
            pl.BlockSpec((tm, tk), lambda i, k: (i, k)),
            pl.BlockSpec((None, tk, d), lambda i, k: (w_idx, k, 0)),
            pl.BlockSpec((tm, d), lambda i, k: (i, 0)),
            pl.BlockSpec((1, 1, d), lambda i, k: (base + mod_row(i, tm), 0, 2)),
            pl.BlockSpec((1, 1, d), lambda i, k: (layer, 0, 0)),
            pl.BlockSpec((1, 1, d), lambda i, k: (layer, 0, 0)),
        ],
        out_specs=pl.BlockSpec((tm, d), lambda i, k: (i, 0)),
        out_shape=jax.ShapeDtypeStruct((m, d), F32),
        scratch_shapes=[pltpu.VMEM((tm, d), F32)],
        compiler_params=_cparams(
            ("arbitrary", "arbitrary"),
            2 * tm * tk * 2 + 3 * tk * d * 4 + 5 * tm * d * 4 + (4 << 20)),
    )(y, w_all, x, mods3, ln_g, ln_b)


def _rope(x, cos, sin):
    half = V7X_LANES // 2
    sw = jnp.concatenate(
        [pltpu.roll(x[:, :V7X_LANES], half, 1), pltpu.roll(x[:, V7X_LANES:], half, 1)], axis=1)
    return x * cos + sw * sin


def _group_norm(o, g, b):
    mu = jnp.mean(o, axis=-1, keepdims=True)
    oc = o - mu
    var = jnp.mean(oc * oc, axis=-1, keepdims=True)
    return oc * lax.rsqrt(var + LN_EPS) * g + b


def _ret_kernel(p_slabs, dk, lg_ref, q_ref, k_ref, v_ref, gf_ref, gb_ref, cos_ref, sin_ref, s0_ref,
                gng_ref, gnb_ref, y_ref, st_ref, q_s, k_s, yacc):
    h = pl.program_id(0)
    s = pl.program_id(1)
    is_prompt = s < p_slabs
    c = RET_CHUNK
    n_chunks = q_ref.shape[0] // c

    cos = cos_ref[0]
    sin = sin_ref[0]
    q_s[...] = _rope(q_ref[...], cos, sin)
    k_s[...] = _rope(k_ref[...], cos, sin) * (dk ** -0.5)

    ri = lax.broadcasted_iota(jnp.int32, (c, c), 0)
    ci = lax.broadcasted_iota(jnp.int32, (c, c), 1)
    diff = (ri - ci).astype(F32)
    pos = lax.broadcasted_iota(jnp.int32, (c, 1), 0).astype(F32)
    gn_g = gng_ref[0]
    gn_b = gnb_ref[0]

    for direction in (0, 1):
        lg = jnp.full((1, 1), lg_ref[direction, h], F32)
        if direction == 0:
            mask = jnp.where(diff >= 0, jnp.exp(jnp.maximum(diff, 0.0) * lg), 0.0)
            q_dec = jnp.exp((pos + 1.0) * lg)
            k_dec = jnp.exp((c - 1.0 - pos) * lg)
            order = range(n_chunks)
            gate_ref = gf_ref
        else:
            mask = jnp.where(diff <= 0, jnp.exp(jnp.maximum(-diff, 0.0) * lg), 0.0)
            q_dec = jnp.exp((c - pos) * lg)
            k_dec = jnp.exp(pos * lg)
            order = range(n_chunks - 1, -1, -1)
            gate_ref = gb_ref
        c_dec = jnp.exp(c * lg)
        state = jnp.where(is_prompt, 0.0, s0_ref[0, direction, 0])
        for n, ch in enumerate(order):
            rows = pl.ds(ch * c, c)
            qc = q_s[rows, :]
            kc = k_s[rows, :]
            vb = v_ref[rows, :].astype(BF16)
            if n > 0:
                state = jnp.where(is_prompt, 0.0, state)
            inner = _dot_nt(qc.astype(BF16), kc.astype(BF16)) * mask
            o = _dot(inner.astype(BF16), vb) + _dot((qc * q_dec).astype(BF16), state.astype(BF16))
            kd_t = jnp.transpose(kc * k_dec).astype(BF16)
            state = c_dec * state + _dot(kd_t, vb)

            @pl.when(is_prompt)
            def _(state=state, ch=ch, direction=direction):
                st_ref[ch, direction, 0] = state

            contrib = _group_norm(o, gn_g, gn_b) * _silu(gate_ref[rows, :])
            if direction == 0:
                yacc[rows, :] = contrib
            else:
                y_ref[rows, :] = (yacc[rows, :] + contrib).astype(BF16)


def _retention(proj, log_g, rope_cos, rope_sin, state_ret, ret_idx, gn_g, gn_b, p_slabs, slab):
    m = proj.shape[0]
    heads, dk, dv = state_ret.shape[-3:]
    n_slabs = m // slab
    per_slab = slab // RET_CHUNK
    v_blk0 = 2 * heads * dk // dv
    p_last = p_slabs - 1

    def tbl(h, s):
        return (jnp.where(s < p_slabs, 0, 1), 0, 0)

    def s0_map(h, s):
        return (jnp.maximum(s - p_slabs, 0), ret_idx, 0, h, 0, 0)

    kernel = functools.partial(_ret_kernel, p_slabs, dk)
    return pl.pallas_call(
        kernel,
        grid=(heads, n_slabs),
        in_specs=[
            pl.BlockSpec(memory_space=pltpu.SMEM),
            pl.BlockSpec((slab, dk), lambda h, s: (s, h)),
            pl.BlockSpec((slab, dk), lambda h, s: (s, heads + h)),
            pl.BlockSpec((slab, dv), lambda h, s: (s, v_blk0 + h)),
            pl.BlockSpec((slab, dv), lambda h, s: (s, v_blk0 + heads + h)),
            pl.BlockSpec((slab, dv), lambda h, s: (s, v_blk0 + 2 * heads + h)),
            pl.BlockSpec((1, slab, dk), tbl),
            pl.BlockSpec((1, slab, dk), tbl),
            pl.BlockSpec((1, None, 2, 1, dk, dv), s0_map),
            pl.BlockSpec((1, 1, dv), lambda h, s: (h, 0, 0)),
            pl.BlockSpec((1, 1, dv), lambda h, s: (h, 0, 0)),
        ],
        out_specs=[
            pl.BlockSpec((slab, dv), lambda h, s: (s, h)),
            pl.BlockSpec((per_slab, 2, 1, dk, dv), lambda h, s: (jnp.minimum(s, p_last), 0, h, 0, 0)),
        ],
        out_shape=[
            jax.ShapeDtypeStruct((m, heads * dv), BF16),
            jax.ShapeDtypeStruct((p_slabs * per_slab, 2, heads, dk, dv), F32),
        ],
        scratch_shapes=[
            pltpu.VMEM((slab, dk), F32),
            pltpu.VMEM((slab, dk), F32),
            pltpu.VMEM((slab, dv), F32),
        ],
        compiler_params=_cparams(
            ("arbitrary", "arbitrary"),
            2 * (2 * slab * dk + 3 * slab * dv) * 4 + 4 * slab * dk * 4 + 2 * 2 * dk * dv * 4
            + 2 * slab * dv * 2 + 2 * per_slab * 2 * dk * dv * 4 + 2 * slab * dk * 4 + slab * dv * 4
            + (8 << 20)),
    )(log_g, proj, proj, proj, proj, proj, rope_cos, rope_sin, state_ret, gn_g, gn_b)


def _attn_kernel(p_slabs, seq, dh, q_ref, k_ref, v_ref, ck_ref, cv_ref, bias_ref, o_ref, nk_ref, nv_ref):
    s = pl.program_id(1)
    is_prompt = s < p_slabs
    slab = q_ref.shape[0]
    per_slab = slab // seq
    k = k_ref[...]
    v = v_ref[...]

    @pl.when(is_prompt)
    def _():
        for b in range(per_slab):
            nk_ref[b, 0, 0] = k[b * seq:(b + 1) * seq, :]
            nv_ref[b, 0, 0] = v[b * seq:(b + 1) * seq, :]

    kb = k.astype(BF16)
    vb = v.astype(BF16)
    ckb = ck_ref[0, 0, 0].astype(BF16)
    cvb = cv_ref[0, 0, 0].astype(BF16)
    qb_rows = ATTN_QB
    for qb in range(slab // qb_rows):
        rows = pl.ds(qb * qb_rows, qb_rows)
        q = (q_ref[rows, :] * (dh ** -0.5)).astype(BF16)
        ri = lax.broadcasted_iota(jnp.int32, (qb_rows, slab), 0) + qb * qb_rows
        ci = lax.broadcasted_iota(jnp.int32, (qb_rows, slab), 1)
        same_seq = (ri // seq) == (ci // seq)
        bias = jnp.where(is_prompt, jnp.where(same_seq, 0.0, NEG_INF), bias_ref[0, rows, :])
        s_loc = _dot_nt(q, kb) + bias
        s_ctx = jnp.where(is_prompt, NEG_INF, _dot_nt(q, ckb))
        mx = jnp.maximum(jnp.max(s_loc, axis=-1, keepdims=True), jnp.max(s_ctx, axis=-1, keepdims=True))
        p_loc = jnp.exp(s_loc - mx)
        p_ctx = jnp.exp(s_ctx - mx)
        denom = jnp.sum(p_loc, axis=-1, keepdims=True) + jnp.sum(p_ctx, axis=-1, keepdims=True)
        o = _dot(p_loc.astype(BF16), vb) + _dot(p_ctx.astype(BF16), cvb)
        o_ref[rows, :] = (o / denom).astype(BF16)


def _attention(proj, cache_k, cache_v, cache_idx, bias, p_slabs, slab, seq):
    m = proj.shape[0]
    heads = cache_k.shape[2]
    dh = cache_k.shape[-1]
    past = cache_k.shape[-2]
    n_slabs = m // slab
    per_slab = slab // seq
    p_last = p_slabs - 1

    def ctx_map(h, s):
        return (jnp.maximum(s - p_slabs, 0), cache_idx, h, 0, 0)

    def new_map(h, s):
        return (jnp.minimum(s, p_last), 0, h, 0, 0)

    kernel = functools.partial(_attn_kernel, p_slabs, seq, dh)
    return pl.pallas_call(
        kernel,
        grid=(heads, n_slabs),
        in_specs=[
            pl.BlockSpec((slab, dh), lambda h, s: (s, h)),
            pl.BlockSpec((slab, dh), lambda h, s: (s, heads + h)),
            pl.BlockSpec((slab, dh), lambda h, s: (s, 2 * heads + h)),
            pl.BlockSpec((1, 1, 1, past, dh), ctx_map),
            pl.BlockSpec((1, 1, 1, past, dh), ctx_map),
            pl.BlockSpec((1, slab, slab), lambda h, s: (h, 0, 0)),
        ],
        out_specs=[
            pl.BlockSpec((slab, dh), lambda h, s: (s, h)),
            pl.BlockSpec((per_slab, 1, 1, seq, dh), new_map),
            pl.BlockSpec((per_slab, 1, 1, seq, dh), new_map),
        ],
        out_shape=[
            jax.ShapeDtypeStruct((m, heads * dh), BF16),
            jax.ShapeDtypeStruct((p_slabs * per_slab, 1, heads, seq, dh), F32),
            jax.ShapeDtypeStruct((p_slabs * per_slab, 1, heads, seq, dh), F32),
        ],
        compiler_params=_cparams(
            ("arbitrary", "arbitrary"),
            2 * slab * slab * 4 + 16 * slab * dh * 4 + 8 * ATTN_QB * (slab + past) * 4 + (8 << 20)),
    )(proj, proj, proj, cache_k, cache_v, bias)


def _na_bias(rpb, rows):
    kh = min(WIN_H, rows)
    col = jnp.arange(GRID_W)
    c0 = jnp.clip(col - WIN_W // 2, 0, GRID_W - WIN_W)
    col_ok = (col[None, :] >= c0[:, None]) & (col[None, :] < c0[:, None] + WIN_W)
    dc_idx = jnp.clip(col[None, :] - col[:, None], -(WIN_W - 1), WIN_W - 1) + WIN_W - 1
    rpb_cols = jnp.where(col_ok[None, None], rpb.astype(F32)[:, :, dc_idx], NEG_INF)
    row = jnp.arange(rows)
    r0 = jnp.clip(row - kh // 2, 0, rows - kh)
    row_ok = (row[None, :] >= r0[:, None]) & (row[None, :] < r0[:, None] + kh)
    dr_idx = jnp.clip(row[None, :] - row[:, None] + WIN_H - 1, 0, 2 * WIN_H - 2)
    b = rpb_cols[:, dr_idx]
    b = jnp.where(row_ok[None, :, :, None, None], b, NEG_INF)
    n = rows * GRID_W
    return b.transpose(0, 1, 3, 2, 4).reshape(rpb.shape[0], n, n)


def _router_kernel(x_ref, sc_ref, sh_ref, wr_ref, br_ref, h_ref, info_ref, cnt_ref, run_ref):
    @pl.when(pl.program_id(0) == 0)
    def _():
        run_ref[...] = jnp.zeros_like(run_ref)

    hm = x_ref[...] * (1.0 + sc_ref[0]) + sh_ref[0]
    h_ref[...] = hm
    logits = jnp.dot(hm, wr_ref[...], preferred_element_type=F32, precision=lax.Precision.HIGHEST) + br_ref[...]
    tm, width = logits.shape
    col = lax.broadcasted_iota(jnp.int32, (tm, width), 1).astype(F32)
    neg = jnp.float32(-3.0e38)

    gl = jnp.where(col < N_GROUPS, logits, neg)
    gmax = jnp.max(gl, axis=-1, keepdims=True)
    gsel = jnp.min(jnp.where(gl == gmax, col, float(width)), axis=-1, keepdims=True)
    p_g = 1.0 / jnp.sum(jnp.where(col < N_GROUPS, jnp.exp(gl - gmax), 0.0), axis=-1, keepdims=True)

    lo = N_GROUPS + N_EXP_PER_GROUP * gsel
    el = jnp.where((col >= lo) & (col < lo + N_EXP_PER_GROUP), logits, neg)
    e1 = jnp.max(el, axis=-1, keepdims=True)
    i1 = jnp.min(jnp.where(el == e1, col, float(width)), axis=-1, keepdims=True)
    el2 = jnp.where(col == i1, neg, el)
    e2 = jnp.max(el2, axis=-1, keepdims=True)
    i2 = jnp.min(jnp.where(el2 == e2, col, float(width)), axis=-1, keepdims=True)
    t = jnp.exp(e2 - e1)
    w1 = 1.0 / (1.0 + t)
    c1 = p_g * w1
    c2 = p_g * (t * w1)
    id1 = i1 - N_GROUPS
    id2 = i2 - N_GROUPS

    onehot = jnp.where((col == id1) | (col == id2), 1.0, 0.0)
    ri = lax.broadcasted_iota(jnp.int32, (tm, tm), 0)
    ci = lax.broadcasted_iota(jnp.int32, (tm, tm), 1)
    tri = jnp.where(ri > ci, 1.0, 0.0).astype(BF16)
    before = _dot(tri, onehot.astype(BF16)) + run_ref[...]
    r1 = jnp.sum(jnp.where(col == id1, before, 0.0), axis=-1, keepdims=True)
    r2 = jnp.sum(jnp.where(col == id2, before, 0.0), axis=-1, keepdims=True)
    run_ref[...] += jnp.sum(onehot, axis=0, keepdims=True)
    cnt_ref[...] = run_ref[...]

    info = jnp.where(col == 0, id1, 0.0)
    info = jnp.where(col == 1, id2, info)
    info = jnp.where(col == 2, c1, info)
    info = jnp.where(col == 3, c2, info)
    info = jnp.where(col == 4, r1, info)
    info = jnp.where(col == 5, r2, info)
    info_ref[...] = info


def _router(x, mods3, mod_row, layer, wr, br, tm=512):
    m, d = x.shape
    base = layer * MOD_ROWS
    return pl.pallas_call(
        _router_kernel,
        grid=(m // tm,),
        in_specs=[
            pl.BlockSpec((tm, d), lambda i: (i, 0)),
            pl.BlockSpec((1, 1, d), lambda i: (base + mod_row(i, tm), 0, 4)),
            pl.BlockSpec((1, 1, d), lambda i: (base + mod_row(i, tm), 0, 3)),
            pl.BlockSpec((d, V7X_LANES), lambda i: (0, 0)),
            pl.BlockSpec((1, V7X_LANES), lambda i: (0, 0)),
        ],
        out_specs=[
            pl.BlockSpec((tm, d), lambda i: (i, 0)),
            pl.BlockSpec((tm, V7X_LANES), lambda i: (i, 0)),
            pl.BlockSpec((1, V7X_LANES), lambda i: (0, 0)),
        ],
        out_shape=[
            jax.ShapeDtypeStruct((m, d), F32),
            jax.ShapeDtypeStruct((m, V7X_LANES), F32),
            jax.ShapeDtypeStruct((1, V7X_LANES), F32),
        ],
        scratch_shapes=[pltpu.VMEM((1, V7X_LANES), F32)],
        compiler_params=_cparams(("arbitrary",), 6 * tm * d * 4 + 2 * d * V7X_LANES * 4 + (8 << 20)),
    )(x, mods3, mods3, wr, br)


def _row_copy(src_hbm, row, dst, dst_row, sem):
    return pltpu.make_async_copy(src_hbm.at[pl.ds(row, 1)], dst.at[pl.ds(dst_row, 1)], sem)


def _moe_kernel(te_ref, nu_ref, src_ref, h_hbm, wg_ref, wu_ref, wd_ref, ys_ref, xbuf, sem):
    del te_ref
    tm = xbuf.shape[0]
    used = pl.program_id(0) < nu_ref[0]

    @pl.when(jnp.logical_not(used))
    def _():
        ys_ref[...] = jnp.zeros_like(ys_ref)

    @pl.when(used)
    def _():
        def issue(r, carry):
            _row_copy(h_hbm, src_ref[0, 0, r], xbuf, r, sem).start()
            return carry

        lax.fori_loop(0, tm, issue, 0)

        def wait(r, carry):
            _row_copy(h_hbm, 0, xbuf, r, sem).wait()
            return carry

        lax.fori_loop(0, tm, wait, 0)
        x = xbuf[...].astype(BF16)
        g = _dot(x, wg_ref[...].astype(BF16))
        u = _dot(x, wu_ref[...].astype(BF16))
        hid = (_silu(g) * u).astype(BF16)
        ys_ref[...] = _dot(hid, wd_ref[...].astype(BF16))


def _moe_experts(hm, src_tok, tile_expert, n_used, w_gate, w_up, w_down, layer):
    m, d = hm.shape
    f = w_gate.shape[-1]
    n_tiles, _, tm = src_tok.shape

    def w_map(i, te, nu):
        return (layer, te[i], 0, 0)

    grid_spec = pltpu.PrefetchScalarGridSpec(
        num_scalar_prefetch=2,
        grid=(n_tiles,),
        in_specs=[
            pl.BlockSpec((1, 1, tm), lambda i, te, nu: (i, 0, 0), memory_space=pltpu.SMEM),
            pl.BlockSpec(memory_space=pl.ANY),
            pl.BlockSpec((None, None, d, f), w_map),
            pl.BlockSpec((None, None, d, f), w_map),
            pl.BlockSpec((None, None, f, d), w_map),
        ],
        out_specs=pl.BlockSpec((tm, d), lambda i, te, nu: (i, 0)),
        scratch_shapes=[pltpu.VMEM((tm, d), F32), pltpu.SemaphoreType.DMA(())],
    )
    return pl.pallas_call(
        _moe_kernel,
        grid_spec=grid_spec,
        out_shape=jax.ShapeDtypeStruct((n_tiles * tm, d), F32),
        compiler_params=_cparams(("arbitrary",), 9 * d * f * 4 + 4 * tm * d * 4 + (8 << 20)),
    )(tile_expert, n_used, src_tok, hm, w_gate, w_up, w_down)


def _combine_ln_kernel(alpha, pos_ref, ys_hbm, info_ref, x_ref, gate_ref, g_ref, b_ref, o_ref, buf, sem):
    tm = x_ref.shape[0]

    def issue(r, carry):
        _row_copy(ys_hbm, pos_ref[0, 0, r], buf, r, sem).start()
        return carry

    lax.fori_loop(0, 2 * tm, issue, 0)

    def wait(r, carry):
        _row_copy(ys_hbm, 0, buf, r, sem).wait()
        return carry

    lax.fori_loop(0, 2 * tm, wait, 0)
    info = info_ref[...]
    y = info[:, 2:3] * buf[pl.ds(0, tm), :] + info[:, 3:4] * buf[pl.ds(tm, tm), :]
    z = alpha * x_ref[...] + gate_ref[0] * y
    o_ref[...] = _layer_norm(z, g_ref[0], b_ref[0])


def _combine_ln(ys, pos, info, x, mods3, mod_row, layer, ln_g, ln_b, alpha):
    m, d = x.shape
    tm = pos.shape[-1] // 2
    base = layer * MOD_ROWS
    return pl.pallas_call(
        functools.partial(_combine_ln_kernel, alpha),
        grid=(m // tm,),
        in_specs=[
            pl.BlockSpec((1, 1, 2 * tm), lambda i: (i, 0, 0), memory_space=pltpu.SMEM),
            pl.BlockSpec(memory_space=pl.ANY),
            pl.BlockSpec((tm, V7X_LANES), lambda i: (i, 0)),
            pl.BlockSpec((tm, d), lambda i: (i, 0)),
            pl.BlockSpec((1, 1, d), lambda i: (base + mod_row(i, tm), 0, 5)),
            pl.BlockSpec((1, 1, d), lambda i: (layer, 0, 0)),
            pl.BlockSpec((1, 1, d), lambda i: (layer, 0, 0)),
        ],
        out_specs=pl.BlockSpec((tm, d), lambda i: (i, 0)),
        out_shape=jax.ShapeDtypeStruct((m, d), F32),
        scratch_shapes=[pltpu.VMEM((2 * tm, d), F32), pltpu.SemaphoreType.DMA(())],
        compiler_params=_cparams(("arbitrary",), 10 * tm * d * 4 + (8 << 20)),
    )(pos, ys, info, x, mods3, ln_g, ln_b)


def _hier_moe_ln(x, mods3, mod_row, layer, w_rg, b_rg, w_re, b_re, w_gate, w_up, w_down, ln_g, ln_b, alpha):
    m, d = x.shape
    n_e = N_EXPERTS
    pad = V7X_LANES - N_GROUPS - n_e
    wr = jnp.concatenate(
        [w_rg[layer], w_re[layer].transpose(1, 0, 2).reshape(d, n_e), jnp.zeros((d, pad), F32)], axis=1)
    br = jnp.concatenate([b_rg[layer], b_re[layer].reshape(n_e), jnp.zeros((pad,), F32)])[None, :]
    hm, info, cnt = _router(x, mods3, mod_row, layer, wr, br)

    tm = MOE_TM
    n_tiles = 2 * m // tm + n_e
    counts = cnt[0, :n_e].astype(jnp.int32)
    tiles_e = (counts + tm - 1) // tm
    tile_end = jnp.cumsum(tiles_e)
    n_used = tile_end[-1]
    row_off = (tile_end - tiles_e) * tm
    tile_ids = jnp.minimum(jnp.arange(n_tiles, dtype=jnp.int32), n_used - 1)
    tile_expert = jnp.searchsorted(tile_end, tile_ids, side="right").astype(jnp.int32)
    id1 = info[:, 0].astype(jnp.int32)
    id2 = info[:, 1].astype(jnp.int32)
    pos1 = row_off[id1] + info[:, 4].astype(jnp.int32)
    pos2 = row_off[id2] + info[:, 5].astype(jnp.int32)
    tok = jnp.arange(m, dtype=jnp.int32)
    src_tok = jnp.zeros((n_tiles * tm,), jnp.int32).at[pos1].set(tok).at[pos2].set(tok)

    ys = _moe_experts(hm, src_tok.reshape(n_tiles, 1, tm), tile_expert, n_used.reshape(1),
                      w_gate, w_up, w_down, layer)

    ct = COMB_TM
    pos = jnp.concatenate([pos1.reshape(m // ct, 1, ct), pos2.reshape(m // ct, 1, ct)], axis=2)
    return _combine_ln(ys, pos, info, x, mods3, mod_row, layer, ln_g, ln_b, alpha)


def kernel(x_prompt, x_sample, state_ret, cache_na_k, cache_na_v, c, c_ctx, w_mod, b_mod, ln1_g, ln1_b, ln2_g, ln2_b, w_ret_in, ret_decay_logit, ret_gn_g, ret_gn_b, w_ret_out, w_na_in, na_rpb, w_na_out, w_rg, b_rg, w_re, b_re, w_gate, w_up, w_down):
    bp, seq, d = x_prompt.shape
    bs, dec_seq, _ = x_sample.shape
    depth = w_mod.shape[0]
    slab = dec_seq
    assert slab % seq == 0 and (bp * seq) % slab == 0 and bs + 1 <= MOD_ROWS
    p_rows = bp * seq
    p_slabs = p_rows // slab
    alpha = (2.0 * depth) ** 0.25

    def mod_row(i, tm):
        start = i * tm
        return jnp.where(start < p_rows, 0, 1 + (start - p_rows) // dec_seq)

    cc = jnp.concatenate([c_ctx[None, :], c, jnp.zeros((MOD_ROWS - 1 - bs, d), F32)], axis=0)
    mods = _modulation(cc, w_mod, b_mod)
    mods3 = mods.reshape(depth * MOD_ROWS, 1, 6 * d)
    x = jnp.concatenate([x_prompt.reshape(p_rows, d), x_sample.reshape(bs * dec_seq, d)], axis=0)

    ln1_g3, ln1_b3 = ln1_g.reshape(depth, 1, d), ln1_b.reshape(depth, 1, d)
    ln2_g3, ln2_b3 = ln2_g.reshape(depth, 1, d), ln2_b.reshape(depth, 1, d)

    new_ret, new_k, new_v = [], [], []
    for l in range(depth):
        j = l // 2
        if l % 2 == 0:
            dk = state_ret.shape[-2]
            proj = _inproj(x, mods3, mod_row, l, w_ret_in, j)
            log_g = -jax.nn.softplus(-ret_decay_logit[j].astype(F32))
            t = jnp.arange(slab)
            nf = dk // 4
            inv_freq = ROPE_BASE ** (-jnp.arange(nf, dtype=F32) / nf)
            ang_r = (t // GRID_W).astype(F32)[:, None] * inv_freq
            ang_c = (t % GRID_W).astype(F32)[:, None] * inv_freq
            cos = jnp.concatenate([jnp.cos(ang_r)] * 2 + [jnp.cos(ang_c)] * 2, axis=1)
            sin = jnp.concatenate([-jnp.sin(ang_r), jnp.sin(ang_r), -jnp.sin(ang_c), jnp.sin(ang_c)], axis=1)
            rope_cos = jnp.stack([jnp.ones_like(cos), cos])
            rope_sin = jnp.stack([jnp.zeros_like(sin), sin])
            y, st = _retention(proj, log_g, rope_cos, rope_sin, state_ret, j, ret_gn_g[j][:, None, :],
                               ret_gn_b[j][:, None, :], p_slabs, slab)
            new_ret.append(st)
            x = _outproj_ln(y, w_ret_out, j, x, mods3, mod_row, l, ln1_g3, ln1_b3, alpha)
        else:
            proj = _inproj(x, mods3, mod_row, l, w_na_in, j)
            bias = _na_bias(na_rpb[j], dec_seq // GRID_W)
            o, nk, nv = _attention(proj, cache_na_k, cache_na_v, j, bias, p_slabs, slab, seq)
            new_k.append(nk[:, 0])
            new_v.append(nv[:, 0])
            x = _outproj_ln(o, w_na_out, j, x, mods3, mod_row, l, ln1_g3, ln1_b3, alpha)
        x = _hier_moe_ln(x, mods3, mod_row, l, w_rg, b_rg, w_re, b_re, w_gate, w_up, w_down,
                         ln2_g3, ln2_b3, alpha)

    y_prompt = x[:p_rows].reshape(bp, seq, d)
    y_sample = x[p_rows:].reshape(bs, dec_seq, d)
    return (y_prompt, y_sample, jnp.stack(new_ret, axis=1), jnp.stack(new_k, axis=1), jnp.stack(new_v, axis=1))
```

```python
import functools

import jax
import jax.numpy as jnp
from jax import lax
from jax.experimental import pallas as pl
from jax.experimental.pallas import tpu as pltpu

F32 = jnp.float32
BF16 = jnp.bfloat16

GRID_W = 64
WIN_H = 8
WIN_W = 16
N_GROUPS = 4
N_EXP_PER_GROUP = 8
N_EXPERTS = N_GROUPS * N_EXP_PER_GROUP
ROPE_BASE = 10000.0
LN_EPS = 1e-5
NEG_INF = -1e30

V7X_VMEM_BYTES = 64 * 1024 * 1024
V7X_LANES = 128
SUBLANES = 8
MOD_ROWS = 8

RET_CHUNK = 256
ATTN_QB = 256
MOE_TM = 256
COMB_TM = 256


def _cparams(sem, vmem_bytes):
    limit = int(min(max(vmem_bytes, 16 * 1024 * 1024), V7X_VMEM_BYTES - 6 * 1024 * 1024))
    return pltpu.CompilerParams(dimension_semantics=sem, vmem_limit_bytes=limit)


def _silu(x):
    return x / (1.0 + jnp.exp(-x))


def _dot(a, b):
    return jnp.dot(a, b, preferred_element_type=F32)


def _dot_nt(a, b):
    return lax.dot_general(a, b, (((1,), (1,)), ((), ())), preferred_element_type=F32)


def _layer_norm(z, g, b):
    mu = jnp.mean(z, axis=-1, keepdims=True)
    zc = z - mu
    var = jnp.mean(zc * zc, axis=-1, keepdims=True)
    return zc * lax.rsqrt(var + LN_EPS) * g + b


def _mod_kernel(c_ref, w_ref, b_ref, o_ref):
    s = _silu(c_ref[...]).astype(BF16)
    o_ref[0] = _dot(s, w_ref[0].astype(BF16)) + b_ref[0]


def _modulation(cc, w_mod, b_mod):
    depth, d, n6 = w_mod.shape
    tn = 1024
    return pl.pallas_call(
        _mod_kernel,
        grid=(depth, n6 // tn),
        in_specs=[
            pl.BlockSpec((MOD_ROWS, d), lambda l, j: (0, 0)),
            pl.BlockSpec((1, d, tn), lambda l, j: (l, 0, j)),
            pl.BlockSpec((1, 1, tn), lambda l, j: (l, 0, j)),
        ],
        out_specs=pl.BlockSpec((1, MOD_ROWS, tn), lambda l, j: (l, 0, j)),
        out_shape=jax.ShapeDtypeStruct((depth, MOD_ROWS, n6), F32),
        compiler_params=_cparams(("arbitrary", "arbitrary"), 3 * d * tn * 4 + (4 << 20)),
    )(cc, w_mod, b_mod.reshape(depth, 1, n6))


def _inproj_kernel(x_ref, sc_ref, sh_ref, w_ref, o_ref, xb_ref):
    @pl.when(pl.program_id(1) == 0)
    def _():
        xb_ref[...] = (x_ref[...] * (1.0 + sc_ref[0]) + sh_ref[0]).astype(BF16)

    o_ref[...] = _dot(xb_ref[...], w_ref[...].astype(BF16))


def _inproj(x, mods3, mod_row, layer, w_all, w_idx, tm=1024, tn=512):
    m, d = x.shape
    n = w_all.shape[-1]
    base = layer * MOD_ROWS
    return pl.pallas_call(
        _inproj_kernel,
        grid=(m // tm, n // tn),
        in_specs=[
            pl.BlockSpec((tm, d), lambda i, j: (i, 0)),
            pl.BlockSpec((1, 1, d), lambda i, j: (base + mod_row(i, tm), 0, 1)),
            pl.BlockSpec((1, 1, d), lambda i, j: (base + mod_row(i, tm), 0, 0)),
            pl.BlockSpec((None, d, tn), lambda i, j: (w_idx, 0, j)),
        ],
        out_specs=pl.BlockSpec((tm, tn), lambda i, j: (i, j)),
        out_shape=jax.ShapeDtypeStruct((m, n), F32),
        scratch_shapes=[pltpu.VMEM((tm, d), BF16)],
        compiler_params=_cparams(
            ("arbitrary", "arbitrary"),
            2 * tm * d * 4 + 3 * d * tn * 4 + 2 * tm * tn * 4 + tm * d * 2 + (4 << 20)),
    )(x, mods3, mods3, w_all)


def _outproj_ln_kernel(alpha, y_ref, w_ref, x_ref, gate_ref, g_ref, b_ref, o_ref, acc_ref):
    k = pl.program_id(1)

    @pl.when(k == 0)
    def _():
        acc_ref[...] = jnp.zeros_like(acc_ref)

    acc_ref[...] += _dot(y_ref[...], w_ref[...].astype(BF16))

    @pl.when(k == pl.num_programs(1) - 1)
    def _():
        z = alpha * x_ref[...] + gate_ref[0] * acc_ref[...]
        o_ref[...] = _layer_norm(z, g_ref[0], b_ref[0])


def _outproj_ln(y, w_all, w_idx, x, mods3, mod_row, layer, ln_g, ln_b, alpha, tm=512, tk=512):
    m, kdim = y.shape
    d = x.shape[1]
    base = layer * MOD_ROWS
    return pl.pallas_call(
        functools.partial(_outproj_ln_kernel, alpha),
        grid=(m // tm, kdim // tk),
        in_specs=[
            pl.BlockSpec((tm, tk), lambda i, k: (i, k)),
            pl.BlockSpec((None, tk, d), lambda i, k: (w_idx, k, 0)),
            pl.BlockSpec((tm, d), lambda i, k: (i, 0)),
            pl.BlockSpec((1, 1, d), lambda i, k: (base + mod_row(i, tm), 0, 2)),
            pl.BlockSpec((1, 1, d), lambda i, k: (layer, 0, 0)),
            pl.BlockSpec((1, 1, d), lambda i, k: (layer, 0, 0)),
        ],
        out_specs=pl.BlockSpec((tm, d), lambda i, k: (i, 0)),
        out_shape=jax.ShapeDtypeStruct((m, d), F32),
        scratch_shapes=[pltpu.VMEM((tm, d), F32)],
        compiler_params=_cparams(
            ("arbitrary", "arbitrary"),
            2 * tm * tk * 2 + 3 * tk * d * 4 + 5 * tm * d * 4 + (4 << 20)),
    )(y, w_all, x, mods3, ln_g, ln_b)


def _rope(x, cos, sin):
    half = V7X_LANES // 2
    sw = jnp.concatenate(
        [pltpu.roll(x[:, :V7X_LANES], half, 1), pltpu.roll(x[:, V7X_LANES:], half, 1)], axis=1)
    return x * cos + sw * sin


def _group_norm(o, g, b):
    mu = jnp.mean(o, axis=-1, keepdims=True)
    oc = o - mu
    var = jnp.mean(oc * oc, axis=-1, keepdims=True)
    return oc * lax.rsqrt(var + LN_EPS) * g + b


def _ret_kernel(p_slabs, dk, lg_ref, q_ref, k_ref, v_ref, gf_ref, gb_ref, cos_ref, sin_ref, s0_ref,
                gng_ref, gnb_ref, y_ref, st_ref, q_s, k_s, yacc):
    h = pl.program_id(0)
    s = pl.program_id(1)
    is_prompt = s < p_slabs
    c = RET_CHUNK
    n_chunks = q_ref.shape[0] // c

    cos = cos_ref[0]
    sin = sin_ref[0]
    q_s[...] = _rope(q_ref[...], cos, sin)
    k_s[...] = _rope(k_ref[...], cos, sin) * (dk ** -0.5)

    ri = lax.broadcasted_iota(jnp.int32, (c, c), 0)
    ci = lax.broadcasted_iota(jnp.int32, (c, c), 1)
    diff = (ri - ci).astype(F32)
    pos = lax.broadcasted_iota(jnp.int32, (c, 1), 0).astype(F32)
    gn_g = gng_ref[0]
    gn_b = gnb_ref[0]

    for direction in (0, 1):
        lg = jnp.full((1, 1), lg_ref[direction, h], F32)
        if direction == 0:
            mask = jnp.where(diff >= 0, jnp.exp(jnp.maximum(diff, 0.0) * lg), 0.0)
            q_dec = jnp.exp((pos + 1.0) * lg)
            k_dec = jnp.exp((c - 1.0 - pos) * lg)
            order = range(n_chunks)
            gate_ref = gf_ref
        else:
            mask = jnp.where(diff <= 0, jnp.exp(jnp.maximum(-diff, 0.0) * lg), 0.0)
            q_dec = jnp.exp((c - pos) * lg)
            k_dec = jnp.exp(pos * lg)
            order = range(n_chunks - 1, -1, -1)
            gate_ref = gb_ref
        c_dec = jnp.exp(c * lg)
        state = jnp.where(is_prompt, 0.0, s0_ref[0, direction, 0])
        for n, ch in enumerate(order):
            rows = pl.ds(ch * c, c)
            qc = q_s[rows, :]
            kc = k_s[rows, :]
            vb = v_ref[rows, :].astype(BF16)
            if n > 0:
                state = jnp.where(is_prompt, 0.0, state)
            inner = _dot_nt(qc.astype(BF16), kc.astype(BF16)) * mask
            o = _dot(inner.astype(BF16), vb) + _dot((qc * q_dec).astype(BF16), state.astype(BF16))
            kd_t = jnp.transpose(kc * k_dec).astype(BF16)
            state = c_dec * state + _dot(kd_t, vb)

            @pl.when(is_prompt)
            def _(state=state, ch=ch, direction=direction):
                st_ref[ch, direction, 0] = state

            contrib = _group_norm(o, gn_g, gn_b) * _silu(gate_ref[rows, :])
            if direction == 0:
                yacc[rows, :] = contrib
            else:
                y_ref[rows, :] = (yacc[rows, :] + contrib).astype(BF16)


def _retention(proj, log_g, rope_cos, rope_sin, state_ret, ret_idx, gn_g, gn_b, p_slabs, slab):
    m = proj.shape[0]
    heads, dk, dv = state_ret.shape[-3:]
    n_slabs = m // slab
    per_slab = slab // RET_CHUNK
    v_blk0 = 2 * heads * dk // dv
    p_last = p_slabs - 1

    def tbl(h, s):
        return (jnp.where(s < p_slabs, 0, 1), 0, 0)

    def s0_map(h, s):
        return (jnp.maximum(s - p_slabs, 0), ret_idx, 0, h, 0, 0)

    kernel = functools.partial(_ret_kernel, p_slabs, dk)
    return pl.pallas_call(
        kernel,
        grid=(heads, n_slabs),
        in_specs=[
            pl.BlockSpec(memory_space=pltpu.SMEM),
            pl.BlockSpec((slab, dk), lambda h, s: (s, h)),
            pl.BlockSpec((slab, dk), lambda h, s: (s, heads + h)),
            pl.BlockSpec((slab, dv), lambda h, s: (s, v_blk0 + h)),
            pl.BlockSpec((slab, dv), lambda h, s: (s, v_blk0 + heads + h)),
            pl.BlockSpec((slab, dv), lambda h, s: (s, v_blk0 + 2 * heads + h)),
            pl.BlockSpec((1, slab, dk), tbl),
            pl.BlockSpec((1, slab, dk), tbl),
            pl.BlockSpec((1, None, 2, 1, dk, dv), s0_map),
            pl.BlockSpec((1, 1, dv), lambda h, s: (h, 0, 0)),
            pl.BlockSpec((1, 1, dv), lambda h, s: (h, 0, 0)),
        ],
        out_specs=[
            pl.BlockSpec((slab, dv), lambda h, s: (s, h)),
            pl.BlockSpec((per_slab, 2, 1, dk, dv), lambda h, s: (jnp.minimum(s, p_last), 0, h, 0, 0)),
        ],
        out_shape=[
            jax.ShapeDtypeStruct((m, heads * dv), BF16),
            jax.ShapeDtypeStruct((p_slabs * per_slab, 2, heads, dk, dv), F32),
        ],
        scratch_shapes=[
            pltpu.VMEM((slab, dk), F32),
            pltpu.VMEM((slab, dk), F32),
            pltpu.VMEM((slab, dv), F32),
        ],
        compiler_params=_cparams(
            ("arbitrary", "arbitrary"),
            2 * (2 * slab * dk + 3 * slab * dv) * 4 + 4 * slab * dk * 4 + 2 * 2 * dk * dv * 4
            + 2 * slab * dv * 2 + 2 * per_slab * 2 * dk * dv * 4 + 2 * slab * dk * 4 + slab * dv * 4
            + (8 << 20)),
    )(log_g, proj, proj, proj, proj, proj, rope_cos, rope_sin, state_ret, gn_g, gn_b)


def _attn_kernel(p_slabs, seq, dh, q_ref, k_ref, v_ref, ck_ref, cv_ref, bias_ref, o_ref, nk_ref, nv_ref):
    s = pl.program_id(1)
    is_prompt = s < p_slabs
    slab = q_ref.shape[0]
    per_slab = slab // seq
    k = k_ref[...]
    v = v_ref[...]

    @pl.when(is_prompt)
    def _():
        for b in range(per_slab):
            nk_ref[b, 0, 0] = k[b * seq:(b + 1) * seq, :]
            nv_ref[b, 0, 0] = v[b * seq:(b + 1) * seq, :]

    kb = k.astype(BF16)
    vb = v.astype(BF16)
    ckb = ck_ref[0, 0, 0].astype(BF16)
    cvb = cv_ref[0, 0, 0].astype(BF16)
    qb_rows = ATTN_QB
    for qb in range(slab // qb_rows):
        rows = pl.ds(qb * qb_rows, qb_rows)
        q = (q_ref[rows, :] * (dh ** -0.5)).astype(BF16)
        ri = lax.broadcasted_iota(jnp.int32, (qb_rows, slab), 0) + qb * qb_rows
        ci = lax.broadcasted_iota(jnp.int32, (qb_rows, slab), 1)
        same_seq = (ri // seq) == (ci // seq)
        bias = jnp.where(is_prompt, jnp.where(same_seq, 0.0, NEG_INF), bias_ref[0, rows, :])
        s_loc = _dot_nt(q, kb) + bias
        s_ctx = jnp.where(is_prompt, NEG_INF, _dot_nt(q, ckb))
        mx = jnp.maximum(jnp.max(s_loc, axis=-1, keepdims=True), jnp.max(s_ctx, axis=-1, keepdims=True))
        p_loc = jnp.exp(s_loc - mx)
        p_ctx = jnp.exp(s_ctx - mx)
        denom = jnp.sum(p_loc, axis=-1, keepdims=True) + jnp.sum(p_ctx, axis=-1, keepdims=True)
        o = _dot(p_loc.astype(BF16), vb) + _dot(p_ctx.astype(BF16), cvb)
        o_ref[rows, :] = (o / denom).astype(BF16)


def _attention(proj, cache_k, cache_v, cache_idx, bias, p_slabs, slab, seq):
    m = proj.shape[0]
    heads = cache_k.shape[2]
    dh = cache_k.shape[-1]
    past = cache_k.shape[-2]
    n_slabs = m // slab
    per_slab = slab // seq
    p_last = p_slabs - 1

    def ctx_map(h, s):
        return (jnp.maximum(s - p_slabs, 0), cache_idx, h, 0, 0)

    def new_map(h, s):
        return (jnp.minimum(s, p_last), 0, h, 0, 0)

    kernel = functools.partial(_attn_kernel, p_slabs, seq, dh)
    return pl.pallas_call(
        kernel,
        grid=(heads, n_slabs),
        in_specs=[
            pl.BlockSpec((slab, dh), lambda h, s: (s, h)),
            pl.BlockSpec((slab, dh), lambda h, s: (s, heads + h)),
            pl.BlockSpec((slab, dh), lambda h, s: (s, 2 * heads + h)),
            pl.BlockSpec((1, 1, 1, past, dh), ctx_map),
            pl.BlockSpec((1, 1, 1, past, dh), ctx_map),
            pl.BlockSpec((1, slab, slab), lambda h, s: (h, 0, 0)),
        ],
        out_specs=[
            pl.BlockSpec((slab, dh), lambda h, s: (s, h)),
            pl.BlockSpec((per_slab, 1, 1, seq, dh), new_map),
            pl.BlockSpec((per_slab, 1, 1, seq, dh), new_map),
        ],
        out_shape=[
            jax.ShapeDtypeStruct((m, heads * dh), BF16),
            jax.ShapeDtypeStruct((p_slabs * per_slab, 1, heads, seq, dh), F32),
            jax.ShapeDtypeStruct((p_slabs * per_slab, 1, heads, seq, dh), F32),
        ],
        compiler_params=_cparams(
            ("arbitrary", "arbitrary"),
            2 * slab * slab * 4 + 16 * slab * dh * 4 + 8 * ATTN_QB * (slab + past) * 4 + (8 << 20)),
    )(proj, proj, proj, cache_k, cache_v, bias)


def _na_bias(rpb, rows):
    kh = min(WIN_H, rows)
    col = jnp.arange(GRID_W)
    c0 = jnp.clip(col - WIN_W // 2, 0, GRID_W - WIN_W)
    col_ok = (col[None, :] >= c0[:, None]) & (col[None, :] < c0[:, None] + WIN_W)
    dc_idx = jnp.clip(col[None, :] - col[:, None], -(WIN_W - 1), WIN_W - 1) + WIN_W - 1
    rpb_cols = jnp.where(col_ok[None, None], rpb.astype(F32)[:, :, dc_idx], NEG_INF)
    row = jnp.arange(rows)
    r0 = jnp.clip(row - kh // 2, 0, rows - kh)
    row_ok = (row[None, :] >= r0[:, None]) & (row[None, :] < r0[:, None] + kh)
    dr_idx = jnp.clip(row[None, :] - row[:, None] + WIN_H - 1, 0, 2 * WIN_H - 2)
    b = rpb_cols[:, dr_idx]
    b = jnp.where(row_ok[None, :, :, None, None], b, NEG_INF)
    n = rows * GRID_W
    return b.transpose(0, 1, 3, 2, 4).reshape(rpb.shape[0], n, n)


def _router_kernel(x_ref, sc_ref, sh_ref, wr_ref, br_ref, info_ref, cnt_ref, run_ref):
    @pl.when(pl.program_id(0) == 0)
    def _():
        run_ref[...] = jnp.zeros_like(run_ref)

    hm = x_ref[...] * (1.0 + sc_ref[0]) + sh_ref[0]
    logits = jnp.dot(hm, wr_ref[...], preferred_element_type=F32, precision=lax.Precision.HIGHEST) + br_ref[...]
    tm, width = logits.shape
    col = lax.broadcasted_iota(jnp.int32, (tm, width), 1).astype(F32)
    neg = jnp.float32(-3.0e38)

    gl = jnp.where(col < N_GROUPS, logits, neg)
    gmax = jnp.max(gl, axis=-1, keepdims=True)
    gsel = jnp.min(jnp.where(gl == gmax, col, float(width)), axis=-1, keepdims=True)
    p_g = 1.0 / jnp.sum(jnp.where(col < N_GROUPS, jnp.exp(gl - gmax), 0.0), axis=-1, keepdims=True)

    lo = N_GROUPS + N_EXP_PER_GROUP * gsel
    el = jnp.where((col >= lo) & (col < lo + N_EXP_PER_GROUP), logits, neg)
    e1 = jnp.max(el, axis=-1, keepdims=True)
    i1 = jnp.min(jnp.where(el == e1, col, float(width)), axis=-1, keepdims=True)
    el2 = jnp.where(col == i1, neg, el)
    e2 = jnp.max(el2, axis=-1, keepdims=True)
    i2 = jnp.min(jnp.where(el2 == e2, col, float(width)), axis=-1, keepdims=True)
    t = jnp.exp(e2 - e1)
    w1 = 1.0 / (1.0 + t)
    c1 = p_g * w1
    c2 = p_g * (t * w1)
    id1 = i1 - N_GROUPS
    id2 = i2 - N_GROUPS

    onehot = jnp.where((col == id1) | (col == id2), 1.0, 0.0)
    ri = lax.broadcasted_iota(jnp.int32, (tm, tm), 0)
    ci = lax.broadcasted_iota(jnp.int32, (tm, tm), 1)
    tri = jnp.where(ri > ci, 1.0, 0.0).astype(BF16)
    before = _dot(tri, onehot.astype(BF16)) + run_ref[...]
    r1 = jnp.sum(jnp.where(col == id1, before, 0.0), axis=-1, keepdims=True)
    r2 = jnp.sum(jnp.where(col == id2, before, 0.0), axis=-1, keepdims=True)
    run_ref[...] += jnp.sum(onehot, axis=0, keepdims=True)
    cnt_ref[...] = run_ref[...]

    info = jnp.where(col == 0, id1, 0.0)
    info = jnp.where(col == 1, id2, info)
    info = jnp.where(col == 2, c1, info)
    info = jnp.where(col == 3, c2, info)
    info = jnp.where(col == 4, r1, info)
    info = jnp.where(col == 5, r2, info)
    info_ref[...] = info


def _router(x, mods3, mod_row, layer, wr, br, tm=512):
    m, d = x.shape
    base = layer * MOD_ROWS
    return pl.pallas_call(
        _router_kernel,
        grid=(m // tm,),
        in_specs=[
            pl.BlockSpec((tm, d), lambda i: (i, 0)),
            pl.BlockSpec((1, 1, d), lambda i: (base + mod_row(i, tm), 0, 4)),
            pl.BlockSpec((1, 1, d), lambda i: (base + mod_row(i, tm), 0, 3)),
            pl.BlockSpec((d, V7X_LANES), lambda i: (0, 0)),
            pl.BlockSpec((1, V7X_LANES), lambda i: (0, 0)),
        ],
        out_specs=[
            pl.BlockSpec((tm, V7X_LANES), lambda i: (i, 0)),
            pl.BlockSpec((1, V7X_LANES), lambda i: (0, 0)),
        ],
        out_shape=[
            jax.ShapeDtypeStruct((m, V7X_LANES), F32),
            jax.ShapeDtypeStruct((1, V7X_LANES), F32),
        ],
        scratch_shapes=[pltpu.VMEM((1, V7X_LANES), F32)],
        compiler_params=_cparams(("arbitrary",), 4 * tm * d * 4 + 2 * d * V7X_LANES * 4 + (8 << 20)),
    )(x, mods3, mods3, wr, br)


def _pos_kernel(info_ref, off_ref, pos_ref):
    info = info_ref[...]
    tm, width = info.shape
    col = lax.broadcasted_iota(jnp.int32, (tm, width), 1).astype(F32)
    off = off_ref[...]
    p1 = jnp.sum(jnp.where(col == info[:, 0:1], off, 0.0), axis=-1, keepdims=True) + info[:, 4:5]
    p2 = jnp.sum(jnp.where(col == info[:, 1:2], off, 0.0), axis=-1, keepdims=True) + info[:, 5:6]
    pm = jnp.where(col == 0, p1, jnp.where(col == 1, p2, 0.0))
    pos_ref[0] = jnp.transpose(pm)[:8, :].astype(jnp.int32)


def _positions(info, row_off, tm):
    m = info.shape[0]
    return pl.pallas_call(
        _pos_kernel,
        grid=(m // tm,),
        in_specs=[
            pl.BlockSpec((tm, V7X_LANES), lambda i: (i, 0)),
            pl.BlockSpec((1, V7X_LANES), lambda i: (0, 0)),
        ],
        out_specs=pl.BlockSpec((1, 8, tm), lambda i: (i, 0, 0)),
        out_shape=jax.ShapeDtypeStruct((m // tm, 8, tm), jnp.int32),
        compiler_params=_cparams(("arbitrary",), 16 << 20),
    )(info, row_off)


def _dispatch_kernel(tm, pad_bits, ps_ref, pl_ref, pos_ref, x_ref, sc_ref, sh_ref, xs_hbm,
                     hbuf, zbuf, sems, zsem):
    i = pl.program_id(0)
    n_steps = pl.num_programs(0)
    n_experts = ps_ref.shape[0] - 1
    zrows = zbuf.shape[0]
    max_tail = (xs_hbm.shape[0] - 2 * x_ref.shape[0] * n_steps) // zrows

    def fill_copies(do):
        for e in range(n_experts):
            start = ps_ref[e]
            rem = pl_ref[e]
            for r in range(SUBLANES - 1):
                @pl.when(r < (rem & (SUBLANES - 1)))
                def _(start=start, r=r):
                    do(pltpu.make_async_copy(zbuf.at[pl.ds(0, 1)], xs_hbm.at[pl.ds(start + r, 1)], zsem))

            off = start + rem
            for bit in pad_bits:
                off = off - (rem & bit)

                @pl.when((rem & bit) != 0)
                def _(off=off, bit=bit):
                    dst = xs_hbm.at[pl.ds(pl.multiple_of(off, SUBLANES), bit)]
                    do(pltpu.make_async_copy(zbuf.at[pl.ds(0, bit)], dst, zsem))
        for t in range(max_tail):
            @pl.when(t < pl_ref[n_experts])
            def _(t=t):
                dst = xs_hbm.at[pl.ds(pl.multiple_of(ps_ref[n_experts] + t * zrows, SUBLANES), zrows)]
                do(pltpu.make_async_copy(zbuf, dst, zsem))

    @pl.when(i == 0)
    def _():
        zbuf[...] = jnp.zeros_like(zbuf)
        fill_copies(lambda cp: cp.start())

    def scatter_start(sub):
        for r in range(tm):
            for slot in range(2):
                pltpu.make_async_copy(hbuf.at[sub, pl.ds(r, 1)],
                                      xs_hbm.at[pl.ds(pos_ref[0, slot, sub * tm + r], 1)], sems.at[sub]).start()

    def scatter_wait(sub):
        for _ in range(2):
            pltpu.make_async_copy(hbuf.at[sub], xs_hbm.at[pl.ds(0, tm)], sems.at[sub]).wait()

    for sub in range(2):
        rows = pl.ds(sub * tm, tm)
        hbuf[sub] = x_ref[rows, :] * (1.0 + sc_ref[0]) + sh_ref[0]
        scatter_start(sub)
        if sub == 0:
            @pl.when(i > 0)
            def _():
                scatter_wait(1)
        else:
            scatter_wait(0)

    @pl.when(i == n_steps - 1)
    def _():
        scatter_wait(1)
        fill_copies(lambda cp: cp.wait())


def _dispatch(x, posT, pad_start, pad_len, mods3, mod_row, layer, n_rows, tm, moe_tm):
    m, d = x.shape
    base = layer * MOD_ROWS
    step = 2 * tm
    pad_bits = tuple(1 << b for b in range(moe_tm.bit_length() - 2, SUBLANES.bit_length() - 2, -1))
    grid_spec = pltpu.PrefetchScalarGridSpec(
        num_scalar_prefetch=2,
        grid=(m // step,),
        in_specs=[
            pl.BlockSpec((1, 8, step), lambda i, ps, pn: (i, 0, 0), memory_space=pltpu.SMEM),
            pl.BlockSpec((step, d), lambda i, ps, pn: (i, 0)),
            pl.BlockSpec((1, 1, d), lambda i, ps, pn: (base + mod_row(i, step), 0, 4)),
            pl.BlockSpec((1, 1, d), lambda i, ps, pn: (base + mod_row(i, step), 0, 3)),
        ],
        out_specs=pl.BlockSpec(memory_space=pl.ANY),
        scratch_shapes=[
            pltpu.VMEM((2, tm, d), F32),
            pltpu.VMEM((moe_tm // 2, d), F32),
            pltpu.SemaphoreType.DMA((2,)),
            pltpu.SemaphoreType.DMA(()),
        ],
    )
    return pl.pallas_call(
        functools.partial(_dispatch_kernel, tm, pad_bits),
        grid_spec=grid_spec,
        out_shape=jax.ShapeDtypeStruct((n_rows, d), F32),
        compiler_params=_cparams(("arbitrary",), 2 * step * d * 4 + 3 * tm * d * 4 + (8 << 20)),
    )(pad_start, pad_len, posT, x, mods3, mods3)


def _moe_kernel(te_ref, nu_ref, xs_ref, wg_ref, wu_ref, wd_ref, ys_ref, wg_b, wu_b, wd_b):
    i = pl.program_id(0)
    used = i < nu_ref[0]

    @pl.when(jnp.logical_not(used))
    def _():
        ys_ref[...] = jnp.zeros_like(ys_ref)

    @pl.when(used & ((i == 0) | (te_ref[i] != te_ref[jnp.maximum(i - 1, 0)])))
    def _():
        wg_b[...] = wg_ref[...].astype(BF16)
        wu_b[...] = wu_ref[...].astype(BF16)
        wd_b[...] = wd_ref[...].astype(BF16)

    @pl.when(used)
    def _():
        x = xs_ref[...].astype(BF16)
        g = _dot(x, wg_b[...])
        u = _dot(x, wu_b[...])
        hid = (_silu(g) * u).astype(BF16)
        ys_ref[...] = _dot(hid, wd_b[...])


def _moe_experts(xs, tile_expert, n_used, w_gate, w_up, w_down, layer, tm):
    n_rows, d = xs.shape
    f = w_gate.shape[-1]
    n_tiles = n_rows // tm

    def w_map(i, te, nu):
        return (layer, te[i], 0, 0)

    grid_spec = pltpu.PrefetchScalarGridSpec(
        num_scalar_prefetch=2,
        grid=(n_tiles,),
        in_specs=[
            pl.BlockSpec((tm, d), lambda i, te, nu: (jnp.minimum(i, nu[0] - 1), 0)),
            pl.BlockSpec((None, None, d, f), w_map),
            pl.BlockSpec((None, None, d, f), w_map),
            pl.BlockSpec((None, None, f, d), w_map),
        ],
        out_specs=pl.BlockSpec((tm, d), lambda i, te, nu: (i, 0)),
        scratch_shapes=[pltpu.VMEM((d, f), BF16), pltpu.VMEM((d, f), BF16), pltpu.VMEM((f, d), BF16)],
    )
    return pl.pallas_call(
        _moe_kernel,
        grid_spec=grid_spec,
        out_shape=jax.ShapeDtypeStruct((n_rows, d), F32),
        compiler_params=_cparams(("arbitrary",), 6 * d * f * 4 + 3 * d * f * 2 + 6 * tm * d * 4 + (8 << 20)),
    )(tile_expert, n_used, xs, w_gate, w_up, w_down)


def _combine_ln_kernel(alpha, tm, pos_ref, nxt_ref, ys_hbm, info_ref, x_ref, gate_ref, g_ref, b_ref, o_ref,
                       buf, sems):
    i = pl.program_id(0)
    n_steps = pl.num_programs(0)

    def gather_start(sub, p_ref):
        for r in range(tm):
            for slot in range(2):
                pltpu.make_async_copy(ys_hbm.at[pl.ds(p_ref[0, slot, sub * tm + r], 1)],
                                      buf.at[sub, slot, pl.ds(r, 1)], sems.at[sub]).start()

    def gather_wait(sub):
        for slot in range(2):
            pltpu.make_async_copy(ys_hbm.at[pl.ds(0, tm)], buf.at[sub, slot], sems.at[sub]).wait()

    def finish(sub):
        rows = pl.ds(sub * tm, tm)
        info = info_ref[rows, :]
        y = info[:, 2:3] * buf[sub, 0] + info[:, 3:4] * buf[sub, 1]
        z = alpha * x_ref[rows, :] + gate_ref[0] * y
        o_ref[rows, :] = _layer_norm(z, g_ref[0], b_ref[0])

    @pl.when(i == 0)
    def _():
        gather_start(0, pos_ref)

    gather_start(1, pos_ref)
    gather_wait(0)
    finish(0)

    @pl.when(i + 1 < n_steps)
    def _():
        gather_start(0, nxt_ref)

    gather_wait(1)
    finish(1)


def _combine_ln(ys, posT, info, x, mods3, mod_row, layer, ln_g, ln_b, alpha, tm):
    m, d = x.shape
    step = 2 * tm
    n_steps = m // step
    base = layer * MOD_ROWS
    return pl.pallas_call(
        functools.partial(_combine_ln_kernel, alpha, tm),
        grid=(n_steps,),
        in_specs=[
            pl.BlockSpec((1, 8, step), lambda i: (i, 0, 0), memory_space=pltpu.SMEM),
            pl.BlockSpec((1, 8, step), lambda i: (jnp.minimum(i + 1, n_steps - 1), 0, 0), memory_space=pltpu.SMEM),
            pl.BlockSpec(memory_space=pl.ANY),
            pl.BlockSpec((step, V7X_LANES), lambda i: (i, 0)),
            pl.BlockSpec((step, d), lambda i: (i, 0)),
            pl.BlockSpec((1, 1, d), lambda i: (base + mod_row(i, step), 0, 5)),
            pl.BlockSpec((1, 1, d), lambda i: (layer, 0, 0)),
            pl.BlockSpec((1, 1, d), lambda i: (layer, 0, 0)),
        ],
        out_specs=pl.BlockSpec((step, d), lambda i: (i, 0)),
        out_shape=jax.ShapeDtypeStruct((m, d), F32),
        scratch_shapes=[pltpu.VMEM((2, 2, tm, d), F32), pltpu.SemaphoreType.DMA((2,))],
        compiler_params=_cparams(("arbitrary",), 4 * step * d * 4 + 4 * tm * d * 4 + 6 * tm * d * 4 + (8 << 20)),
    )(posT, posT, ys, info, x, mods3, ln_g, ln_b)


def _hier_moe_ln(x, mods3, mod_row, layer, w_rg, b_rg, w_re, b_re, w_gate, w_up, w_down, ln_g, ln_b, alpha):
    m, d = x.shape
    n_e = N_EXPERTS
    pad = V7X_LANES - N_GROUPS - n_e
    wr = jnp.concatenate(
        [w_rg[layer], w_re[layer].transpose(1, 0, 2).reshape(d, n_e), jnp.zeros((d, pad), F32)], axis=1)
    br = jnp.concatenate([b_rg[layer], b_re[layer].reshape(n_e), jnp.zeros((pad,), F32)])[None, :]
    info, cnt = _router(x, mods3, mod_row, layer, wr, br)

    tm = MOE_TM
    n_tiles = 2 * m // tm + n_e
    counts = cnt[0, :n_e].astype(jnp.int32)
    tiles_e = (counts + tm - 1) // tm
    tile_end = jnp.cumsum(tiles_e)
    n_used = tile_end[-1]
    row_off = (tile_end - tiles_e) * tm
    tile_ids = jnp.minimum(jnp.arange(n_tiles, dtype=jnp.int32), n_used - 1)
    tile_expert = jnp.sum((tile_ids[:, None] >= tile_end[None, :]).astype(jnp.int32), axis=1)
    zrows = tm // 2
    pad_start = jnp.concatenate([row_off + counts, (n_used * tm)[None]])
    pad_len = jnp.concatenate([tiles_e * tm - counts, ((n_tiles - n_used) * (tm // zrows))[None]])
    row_off_f = jnp.concatenate([row_off.astype(F32), jnp.zeros((V7X_LANES - n_e,), F32)])[None, :]

    ct = COMB_TM
    posT = _positions(info, row_off_f, 2 * ct)
    xs = _dispatch(x, posT, pad_start, pad_len, mods3, mod_row, layer, n_tiles * tm, ct, tm)
    ys = _moe_experts(xs, tile_expert, n_used.reshape(1), w_gate, w_up, w_down, layer, tm)
    return _combine_ln(ys, posT, info, x, mods3, mod_row, layer, ln_g, ln_b, alpha, ct)


def kernel(x_prompt, x_sample, state_ret, cache_na_k, cache_na_v, c, c_ctx, w_mod, b_mod, ln1_g, ln1_b, ln2_g, ln2_b, w_ret_in, ret_decay_logit, ret_gn_g, ret_gn_b, w_ret_out, w_na_in, na_rpb, w_na_out, w_rg, b_rg, w_re, b_re, w_gate, w_up, w_down):
    bp, seq, d = x_prompt.shape
    bs, dec_seq, _ = x_sample.shape
    depth = w_mod.shape[0]
    slab = dec_seq
    assert slab % seq == 0 and (bp * seq) % slab == 0 and bs + 1 <= MOD_ROWS
    p_rows = bp * seq
    p_slabs = p_rows // slab
    alpha = (2.0 * depth) ** 0.25

    def mod_row(i, tm):
        start = i * tm
        return jnp.where(start < p_rows, 0, 1 + (start - p_rows) // dec_seq)

    cc = jnp.concatenate([c_ctx[None, :], c, jnp.zeros((MOD_ROWS - 1 - bs, d), F32)], axis=0)
    mods = _modulation(cc, w_mod, b_mod)
    mods3 = mods.reshape(depth * MOD_ROWS, 1, 6 * d)
    x = jnp.concatenate([x_prompt.reshape(p_rows, d), x_sample.reshape(bs * dec_seq, d)], axis=0)

    ln1_g3, ln1_b3 = ln1_g.reshape(depth, 1, d), ln1_b.reshape(depth, 1, d)
    ln2_g3, ln2_b3 = ln2_g.reshape(depth, 1, d), ln2_b.reshape(depth, 1, d)

    new_ret, new_k, new_v = [], [], []
    for l in range(depth):
        j = l // 2
        if l % 2 == 0:
            dk = state_ret.shape[-2]
            proj = _inproj(x, mods3, mod_row, l, w_ret_in, j)
            log_g = -jax.nn.softplus(-ret_decay_logit[j].astype(F32))
            t = jnp.arange(slab)
            nf = dk // 4
            inv_freq = ROPE_BASE ** (-jnp.arange(nf, dtype=F32) / nf)
            ang_r = (t // GRID_W).astype(F32)[:, None] * inv_freq
            ang_c = (t % GRID_W).astype(F32)[:, None] * inv_freq
            cos = jnp.concatenate([jnp.cos(ang_r)] * 2 + [jnp.cos(ang_c)] * 2, axis=1)
            sin = jnp.concatenate([-jnp.sin(ang_r), jnp.sin(ang_r), -jnp.sin(ang_c), jnp.sin(ang_c)], axis=1)
            rope_cos = jnp.stack([jnp.ones_like(cos), cos])
            rope_sin = jnp.stack([jnp.zeros_like(sin), sin])
            y, st = _retention(proj, log_g, rope_cos, rope_sin, state_ret, j, ret_gn_g[j][:, None, :],
                               ret_gn_b[j][:, None, :], p_slabs, slab)
            new_ret.append(st)
            x = _outproj_ln(y, w_ret_out, j, x, mods3, mod_row, l, ln1_g3, ln1_b3, alpha)
        else:
            proj = _inproj(x, mods3, mod_row, l, w_na_in, j)
            bias = _na_bias(na_rpb[j], dec_seq // GRID_W)
            o, nk, nv = _attention(proj, cache_na_k, cache_na_v, j, bias, p_slabs, slab, seq)
            new_k.append(nk[:, 0])
            new_v.append(nv[:, 0])
            x = _outproj_ln(o, w_na_out, j, x, mods3, mod_row, l, ln1_g3, ln1_b3, alpha)
        x = _hier_moe_ln(x, mods3, mod_row, l, w_rg, b_rg, w_re, b_re, w_gate, w_up, w_down,
                         ln2_g3, ln2_b3, alpha)

    y_prompt = x[:p_rows].reshape(bp, seq, d)
    y_sample = x[p_rows:].reshape(bs, dec_seq, d)
    return (y_prompt, y_sample, jnp.stack(new_ret, axis=1), jnp.stack(new_k, axis=1), jnp.stack(new_v, axis=1))
```

```python
import functools

import jax
import jax.numpy as jnp
from jax import lax
from jax.experimental import pallas as pl
from jax.experimental.pallas import tpu as pltpu

F32 = jnp.float32
BF16 = jnp.bfloat16

GRID_W = 64
WIN_H = 8
WIN_W = 16
N_GROUPS = 4
N_EXP_PER_GROUP = 8
N_EXPERTS = N_GROUPS * N_EXP_PER_GROUP
ROPE_BASE = 10000.0
LN_EPS = 1e-5
NEG_INF = -1e30

V7X_VMEM_BYTES = 64 * 1024 * 1024
V7X_LANES = 128
SUBLANES = 8
MOD_ROWS = 8

RET_CHUNK = 256
ATTN_QB = 256
MOE_TM = 256
COMB_TM = 256


def _cparams(sem, vmem_bytes):
    limit = int(min(max(vmem_bytes, 16 * 1024 * 1024), V7X_VMEM_BYTES - 6 * 1024 * 1024))
    return pltpu.CompilerParams(dimension_semantics=sem, vmem_limit_bytes=limit)


def _silu(x):
    return x / (1.0 + jnp.exp(-x))


def _dot(a, b):
    return jnp.dot(a, b, preferred_element_type=F32)


def _dot_nt(a, b):
    return lax.dot_general(a, b, (((1,), (1,)), ((), ())), preferred_element_type=F32)


def _layer_norm(z, g, b):
    mu = jnp.mean(z, axis=-1, keepdims=True)
    zc = z - mu
    var = jnp.mean(zc * zc, axis=-1, keepdims=True)
    return zc * lax.rsqrt(var + LN_EPS) * g + b


def _mod_kernel(c_ref, w_ref, b_ref, o_ref):
    s = _silu(c_ref[...]).astype(BF16)
    o_ref[0] = _dot(s, w_ref[0].astype(BF16)) + b_ref[0]


def _modulation(cc, w_mod, b_mod):
    depth, d, n6 = w_mod.shape
    tn = 1024
    return pl.pallas_call(
        _mod_kernel,
        grid=(depth, n6 // tn),
        in_specs=[
            pl.BlockSpec((MOD_ROWS, d), lambda l, j: (0, 0)),
            pl.BlockSpec((1, d, tn), lambda l, j: (l, 0, j)),
            pl.BlockSpec((1, 1, tn), lambda l, j: (l, 0, j)),
        ],
        out_specs=pl.BlockSpec((1, MOD_ROWS, tn), lambda l, j: (l, 0, j)),
        out_shape=jax.ShapeDtypeStruct((depth, MOD_ROWS, n6), F32),
        compiler_params=_cparams(("arbitrary", "arbitrary"), 3 * d * tn * 4 + (4 << 20)),
    )(cc, w_mod, b_mod.reshape(depth, 1, n6))


def _inproj_kernel(x_ref, sc_ref, sh_ref, w_ref, o_ref, xb_ref):
    @pl.when(pl.program_id(1) == 0)
    def _():
        xb_ref[...] = (x_ref[...] * (1.0 + sc_ref[0]) + sh_ref[0]).astype(BF16)

    o_ref[...] = _dot(xb_ref[...], w_ref[...].astype(BF16))


def _inproj(x, mods3, mod_row, layer, w_all, w_idx, tm=1024, tn=512):
    m, d = x.shape
    n = w_all.shape[-1]
    base = layer * MOD_ROWS
    return pl.pallas_call(
        _inproj_kernel,
        grid=(m // tm, n // tn),
        in_specs=[
            pl.BlockSpec((tm, d), lambda i, j: (i, 0)),
            pl.BlockSpec((1, 1, d), lambda i, j: (base + mod_row(i, tm), 0, 1)),
            pl.BlockSpec((1, 1, d), lambda i, j: (base + mod_row(i, tm), 0, 0)),
            pl.BlockSpec((None, d, tn), lambda i, j: (w_idx, 0, j)),
        ],
        out_specs=pl.BlockSpec((tm, tn), lambda i, j: (i, j)),
        out_shape=jax.ShapeDtypeStruct((m, n), F32),
        scratch_shapes=[pltpu.VMEM((tm, d), BF16)],
        compiler_params=_cparams(
            ("arbitrary", "arbitrary"),
            2 * tm * d * 4 + 3 * d * tn * 4 + 2 * tm * tn * 4 + tm * d * 2 + (4 << 20)),
    )(x, mods3, mods3, w_all)


def _outproj_ln_kernel(alpha, y_ref, w_ref, x_ref, gate_ref, g_ref, b_ref, o_ref, acc_ref):
    k = pl.program_id(1)

    @pl.when(k == 0)
    def _():
        acc_ref[...] = jnp.zeros_like(acc_ref)

    acc_ref[...] += _dot(y_ref[...], w_ref[...].astype(BF16))

    @pl.when(k == pl.num_programs(1) - 1)
    def _():
        z = alpha * x_ref[...] + gate_ref[0] * acc_ref[...]
        o_ref[...] = _layer_norm(z, g_ref[0], b_ref[0])


def _outproj_ln(y, w_all, w_idx, x, mods3, mod_row, layer, ln_g, ln_b, alpha, tm=512, tk=512):
    m, kdim = y.shape
    d = x.shape[1]
    base = layer * MOD_ROWS
    return pl.pallas_call(
        functools.partial(_outproj_ln_kernel, alpha),
        grid=(m // tm, kdim // tk),
        in_specs=[
            pl.BlockSpec((tm, tk), lambda i, k: (i, k)),
            pl.BlockSpec((None, tk, d), lambda i, k: (w_idx, k, 0)),
            pl.BlockSpec((tm, d), lambda i, k: (i, 0)),
            pl.BlockSpec((1, 1, d), lambda i, k: (base + mod_row(i, tm), 0, 2)),
            pl.BlockSpec((1, 1, d), lambda i, k: (layer, 0, 0)),
            pl.BlockSpec((1, 1, d), lambda i, k: (layer, 0, 0)),
        ],
        out_specs=pl.BlockSpec((tm, d), lambda i, k: (i, 0)),
        out_shape=jax.ShapeDtypeStruct((m, d), F32),
        scratch_shapes=[pltpu.VMEM((tm, d), F32)],
        compiler_params=_cparams(
            ("arbitrary", "arbitrary"),
            2 * tm * tk * 2 + 3 * tk * d * 4 + 5 * tm * d * 4 + (4 << 20)),
    )(y, w_all, x, mods3, ln_g, ln_b)


def _rope(x, cos, sin):
    half = V7X_LANES // 2
    sw = jnp.concatenate(
        [pltpu.roll(x[:, :V7X_LANES], half, 1), pltpu.roll(x[:, V7X_LANES:], half, 1)], axis=1)
    return x * cos + sw * sin


def _group_norm(o, g, b):
    mu = jnp.mean(o, axis=-1, keepdims=True)
    oc = o - mu
    var = jnp.mean(oc * oc, axis=-1, keepdims=True)
    return oc * lax.rsqrt(var + LN_EPS) * g + b


def _ret_kernel(p_slabs, dk, lg_ref, q_ref, k_ref, v_ref, gf_ref, gb_ref, cos_ref, sin_ref, s0_ref,
                gng_ref, gnb_ref, y_ref, st_ref, q_s, k_s, yacc):
    h = pl.program_id(0)
    s = pl.program_id(1)
    is_prompt = s < p_slabs
    c = RET_CHUNK
    n_chunks = q_ref.shape[0] // c

    cos = cos_ref[0]
    sin = sin_ref[0]
    q_s[...] = _rope(q_ref[...], cos, sin)
    k_s[...] = _rope(k_ref[...], cos, sin) * (dk ** -0.5)

    ri = lax.broadcasted_iota(jnp.int32, (c, c), 0)
    ci = lax.broadcasted_iota(jnp.int32, (c, c), 1)
    diff = (ri - ci).astype(F32)
    pos = lax.broadcasted_iota(jnp.int32, (c, 1), 0).astype(F32)
    gn_g = gng_ref[0]
    gn_b = gnb_ref[0]

    for direction in (0, 1):
        lg = jnp.full((1, 1), lg_ref[direction, h], F32)
        if direction == 0:
            mask = jnp.where(diff >= 0, jnp.exp(jnp.maximum(diff, 0.0) * lg), 0.0)
            q_dec = jnp.exp((pos + 1.0) * lg)
            k_dec = jnp.exp((c - 1.0 - pos) * lg)
            order = range(n_chunks)
            gate_ref = gf_ref
        else:
            mask = jnp.where(diff <= 0, jnp.exp(jnp.maximum(-diff, 0.0) * lg), 0.0)
            q_dec = jnp.exp((c - pos) * lg)
            k_dec = jnp.exp(pos * lg)
            order = range(n_chunks - 1, -1, -1)
            gate_ref = gb_ref
        c_dec = jnp.exp(c * lg)
        state = jnp.where(is_prompt, 0.0, s0_ref[0, direction, 0])
        for n, ch in enumerate(order):
            rows = pl.ds(ch * c, c)
            qc = q_s[rows, :]
            kc = k_s[rows, :]
            vb = v_ref[rows, :].astype(BF16)
            if n > 0:
                state = jnp.where(is_prompt, 0.0, state)
            inner = _dot_nt(qc.astype(BF16), kc.astype(BF16)) * mask
            o = _dot(inner.astype(BF16), vb) + _dot((qc * q_dec).astype(BF16), state.astype(BF16))
            kd_t = jnp.transpose(kc * k_dec).astype(BF16)
            state = c_dec * state + _dot(kd_t, vb)

            @pl.when(is_prompt)
            def _(state=state, ch=ch, direction=direction):
                st_ref[ch, direction, 0] = state

            contrib = _group_norm(o, gn_g, gn_b) * _silu(gate_ref[rows, :])
            if direction == 0:
                yacc[rows, :] = contrib
            else:
                y_ref[rows, :] = (yacc[rows, :] + contrib).astype(BF16)


def _retention(proj, log_g, rope_cos, rope_sin, state_ret, ret_idx, gn_g, gn_b, p_slabs, slab):
    m = proj.shape[0]
    heads, dk, dv = state_ret.shape[-3:]
    n_slabs = m // slab
    per_slab = slab // RET_CHUNK
    v_blk0 = 2 * heads * dk // dv
    p_last = p_slabs - 1

    def tbl(h, s):
        return (jnp.where(s < p_slabs, 0, 1), 0, 0)

    def s0_map(h, s):
        return (jnp.maximum(s - p_slabs, 0), ret_idx, 0, h, 0, 0)

    kernel = functools.partial(_ret_kernel, p_slabs, dk)
    return pl.pallas_call(
        kernel,
        grid=(heads, n_slabs),
        in_specs=[
            pl.BlockSpec(memory_space=pltpu.SMEM),
            pl.BlockSpec((slab, dk), lambda h, s: (s, h)),
            pl.BlockSpec((slab, dk), lambda h, s: (s, heads + h)),
            pl.BlockSpec((slab, dv), lambda h, s: (s, v_blk0 + h)),
            pl.BlockSpec((slab, dv), lambda h, s: (s, v_blk0 + heads + h)),
            pl.BlockSpec((slab, dv), lambda h, s: (s, v_blk0 + 2 * heads + h)),
            pl.BlockSpec((1, slab, dk), tbl),
            pl.BlockSpec((1, slab, dk), tbl),
            pl.BlockSpec((1, None, 2, 1, dk, dv), s0_map),
            pl.BlockSpec((1, 1, dv), lambda h, s: (h, 0, 0)),
            pl.BlockSpec((1, 1, dv), lambda h, s: (h, 0, 0)),
        ],
        out_specs=[
            pl.BlockSpec((slab, dv), lambda h, s: (s, h)),
            pl.BlockSpec((per_slab, 2, 1, dk, dv), lambda h, s: (jnp.minimum(s, p_last), 0, h, 0, 0)),
        ],
        out_shape=[
            jax.ShapeDtypeStruct((m, heads * dv), BF16),
            jax.ShapeDtypeStruct((p_slabs * per_slab, 2, heads, dk, dv), F32),
        ],
        scratch_shapes=[
            pltpu.VMEM((slab, dk), F32),
            pltpu.VMEM((slab, dk), F32),
            pltpu.VMEM((slab, dv), F32),
        ],
        compiler_params=_cparams(
            ("arbitrary", "arbitrary"),
            2 * (2 * slab * dk + 3 * slab * dv) * 4 + 4 * slab * dk * 4 + 2 * 2 * dk * dv * 4
            + 2 * slab * dv * 2 + 2 * per_slab * 2 * dk * dv * 4 + 2 * slab * dk * 4 + slab * dv * 4
            + (8 << 20)),
    )(log_g, proj, proj, proj, proj, proj, rope_cos, rope_sin, state_ret, gn_g, gn_b)


def _na_window(rq, rows):
    kh = min(WIN_H, rows)
    return min(max(rq - kh // 2, 0), rows - kh), kh


def _attn_kernel(p_slabs, seq, dh, q_ref, k_ref, v_ref, ck_ref, cv_ref, strip_ref, o_ref, nk_ref, nv_ref):
    s = pl.program_id(1)
    is_prompt = s < p_slabs
    slab = q_ref.shape[0]
    scale = dh ** -0.5

    @pl.when(is_prompt)
    def _():
        for b in range(slab // seq):
            rows = pl.ds(b * seq, seq)
            k = k_ref[rows, :]
            v = v_ref[rows, :]
            nk_ref[b, 0, 0] = k
            nv_ref[b, 0, 0] = v
            q = (q_ref[rows, :] * scale).astype(BF16)
            sc = _dot_nt(q, k.astype(BF16))
            p = jnp.exp(sc - jnp.max(sc, axis=-1, keepdims=True))
            denom = jnp.sum(p, axis=-1, keepdims=True)
            o_ref[rows, :] = (_dot(p.astype(BF16), v.astype(BF16)) / denom).astype(BF16)

    @pl.when(jnp.logical_not(is_prompt))
    def _():
        grid_rows = slab // GRID_W
        rows_per_block = ATTN_QB // GRID_W
        ckb = ck_ref[0, 0, 0].astype(BF16)
        cvb = cv_ref[0, 0, 0].astype(BF16)
        for qb in range(slab // ATTN_QB):
            rq0 = qb * rows_per_block
            ka = _na_window(rq0, grid_rows)[0] // 2 * 2
            last0, kh = _na_window(rq0 + rows_per_block - 1, grid_rows)
            kb = -((last0 + kh) // -2) * 2
            n_keys = (kb - ka) * GRID_W
            keys = pl.ds(ka * GRID_W, n_keys)
            pieces = []
            for i in range(rows_per_block):
                rq = rq0 + i
                r0, kh = _na_window(rq, grid_rows)
                first = ka - rq + WIN_H - 1
                piece = strip_ref[0, first % 2, :, pl.ds((first - first % 2) * GRID_W, n_keys)]
                if r0 != ka or r0 + kh != kb:
                    key_row = ka + (lax.broadcasted_iota(jnp.int32, piece.shape, 1) // GRID_W)
                    piece = jnp.where((key_row >= r0) & (key_row < r0 + kh), piece, NEG_INF)
                pieces.append(piece)
            bias = jnp.concatenate(pieces, axis=0)
            rows = pl.ds(qb * ATTN_QB, ATTN_QB)
            q = (q_ref[rows, :] * scale).astype(BF16)
            s_loc = _dot_nt(q, k_ref[keys, :].astype(BF16)) + bias
            s_ctx = _dot_nt(q, ckb)
            mx = jnp.maximum(jnp.max(s_loc, axis=-1, keepdims=True), jnp.max(s_ctx, axis=-1, keepdims=True))
            p_loc = jnp.exp(s_loc - mx)
            p_ctx = jnp.exp(s_ctx - mx)
            denom = jnp.sum(p_loc, axis=-1, keepdims=True) + jnp.sum(p_ctx, axis=-1, keepdims=True)
            o = _dot(p_loc.astype(BF16), v_ref[keys, :].astype(BF16)) + _dot(p_ctx.astype(BF16), cvb)
            o_ref[rows, :] = (o / denom).astype(BF16)


def _attention(proj, cache_k, cache_v, cache_idx, strips, p_slabs, slab, seq):
    m = proj.shape[0]
    heads = cache_k.shape[2]
    dh = cache_k.shape[-1]
    past = cache_k.shape[-2]
    n_slabs = m // slab
    per_slab = slab // seq
    p_last = p_slabs - 1

    def ctx_map(h, s):
        return (jnp.maximum(s - p_slabs, 0), cache_idx, h, 0, 0)

    def new_map(h, s):
        return (jnp.minimum(s, p_last), 0, h, 0, 0)

    kernel = functools.partial(_attn_kernel, p_slabs, seq, dh)
    return pl.pallas_call(
        kernel,
        grid=(heads, n_slabs),
        in_specs=[
            pl.BlockSpec((slab, dh), lambda h, s: (s, h)),
            pl.BlockSpec((slab, dh), lambda h, s: (s, heads + h)),
            pl.BlockSpec((slab, dh), lambda h, s: (s, 2 * heads + h)),
            pl.BlockSpec((1, 1, 1, past, dh), ctx_map),
            pl.BlockSpec((1, 1, 1, past, dh), ctx_map),
            pl.BlockSpec((1,) + strips.shape[1:], lambda h, s: (h, 0, 0, 0)),
        ],
        out_specs=[
            pl.BlockSpec((slab, dh), lambda h, s: (s, h)),
            pl.BlockSpec((per_slab, 1, 1, seq, dh), new_map),
            pl.BlockSpec((per_slab, 1, 1, seq, dh), new_map),
        ],
        out_shape=[
            jax.ShapeDtypeStruct((m, heads * dh), BF16),
            jax.ShapeDtypeStruct((p_slabs * per_slab, 1, heads, seq, dh), F32),
            jax.ShapeDtypeStruct((p_slabs * per_slab, 1, heads, seq, dh), F32),
        ],
        compiler_params=_cparams(
            ("arbitrary", "arbitrary"),
            2 * strips.shape[1] * strips.shape[2] * strips.shape[3] * 4 + 16 * slab * dh * 4
            + 10 * ATTN_QB * (slab + past) * 4 + (8 << 20)),
    )(proj, proj, proj, cache_k, cache_v, strips)


def _na_strips(rpb):
    heads, n_dr, _ = rpb.shape
    col = jnp.arange(GRID_W)
    c0 = jnp.clip(col - WIN_W // 2, 0, GRID_W - WIN_W)
    col_ok = (col[None, :] >= c0[:, None]) & (col[None, :] < c0[:, None] + WIN_W)
    dc_idx = jnp.clip(col[None, :] - col[:, None], -(WIN_W - 1), WIN_W - 1) + WIN_W - 1
    tiles = jnp.where(col_ok[None, None], rpb.astype(F32)[:, :, dc_idx], NEG_INF)
    n_tiles = 2 * WIN_H
    neg = jnp.full((heads, n_tiles + 1 - n_dr, GRID_W, GRID_W), NEG_INF, F32)
    tiles = jnp.concatenate([tiles, neg], axis=1)
    both = jnp.stack([tiles[:, :n_tiles], tiles[:, 1:]], axis=1)
    return both.transpose(0, 1, 3, 2, 4).reshape(heads, 2, GRID_W, n_tiles * GRID_W)


def _router_kernel(x_ref, sc_ref, sh_ref, wr_ref, br_ref, info_ref, cnt_ref, run_ref):
    @pl.when(pl.program_id(0) == 0)
    def _():
        run_ref[...] = jnp.zeros_like(run_ref)

    hm = x_ref[...] * (1.0 + sc_ref[0]) + sh_ref[0]
    logits = jnp.dot(hm, wr_ref[...], preferred_element_type=F32, precision=lax.Precision.HIGHEST) + br_ref[...]
    tm, width = logits.shape
    col = lax.broadcasted_iota(jnp.int32, (tm, width), 1).astype(F32)
    neg = jnp.float32(-3.0e38)

    gl = jnp.where(col < N_GROUPS, logits, neg)
    gmax = jnp.max(gl, axis=-1, keepdims=True)
    gsel = jnp.min(jnp.where(gl == gmax, col, float(width)), axis=-1, keepdims=True)
    p_g = 1.0 / jnp.sum(jnp.where(col < N_GROUPS, jnp.exp(gl - gmax), 0.0), axis=-1, keepdims=True)

    lo = N_GROUPS + N_EXP_PER_GROUP * gsel
    el = jnp.where((col >= lo) & (col < lo + N_EXP_PER_GROUP), logits, neg)
    e1 = jnp.max(el, axis=-1, keepdims=True)
    i1 = jnp.min(jnp.where(el == e1, col, float(width)), axis=-1, keepdims=True)
    el2 = jnp.where(col == i1, neg, el)
    e2 = jnp.max(el2, axis=-1, keepdims=True)
    i2 = jnp.min(jnp.where(el2 == e2, col, float(width)), axis=-1, keepdims=True)
    t = jnp.exp(e2 - e1)
    w1 = 1.0 / (1.0 + t)
    c1 = p_g * w1
    c2 = p_g * (t * w1)
    id1 = i1 - N_GROUPS
    id2 = i2 - N_GROUPS

    onehot = jnp.where((col == id1) | (col == id2), 1.0, 0.0)
    ri = lax.broadcasted_iota(jnp.int32, (tm, tm), 0)
    ci = lax.broadcasted_iota(jnp.int32, (tm, tm), 1)
    tri = jnp.where(ri > ci, 1.0, 0.0).astype(BF16)
    before = _dot(tri, onehot.astype(BF16)) + run_ref[...]
    r1 = jnp.sum(jnp.where(col == id1, before, 0.0), axis=-1, keepdims=True)
    r2 = jnp.sum(jnp.where(col == id2, before, 0.0), axis=-1, keepdims=True)
    run_ref[...] += jnp.sum(onehot, axis=0, keepdims=True)
    cnt_ref[...] = run_ref[...]

    info = jnp.where(col == 0, id1, 0.0)
    info = jnp.where(col == 1, id2, info)
    info = jnp.where(col == 2, c1, info)
    info = jnp.where(col == 3, c2, info)
    info = jnp.where(col == 4, r1, info)
    info = jnp.where(col == 5, r2, info)
    info_ref[...] = info


def _router(x, mods3, mod_row, layer, wr, br, tm=512):
    m, d = x.shape
    base = layer * MOD_ROWS
    return pl.pallas_call(
        _router_kernel,
        grid=(m // tm,),
        in_specs=[
            pl.BlockSpec((tm, d), lambda i: (i, 0)),
            pl.BlockSpec((1, 1, d), lambda i: (base + mod_row(i, tm), 0, 4)),
            pl.BlockSpec((1, 1, d), lambda i: (base + mod_row(i, tm), 0, 3)),
            pl.BlockSpec((d, V7X_LANES), lambda i: (0, 0)),
            pl.BlockSpec((1, V7X_LANES), lambda i: (0, 0)),
        ],
        out_specs=[
            pl.BlockSpec((tm, V7X_LANES), lambda i: (i, 0)),
            pl.BlockSpec((1, V7X_LANES), lambda i: (0, 0)),
        ],
        out_shape=[
            jax.ShapeDtypeStruct((m, V7X_LANES), F32),
            jax.ShapeDtypeStruct((1, V7X_LANES), F32),
        ],
        scratch_shapes=[pltpu.VMEM((1, V7X_LANES), F32)],
        compiler_params=_cparams(("arbitrary",), 4 * tm * d * 4 + 2 * d * V7X_LANES * 4 + (8 << 20)),
    )(x, mods3, mods3, wr, br)


def _pos_kernel(info_ref, off_ref, pos_ref):
    info = info_ref[...]
    tm, width = info.shape
    col = lax.broadcasted_iota(jnp.int32, (tm, width), 1).astype(F32)
    off = off_ref[...]
    p1 = jnp.sum(jnp.where(col == info[:, 0:1], off, 0.0), axis=-1, keepdims=True) + info[:, 4:5]
    p2 = jnp.sum(jnp.where(col == info[:, 1:2], off, 0.0), axis=-1, keepdims=True) + info[:, 5:6]
    pm = jnp.where(col == 0, p1, jnp.where(col == 1, p2, 0.0))
    pos_ref[0] = jnp.transpose(pm)[:8, :].astype(jnp.int32)


def _positions(info, row_off, tm):
    m = info.shape[0]
    return pl.pallas_call(
        _pos_kernel,
        grid=(m // tm,),
        in_specs=[
            pl.BlockSpec((tm, V7X_LANES), lambda i: (i, 0)),
            pl.BlockSpec((1, V7X_LANES), lambda i: (0, 0)),
        ],
        out_specs=pl.BlockSpec((1, 8, tm), lambda i: (i, 0, 0)),
        out_shape=jax.ShapeDtypeStruct((m // tm, 8, tm), jnp.int32),
        compiler_params=_cparams(("arbitrary",), 16 << 20),
    )(info, row_off)


def _dispatch_kernel(tm, pad_bits, ps_ref, pl_ref, pos_ref, x_ref, sc_ref, sh_ref, xs_hbm,
                     hbuf, zbuf, sems, zsem):
    i = pl.program_id(0)
    n_steps = pl.num_programs(0)
    n_experts = ps_ref.shape[0] - 1
    zrows = zbuf.shape[0]
    max_tail = (xs_hbm.shape[0] - 2 * x_ref.shape[0] * n_steps) // zrows

    def fill_copies(do):
        for e in range(n_experts):
            start = ps_ref[e]
            rem = pl_ref[e]
            for r in range(SUBLANES - 1):
                @pl.when(r < (rem & (SUBLANES - 1)))
                def _(start=start, r=r):
                    do(pltpu.make_async_copy(zbuf.at[pl.ds(0, 1)], xs_hbm.at[pl.ds(start + r, 1)], zsem))

            off = start + rem
            for bit in pad_bits:
                off = off - (rem & bit)

                @pl.when((rem & bit) != 0)
                def _(off=off, bit=bit):
                    dst = xs_hbm.at[pl.ds(pl.multiple_of(off, SUBLANES), bit)]
                    do(pltpu.make_async_copy(zbuf.at[pl.ds(0, bit)], dst, zsem))
        for t in range(max_tail):
            @pl.when(t < pl_ref[n_experts])
            def _(t=t):
                dst = xs_hbm.at[pl.ds(pl.multiple_of(ps_ref[n_experts] + t * zrows, SUBLANES), zrows)]
                do(pltpu.make_async_copy(zbuf, dst, zsem))

    @pl.when(i == 0)
    def _():
        zbuf[...] = jnp.zeros_like(zbuf)
        fill_copies(lambda cp: cp.start())

    def scatter_start(sub):
        for r in range(tm):
            for slot in range(2):
                pltpu.make_async_copy(hbuf.at[sub, pl.ds(r, 1)],
                                      xs_hbm.at[pl.ds(pos_ref[0, slot, sub * tm + r], 1)], sems.at[sub]).start()

    def scatter_wait(sub):
        for _ in range(2):
            pltpu.make_async_copy(hbuf.at[sub], xs_hbm.at[pl.ds(0, tm)], sems.at[sub]).wait()

    for sub in range(2):
        rows = pl.ds(sub * tm, tm)
        hbuf[sub] = x_ref[rows, :] * (1.0 + sc_ref[0]) + sh_ref[0]
        scatter_start(sub)
        if sub == 0:
            @pl.when(i > 0)
            def _():
                scatter_wait(1)
        else:
            scatter_wait(0)

    @pl.when(i == n_steps - 1)
    def _():
        scatter_wait(1)
        fill_copies(lambda cp: cp.wait())


def _dispatch(x, posT, pad_start, pad_len, mods3, mod_row, layer, n_rows, tm, moe_tm):
    m, d = x.shape
    base = layer * MOD_ROWS
    step = 2 * tm
    pad_bits = tuple(1 << b for b in range(moe_tm.bit_length() - 2, SUBLANES.bit_length() - 2, -1))
    grid_spec = pltpu.PrefetchScalarGridSpec(
        num_scalar_prefetch=2,
        grid=(m // step,),
        in_specs=[
            pl.BlockSpec((1, 8, step), lambda i, ps, pn: (i, 0, 0), memory_space=pltpu.SMEM),
            pl.BlockSpec((step, d), lambda i, ps, pn: (i, 0)),
            pl.BlockSpec((1, 1, d), lambda i, ps, pn: (base + mod_row(i, step), 0, 4)),
            pl.BlockSpec((1, 1, d), lambda i, ps, pn: (base + mod_row(i, step), 0, 3)),
        ],
        out_specs=pl.BlockSpec(memory_space=pl.ANY),
        scratch_shapes=[
            pltpu.VMEM((2, tm, d), F32),
            pltpu.VMEM((moe_tm // 2, d), F32),
            pltpu.SemaphoreType.DMA((2,)),
            pltpu.SemaphoreType.DMA(()),
        ],
    )
    return pl.pallas_call(
        functools.partial(_dispatch_kernel, tm, pad_bits),
        grid_spec=grid_spec,
        out_shape=jax.ShapeDtypeStruct((n_rows, d), F32),
        compiler_params=_cparams(("arbitrary",), 2 * step * d * 4 + 3 * tm * d * 4 + (8 << 20)),
    )(pad_start, pad_len, posT, x, mods3, mods3)


def _moe_kernel(layer, te_ref, nu_ref, nxt_ref, par_ref, xs_ref, wg_hbm, wu_hbm, wd_hbm, ys_ref,
                wg_f, wu_f, wd_f, wg_b, wu_b, wd_b, sems):
    i = pl.program_id(0)
    used = i < nu_ref[0]
    expert = te_ref[i]
    first = used & ((i == 0) | (expert != te_ref[jnp.maximum(i - 1, 0)]))

    def weight_copies(e, slot):
        return (pltpu.make_async_copy(wg_hbm.at[layer, e], wg_f.at[slot], sems.at[slot]),
                pltpu.make_async_copy(wu_hbm.at[layer, e], wu_f.at[slot], sems.at[slot]),
                pltpu.make_async_copy(wd_hbm.at[layer, e], wd_f.at[slot], sems.at[slot]))

    @pl.when(jnp.logical_not(used))
    def _():
        ys_ref[...] = jnp.zeros_like(ys_ref)

    @pl.when(i == 0)
    def _():
        for cp in weight_copies(expert, par_ref[0]):
            cp.start()

    @pl.when(first)
    def _():
        slot = par_ref[i]
        for cp in weight_copies(expert, slot):
            cp.wait()

        @pl.when(nxt_ref[i] >= 0)
        def _():
            for cp in weight_copies(nxt_ref[i], 1 - slot):
                cp.start()

        wg_b[...] = wg_f[slot].astype(BF16)
        wu_b[...] = wu_f[slot].astype(BF16)
        wd_b[...] = wd_f[slot].astype(BF16)

    @pl.when(used)
    def _():
        x = xs_ref[...].astype(BF16)
        g = _dot(x, wg_b[...])
        u = _dot(x, wu_b[...])
        hid = (_silu(g) * u).astype(BF16)
        ys_ref[...] = _dot(hid, wd_b[...])


def _moe_experts(xs, tile_expert, n_used, next_expert, slot, w_gate, w_up, w_down, layer, tm):
    n_rows, d = xs.shape
    f = w_gate.shape[-1]
    n_tiles = n_rows // tm
    grid_spec = pltpu.PrefetchScalarGridSpec(
        num_scalar_prefetch=4,
        grid=(n_tiles,),
        in_specs=[
            pl.BlockSpec((tm, d), lambda i, te, nu, nx, pr: (jnp.maximum(jnp.minimum(i, nu[0] - 1), 0), 0)),
            pl.BlockSpec(memory_space=pl.ANY),
            pl.BlockSpec(memory_space=pl.ANY),
            pl.BlockSpec(memory_space=pl.ANY),
        ],
        out_specs=pl.BlockSpec((tm, d), lambda i, te, nu, nx, pr: (i, 0)),
        scratch_shapes=[
            pltpu.VMEM((2, d, f), F32), pltpu.VMEM((2, d, f), F32), pltpu.VMEM((2, f, d), F32),
            pltpu.VMEM((d, f), BF16), pltpu.VMEM((d, f), BF16), pltpu.VMEM((f, d), BF16),
            pltpu.SemaphoreType.DMA((2,)),
        ],
    )
    return pl.pallas_call(
        functools.partial(_moe_kernel, layer),
        grid_spec=grid_spec,
        out_shape=jax.ShapeDtypeStruct((n_rows, d), F32),
        compiler_params=_cparams(("arbitrary",), 6 * d * f * 4 + 3 * d * f * 2 + 8 * tm * d * 4 + (8 << 20)),
    )(tile_expert, n_used, next_expert, slot, xs, w_gate, w_up, w_down)


def _combine_ln_kernel(alpha, tm, pos_ref, nxt_ref, ys_hbm, info_ref, x_ref, gate_ref, g_ref, b_ref, o_ref,
                       buf, sems):
    i = pl.program_id(0)
    n_steps = pl.num_programs(0)

    def gather_start(sub, p_ref):
        for r in range(tm):
            for slot in range(2):
                pltpu.make_async_copy(ys_hbm.at[pl.ds(p_ref[0, slot, sub * tm + r], 1)],
                                      buf.at[sub, slot, pl.ds(r, 1)], sems.at[sub]).start()

    def gather_wait(sub):
        for slot in range(2):
            pltpu.make_async_copy(ys_hbm.at[pl.ds(0, tm)], buf.at[sub, slot], sems.at[sub]).wait()

    def finish(sub):
        rows = pl.ds(sub * tm, tm)
        info = info_ref[rows, :]
        y = info[:, 2:3] * buf[sub, 0] + info[:, 3:4] * buf[sub, 1]
        z = alpha * x_ref[rows, :] + gate_ref[0] * y
        o_ref[rows, :] = _layer_norm(z, g_ref[0], b_ref[0])

    @pl.when(i == 0)
    def _():
        gather_start(0, pos_ref)

    gather_start(1, pos_ref)
    gather_wait(0)
    finish(0)

    @pl.when(i + 1 < n_steps)
    def _():
        gather_start(0, nxt_ref)

    gather_wait(1)
    finish(1)


def _combine_ln(ys, posT, info, x, mods3, mod_row, layer, ln_g, ln_b, alpha, tm):
    m, d = x.shape
    step = 2 * tm
    n_steps = m // step
    base = layer * MOD_ROWS
    return pl.pallas_call(
        functools.partial(_combine_ln_kernel, alpha, tm),
        grid=(n_steps,),
        in_specs=[
            pl.BlockSpec((1, 8, step), lambda i: (i, 0, 0), memory_space=pltpu.SMEM),
            pl.BlockSpec((1, 8, step), lambda i: (jnp.minimum(i + 1, n_steps - 1), 0, 0), memory_space=pltpu.SMEM),
            pl.BlockSpec(memory_space=pl.ANY),
            pl.BlockSpec((step, V7X_LANES), lambda i: (i, 0)),
            pl.BlockSpec((step, d), lambda i: (i, 0)),
            pl.BlockSpec((1, 1, d), lambda i: (base + mod_row(i, step), 0, 5)),
            pl.BlockSpec((1, 1, d), lambda i: (layer, 0, 0)),
            pl.BlockSpec((1, 1, d), lambda i: (layer, 0, 0)),
        ],
        out_specs=pl.BlockSpec((step, d), lambda i: (i, 0)),
        out_shape=jax.ShapeDtypeStruct((m, d), F32),
        scratch_shapes=[pltpu.VMEM((2, 2, tm, d), F32), pltpu.SemaphoreType.DMA((2,))],
        compiler_params=_cparams(("arbitrary",), 4 * step * d * 4 + 4 * tm * d * 4 + 6 * tm * d * 4 + (8 << 20)),
    )(posT, posT, ys, info, x, mods3, ln_g, ln_b)


def _hier_moe_ln(x, mods3, mod_row, layer, w_rg, b_rg, w_re, b_re, w_gate, w_up, w_down, ln_g, ln_b, alpha):
    m, d = x.shape
    n_e = N_EXPERTS
    pad = V7X_LANES - N_GROUPS - n_e
    wr = jnp.concatenate(
        [w_rg[layer], w_re[layer].transpose(1, 0, 2).reshape(d, n_e), jnp.zeros((d, pad), F32)], axis=1)
    br = jnp.concatenate([b_rg[layer], b_re[layer].reshape(n_e), jnp.zeros((pad,), F32)])[None, :]
    info, cnt = _router(x, mods3, mod_row, layer, wr, br)

    tm = MOE_TM
    n_tiles = 2 * m // tm + n_e
    counts = cnt[0, :n_e].astype(jnp.int32)
    tiles_e = (counts + tm - 1) // tm
    tile_end = jnp.cumsum(tiles_e)
    n_used = tile_end[-1]
    row_off = (tile_end - tiles_e) * tm
    tile_ids = jnp.minimum(jnp.arange(n_tiles, dtype=jnp.int32), n_used - 1)
    tile_expert = jnp.sum((tile_ids[:, None] >= tile_end[None, :]).astype(jnp.int32), axis=1)
    group_end = tile_end[tile_expert]
    next_expert = jnp.where(group_end < n_used, tile_expert[jnp.minimum(group_end, n_tiles - 1)], -1)
    is_first = jnp.concatenate([jnp.ones((1,), jnp.int32), (tile_expert[1:] != tile_expert[:-1]).astype(jnp.int32)])
    slot = (jnp.cumsum(is_first) - 1) % 2
    zrows = tm // 2
    pad_start = jnp.concatenate([row_off + counts, (n_used * tm)[None]])
    pad_len = jnp.concatenate([tiles_e * tm - counts, ((n_tiles - n_used) * (tm // zrows))[None]])
    row_off_f = jnp.concatenate([row_off.astype(F32), jnp.zeros((V7X_LANES - n_e,), F32)])[None, :]

    ct = COMB_TM
    posT = _positions(info, row_off_f, 2 * ct)
    xs = _dispatch(x, posT, pad_start, pad_len, mods3, mod_row, layer, n_tiles * tm, ct, tm)
    ys = _moe_experts(xs, tile_expert, n_used.reshape(1), next_expert.astype(jnp.int32), slot.astype(jnp.int32),
                      w_gate, w_up, w_down, layer, tm)
    return _combine_ln(ys, posT, info, x, mods3, mod_row, layer, ln_g, ln_b, alpha, ct)


def kernel(x_prompt, x_sample, state_ret, cache_na_k, cache_na_v, c, c_ctx, w_mod, b_mod, ln1_g, ln1_b, ln2_g, ln2_b, w_ret_in, ret_decay_logit, ret_gn_g, ret_gn_b, w_ret_out, w_na_in, na_rpb, w_na_out, w_rg, b_rg, w_re, b_re, w_gate, w_up, w_down):
    bp, seq, d = x_prompt.shape
    bs, dec_seq, _ = x_sample.shape
    depth = w_mod.shape[0]
    slab = dec_seq
    assert slab % seq == 0 and (bp * seq) % slab == 0 and bs + 1 <= MOD_ROWS
    p_rows = bp * seq
    p_slabs = p_rows // slab
    alpha = (2.0 * depth) ** 0.25

    def mod_row(i, tm):
        start = i * tm
        return jnp.where(start < p_rows, 0, 1 + (start - p_rows) // dec_seq)

    cc = jnp.concatenate([c_ctx[None, :], c, jnp.zeros((MOD_ROWS - 1 - bs, d), F32)], axis=0)
    mods = _modulation(cc, w_mod, b_mod)
    mods3 = mods.reshape(depth * MOD_ROWS, 1, 6 * d)
    x = jnp.concatenate([x_prompt.reshape(p_rows, d), x_sample.reshape(bs * dec_seq, d)], axis=0)

    ln1_g3, ln1_b3 = ln1_g.reshape(depth, 1, d), ln1_b.reshape(depth, 1, d)
    ln2_g3, ln2_b3 = ln2_g.reshape(depth, 1, d), ln2_b.reshape(depth, 1, d)

    new_ret, new_k, new_v = [], [], []
    for l in range(depth):
        j = l // 2
        if l % 2 == 0:
            dk = state_ret.shape[-2]
            proj = _inproj(x, mods3, mod_row, l, w_ret_in, j)
            log_g = -jax.nn.softplus(-ret_decay_logit[j].astype(F32))
            t = jnp.arange(slab)
            nf = dk // 4
            inv_freq = ROPE_BASE ** (-jnp.arange(nf, dtype=F32) / nf)
            ang_r = (t // GRID_W).astype(F32)[:, None] * inv_freq
            ang_c = (t % GRID_W).astype(F32)[:, None] * inv_freq
            cos = jnp.concatenate([jnp.cos(ang_r)] * 2 + [jnp.cos(ang_c)] * 2, axis=1)
            sin = jnp.concatenate([-jnp.sin(ang_r), jnp.sin(ang_r), -jnp.sin(ang_c), jnp.sin(ang_c)], axis=1)
            rope_cos = jnp.stack([jnp.ones_like(cos), cos])
            rope_sin = jnp.stack([jnp.zeros_like(sin), sin])
            y, st = _retention(proj, log_g, rope_cos, rope_sin, state_ret, j, ret_gn_g[j][:, None, :],
                               ret_gn_b[j][:, None, :], p_slabs, slab)
            new_ret.append(st)
            x = _outproj_ln(y, w_ret_out, j, x, mods3, mod_row, l, ln1_g3, ln1_b3, alpha)
        else:
            proj = _inproj(x, mods3, mod_row, l, w_na_in, j)
            o, nk, nv = _attention(proj, cache_na_k, cache_na_v, j, _na_strips(na_rpb[j]), p_slabs, slab, seq)
            new_k.append(nk[:, 0])
            new_v.append(nv[:, 0])
            x = _outproj_ln(o, w_na_out, j, x, mods3, mod_row, l, ln1_g3, ln1_b3, alpha)
        x = _hier_moe_ln(x, mods3, mod_row, l, w_rg, b_rg, w_re, b_re, w_gate, w_up, w_down,
                         ln2_g3, ln2_b3, alpha)

    y_prompt = x[:p_rows].reshape(bp, seq, d)
    y_sample = x[p_rows:].reshape(bs, dec_seq, d)
    return (y_prompt, y_sample, jnp.stack(new_ret, axis=1), jnp.stack(new_k, axis=1), jnp.stack(new_v, axis=1))
```

```python
import functools

import jax
import jax.numpy as jnp
from jax import lax
from jax.experimental import pallas as pl
from jax.experimental.pallas import tpu as pltpu

F32 = jnp.float32
BF16 = jnp.bfloat16

GRID_W = 64
WIN_H = 8
WIN_W = 16
N_GROUPS = 4
N_EXP_PER_GROUP = 8
N_EXPERTS = N_GROUPS * N_EXP_PER_GROUP
ROPE_BASE = 10000.0
LN_EPS = 1e-5
NEG_INF = -1e30

V7X_VMEM_BYTES = 64 * 1024 * 1024
V7X_LANES = 128
SUBLANES = 8
MOD_ROWS = 8

RET_CHUNK = 256
ATTN_QB = 256
ATTN_HEADS_PER_STEP = 2
MOE_TM = 256
COMB_TM = 256


def _cparams(sem, vmem_bytes):
    limit = int(min(max(vmem_bytes, 16 * 1024 * 1024), V7X_VMEM_BYTES - 6 * 1024 * 1024))
    return pltpu.CompilerParams(dimension_semantics=sem, vmem_limit_bytes=limit)


def _silu(x):
    return x / (1.0 + jnp.exp(-x))


def _dot(a, b):
    return jnp.dot(a, b, preferred_element_type=F32)


def _dot_nt(a, b):
    return lax.dot_general(a, b, (((1,), (1,)), ((), ())), preferred_element_type=F32)


def _layer_norm(z, g, b):
    mu = jnp.mean(z, axis=-1, keepdims=True)
    zc = z - mu
    var = jnp.mean(zc * zc, axis=-1, keepdims=True)
    return zc * lax.rsqrt(var + LN_EPS) * g + b


def _mod_kernel(c_ref, w_ref, b_ref, o_ref):
    s = _silu(c_ref[...]).astype(BF16)
    o_ref[0] = _dot(s, w_ref[0].astype(BF16)) + b_ref[0]


def _modulation(cc, w_mod, b_mod):
    depth, d, n6 = w_mod.shape
    tn = 1024
    return pl.pallas_call(
        _mod_kernel,
        grid=(depth, n6 // tn),
        in_specs=[
            pl.BlockSpec((MOD_ROWS, d), lambda l, j: (0, 0)),
            pl.BlockSpec((1, d, tn), lambda l, j: (l, 0, j)),
            pl.BlockSpec((1, 1, tn), lambda l, j: (l, 0, j)),
        ],
        out_specs=pl.BlockSpec((1, MOD_ROWS, tn), lambda l, j: (l, 0, j)),
        out_shape=jax.ShapeDtypeStruct((depth, MOD_ROWS, n6), F32),
        compiler_params=_cparams(("arbitrary", "arbitrary"), 3 * d * tn * 4 + (4 << 20)),
    )(cc, w_mod, b_mod.reshape(depth, 1, n6))


def _prologue_kernel(p_tiles, xp_ref, xs_ref, sc_ref, sh_ref, x_ref, hb_ref):
    @pl.when(pl.program_id(0) < p_tiles)
    def _():
        x = xp_ref[...]
        x_ref[...] = x
        hb_ref[...] = (x * (1.0 + sc_ref[0]) + sh_ref[0]).astype(BF16)

    @pl.when(pl.program_id(0) >= p_tiles)
    def _():
        x = xs_ref[...]
        x_ref[...] = x
        hb_ref[...] = (x * (1.0 + sc_ref[0]) + sh_ref[0]).astype(BF16)


def _prologue(xp, xs, mods3, mod_row, tm=512):
    p_rows, d = xp.shape
    m = p_rows + xs.shape[0]
    p_tiles = p_rows // tm
    return pl.pallas_call(
        functools.partial(_prologue_kernel, p_tiles),
        grid=(m // tm,),
        in_specs=[
            pl.BlockSpec((tm, d), lambda i: (jnp.minimum(i, p_tiles - 1), 0)),
            pl.BlockSpec((tm, d), lambda i: (jnp.maximum(i - p_tiles, 0), 0)),
            pl.BlockSpec((1, 1, d), lambda i: (mod_row(i, tm), 0, 1)),
            pl.BlockSpec((1, 1, d), lambda i: (mod_row(i, tm), 0, 0)),
        ],
        out_specs=[pl.BlockSpec((tm, d), lambda i: (i, 0)), pl.BlockSpec((tm, d), lambda i: (i, 0))],
        out_shape=[jax.ShapeDtypeStruct((m, d), F32), jax.ShapeDtypeStruct((m, d), BF16)],
        compiler_params=_cparams(("arbitrary",), 8 * tm * d * 4 + (8 << 20)),
    )(xp, xs, mods3, mods3)


def _inproj_kernel(h_ref, w_ref, o_ref):
    o_ref[...] = _dot(h_ref[...], w_ref[...].astype(BF16))


def _inproj(hb, w_all, w_idx, tm=2048, tn=512):
    m, d = hb.shape
    n = w_all.shape[-1]
    return pl.pallas_call(
        _inproj_kernel,
        grid=(m // tm, n // tn),
        in_specs=[
            pl.BlockSpec((tm, d), lambda i, j: (i, 0)),
            pl.BlockSpec((None, d, tn), lambda i, j: (w_idx, 0, j)),
        ],
        out_specs=pl.BlockSpec((tm, tn), lambda i, j: (i, j)),
        out_shape=jax.ShapeDtypeStruct((m, n), F32),
        compiler_params=_cparams(
            ("arbitrary", "arbitrary"),
            2 * tm * d * 2 + 2 * d * tn * 4 + d * tn * 2 + 3 * tm * tn * 4 + (4 << 20)),
    )(hb, w_all)


def _outproj_ln_kernel(alpha, y_ref, w_ref, x_ref, gate_ref, g_ref, b_ref, o_ref):
    z = alpha * x_ref[...] + gate_ref[0] * _dot(y_ref[...], w_ref[...])
    o_ref[...] = _layer_norm(z, g_ref[0], b_ref[0])


def _outproj_ln(y, w, x, mods3, mod_row, layer, ln_g, ln_b, alpha, tm=512):
    m, kdim = y.shape
    d = x.shape[1]
    base = layer * MOD_ROWS
    return pl.pallas_call(
        functools.partial(_outproj_ln_kernel, alpha),
        grid=(m // tm,),
        in_specs=[
            pl.BlockSpec((tm, kdim), lambda i: (i, 0)),
            pl.BlockSpec((kdim, d), lambda i: (0, 0)),
            pl.BlockSpec((tm, d), lambda i: (i, 0)),
            pl.BlockSpec((1, 1, d), lambda i: (base + mod_row(i, tm), 0, 2)),
            pl.BlockSpec((1, 1, d), lambda i: (layer, 0, 0)),
            pl.BlockSpec((1, 1, d), lambda i: (layer, 0, 0)),
        ],
        out_specs=pl.BlockSpec((tm, d), lambda i: (i, 0)),
        out_shape=jax.ShapeDtypeStruct((m, d), F32),
        compiler_params=_cparams(
            ("arbitrary",), kdim * d * 2 + 2 * tm * kdim * 2 + 7 * tm * d * 4 + (4 << 20)),
    )(y, w, x, mods3, ln_g, ln_b)


def _rope(x, cos, sin):
    half = V7X_LANES // 2
    sw = jnp.concatenate(
        [pltpu.roll(x[:, :V7X_LANES], half, 1), pltpu.roll(x[:, V7X_LANES:], half, 1)], axis=1)
    return x * cos + sw * sin


def _group_norm(o, g, b):
    mu = jnp.mean(o, axis=-1, keepdims=True)
    oc = o - mu
    var = jnp.mean(oc * oc, axis=-1, keepdims=True)
    return oc * lax.rsqrt(var + LN_EPS) * g + b


def _ret_kernel(p_slabs, dk, lg_ref, q_ref, k_ref, v_ref, gf_ref, gb_ref, cos_ref, sin_ref, s0_ref,
                gng_ref, gnb_ref, y_ref, st_ref, q_s, k_s, yacc):
    h = pl.program_id(0)
    s = pl.program_id(1)
    is_prompt = s < p_slabs
    c = RET_CHUNK
    n_chunks = q_ref.shape[0] // c

    def run(prompt):
        ri = lax.broadcasted_iota(jnp.int32, (c, c), 0)
        ci = lax.broadcasted_iota(jnp.int32, (c, c), 1)
        diff = (ri - ci).astype(F32)
        pos = lax.broadcasted_iota(jnp.int32, (c, 1), 0).astype(F32)
        gn_g = gng_ref[0]
        gn_b = gnb_ref[0]
        scale = dk ** -0.5
        if not prompt:
            q_s[...] = _rope(q_ref[...], cos_ref[...], sin_ref[...])
            k_s[...] = _rope(k_ref[...], cos_ref[...], sin_ref[...]) * scale

        for direction in (0, 1):
            lg = jnp.full((1, 1), lg_ref[direction, h], F32)
            if direction == 0:
                mask = jnp.where(diff >= 0, jnp.exp(jnp.maximum(diff, 0.0) * lg), 0.0)
                q_dec = jnp.exp((pos + 1.0) * lg)
                k_dec = jnp.exp((c - 1.0 - pos) * lg)
                order = range(n_chunks)
                gate_ref = gf_ref
            else:
                mask = jnp.where(diff <= 0, jnp.exp(jnp.maximum(-diff, 0.0) * lg), 0.0)
                q_dec = jnp.exp((c - pos) * lg)
                k_dec = jnp.exp(pos * lg)
                order = range(n_chunks - 1, -1, -1)
                gate_ref = gb_ref
            c_dec = jnp.exp(c * lg)
            state = None if prompt else s0_ref[0, direction, 0]
            for ch in order:
                rows = pl.ds(ch * c, c)
                if prompt:
                    qc = q_ref[rows, :]
                    kc = k_ref[rows, :] * scale
                else:
                    qc = q_s[rows, :]
                    kc = k_s[rows, :]
                vb = v_ref[rows, :].astype(BF16)
                inner = _dot_nt(qc.astype(BF16), kc.astype(BF16)) * mask
                o = _dot(inner.astype(BF16), vb)
                kv = _dot(jnp.transpose(kc * k_dec).astype(BF16), vb)
                if prompt:
                    st_ref[ch, direction, 0] = kv
                else:
                    o = o + _dot((qc * q_dec).astype(BF16), state.astype(BF16))
                    state = c_dec * state + kv
                contrib = _group_norm(o, gn_g, gn_b) * _silu(gate_ref[rows, :])
                if direction == 0:
                    yacc[rows, :] = contrib
                else:
                    y_ref[rows, :] = (yacc[rows, :] + contrib).astype(BF16)

    @pl.when(is_prompt)
    def _():
        run(True)

    @pl.when(jnp.logical_not(is_prompt))
    def _():
        run(False)


def _retention(proj, log_g, rope_cos, rope_sin, state_ret, ret_idx, gn_g, gn_b, p_slabs, slab):
    m = proj.shape[0]
    heads, dk, dv = state_ret.shape[-3:]
    n_slabs = m // slab
    per_slab = slab // RET_CHUNK
    v_blk0 = 2 * heads * dk // dv
    p_last = p_slabs - 1

    def s0_map(h, s):
        return (jnp.maximum(s - p_slabs, 0), ret_idx, 0, h, 0, 0)

    kernel = functools.partial(_ret_kernel, p_slabs, dk)
    return pl.pallas_call(
        kernel,
        grid=(heads, n_slabs),
        in_specs=[
            pl.BlockSpec(memory_space=pltpu.SMEM),
            pl.BlockSpec((slab, dk), lambda h, s: (s, h)),
            pl.BlockSpec((slab, dk), lambda h, s: (s, heads + h)),
            pl.BlockSpec((slab, dv), lambda h, s: (s, v_blk0 + h)),
            pl.BlockSpec((slab, dv), lambda h, s: (s, v_blk0 + heads + h)),
            pl.BlockSpec((slab, dv), lambda h, s: (s, v_blk0 + 2 * heads + h)),
            pl.BlockSpec((slab, dk), lambda h, s: (0, 0)),
            pl.BlockSpec((slab, dk), lambda h, s: (0, 0)),
            pl.BlockSpec((1, None, 2, 1, dk, dv), s0_map),
            pl.BlockSpec((1, 1, dv), lambda h, s: (h, 0, 0)),
            pl.BlockSpec((1, 1, dv), lambda h, s: (h, 0, 0)),
        ],
        out_specs=[
            pl.BlockSpec((slab, dv), lambda h, s: (s, h)),
            pl.BlockSpec((per_slab, 2, 1, dk, dv), lambda h, s: (jnp.minimum(s, p_last), 0, h, 0, 0)),
        ],
        out_shape=[
            jax.ShapeDtypeStruct((m, heads * dv), BF16),
            jax.ShapeDtypeStruct((p_slabs * per_slab, 2, heads, dk, dv), F32),
        ],
        scratch_shapes=[
            pltpu.VMEM((slab, dk), F32),
            pltpu.VMEM((slab, dk), F32),
            pltpu.VMEM((slab, dv), F32),
        ],
        compiler_params=_cparams(
            ("arbitrary", "arbitrary"),
            2 * (2 * slab * dk + 3 * slab * dv) * 4 + 4 * slab * dk * 4 + 2 * 2 * dk * dv * 4
            + 2 * slab * dv * 2 + 2 * per_slab * 2 * dk * dv * 4 + 2 * slab * dk * 4 + slab * dv * 4
            + (8 << 20)),
    )(log_g, proj, proj, proj, proj, proj, rope_cos, rope_sin, state_ret, gn_g, gn_b)


def _na_window(rq, rows):
    kh = min(WIN_H, rows)
    return min(max(rq - kh // 2, 0), rows - kh), kh


def _attn_kernel(p_slabs, seq, dh, q_ref, k_ref, v_ref, ck_ref, cv_ref, strip_ref, o_ref, nk_ref, nv_ref):
    s = pl.program_id(1)
    is_prompt = s < p_slabs
    slab = q_ref.shape[0]
    scale = dh ** -0.5

    def prompt_head(hh):
        cols = pl.ds(hh * dh, dh)
        for b in range(slab // seq):
            rows = pl.ds(b * seq, seq)
            k = k_ref[rows, cols]
            v = v_ref[rows, cols]
            nk_ref[b, 0, hh] = k
            nv_ref[b, 0, hh] = v
            q = (q_ref[rows, cols] * scale).astype(BF16)
            sc = _dot_nt(q, k.astype(BF16))
            p = jnp.exp(sc - jnp.max(sc, axis=-1, keepdims=True))
            denom = jnp.sum(p, axis=-1, keepdims=True)
            o_ref[rows, cols] = (_dot(p.astype(BF16), v.astype(BF16)) / denom).astype(BF16)

    def latent_head(hh):
        cols = pl.ds(hh * dh, dh)
        grid_rows = slab // GRID_W
        rows_per_block = ATTN_QB // GRID_W
        ckb = ck_ref[0, 0, hh].astype(BF16)
        cvb = cv_ref[0, 0, hh].astype(BF16)
        for qb in range(slab // ATTN_QB):
            rq0 = qb * rows_per_block
            ka = _na_window(rq0, grid_rows)[0] // 2 * 2
            last0, kh = _na_window(rq0 + rows_per_block - 1, grid_rows)
            kb = -((last0 + kh) // -2) * 2
            n_keys = (kb - ka) * GRID_W
            keys = pl.ds(ka * GRID_W, n_keys)
            pieces = []
            for i in range(rows_per_block):
                rq = rq0 + i
                r0, kh = _na_window(rq, grid_rows)
                first = ka - rq + WIN_H - 1
                piece = strip_ref[hh, first % 2, :, pl.ds((first - first % 2) * GRID_W, n_keys)]
                if r0 != ka or r0 + kh != kb:
                    key_row = ka + (lax.broadcasted_iota(jnp.int32, piece.shape, 1) // GRID_W)
                    piece = jnp.where((key_row >= r0) & (key_row < r0 + kh), piece, NEG_INF)
                pieces.append(piece)
            bias = jnp.concatenate(pieces, axis=0)
            rows = pl.ds(qb * ATTN_QB, ATTN_QB)
            q = (q_ref[rows, cols] * scale).astype(BF16)
            s_loc = _dot_nt(q, k_ref[keys, cols].astype(BF16)) + bias
            s_ctx = _dot_nt(q, ckb)
            mx = jnp.maximum(jnp.max(s_loc, axis=-1, keepdims=True), jnp.max(s_ctx, axis=-1, keepdims=True))
            p_loc = jnp.exp(s_loc - mx)
            p_ctx = jnp.exp(s_ctx - mx)
            denom = jnp.sum(p_loc, axis=-1, keepdims=True) + jnp.sum(p_ctx, axis=-1, keepdims=True)
            o = _dot(p_loc.astype(BF16), v_ref[keys, cols].astype(BF16)) + _dot(p_ctx.astype(BF16), cvb)
            o_ref[rows, cols] = (o / denom).astype(BF16)

    heads_per_step = q_ref.shape[1] // dh

    @pl.when(is_prompt)
    def _():
        for hh in range(heads_per_step):
            prompt_head(hh)

    @pl.when(jnp.logical_not(is_prompt))
    def _():
        for hh in range(heads_per_step):
            latent_head(hh)


def _attention(proj, cache_k, cache_v, cache_idx, strips, p_slabs, slab, seq):
    m = proj.shape[0]
    heads = cache_k.shape[2]
    dh = cache_k.shape[-1]
    past = cache_k.shape[-2]
    n_slabs = m // slab
    per_slab = slab // seq
    p_last = p_slabs - 1

    hp = ATTN_HEADS_PER_STEP
    groups = heads // hp

    def ctx_map(h, s):
        return (jnp.maximum(s - p_slabs, 0), cache_idx, h, 0, 0)

    def new_map(h, s):
        return (jnp.minimum(s, p_last), 0, h, 0, 0)

    kernel = functools.partial(_attn_kernel, p_slabs, seq, dh)
    return pl.pallas_call(
        kernel,
        grid=(groups, n_slabs),
        in_specs=[
            pl.BlockSpec((slab, hp * dh), lambda h, s: (s, h)),
            pl.BlockSpec((slab, hp * dh), lambda h, s: (s, groups + h)),
            pl.BlockSpec((slab, hp * dh), lambda h, s: (s, 2 * groups + h)),
            pl.BlockSpec((1, 1, hp, past, dh), ctx_map),
            pl.BlockSpec((1, 1, hp, past, dh), ctx_map),
            pl.BlockSpec((hp,) + strips.shape[1:], lambda h, s: (h, 0, 0, 0)),
        ],
        out_specs=[
            pl.BlockSpec((slab, hp * dh), lambda h, s: (s, h)),
            pl.BlockSpec((per_slab, 1, hp, seq, dh), new_map),
            pl.BlockSpec((per_slab, 1, hp, seq, dh), new_map),
        ],
        out_shape=[
            jax.ShapeDtypeStruct((m, heads * dh), BF16),
            jax.ShapeDtypeStruct((p_slabs * per_slab, 1, heads, seq, dh), F32),
            jax.ShapeDtypeStruct((p_slabs * per_slab, 1, heads, seq, dh), F32),
        ],
        compiler_params=_cparams(
            ("arbitrary", "arbitrary"),
            hp * (2 * strips.shape[1] * strips.shape[2] * strips.shape[3] * 4 + 16 * slab * dh * 4)
            + 10 * ATTN_QB * (slab + past) * 4 + (8 << 20)),
    )(proj, proj, proj, cache_k, cache_v, strips)


def _na_strips(rpb):
    heads, n_dr, n_dc = rpb.shape
    col = jnp.arange(GRID_W)
    c0 = jnp.clip(col - WIN_W // 2, 0, GRID_W - WIN_W)
    col_ok = (col[None, :] >= c0[:, None]) & (col[None, :] < c0[:, None] + WIN_W)
    dc_idx = jnp.clip(col[None, :] - col[:, None], -(WIN_W - 1), WIN_W - 1) + WIN_W - 1
    onehot = (dc_idx[:, :, None] == jnp.arange(n_dc)[None, None, :]).astype(F32)
    tiles = jnp.einsum("hdk,qck->hqdc", rpb.astype(F32), onehot, precision=lax.Precision.HIGHEST)
    tiles = jnp.where(col_ok[None, :, None, :], tiles, NEG_INF)
    n_tiles = 2 * WIN_H
    neg = jnp.full((heads, GRID_W, n_tiles + 1 - n_dr, GRID_W), NEG_INF, F32)
    tiles = jnp.concatenate([tiles, neg], axis=2)
    both = jnp.stack([tiles[:, :, :n_tiles], tiles[:, :, 1:]], axis=1)
    return both.reshape(heads, 2, GRID_W, n_tiles * GRID_W)


def _router_kernel(x_ref, sc_ref, sh_ref, wr_ref, br_ref, info_ref, cnt_ref, run_ref):
    @pl.when(pl.program_id(0) == 0)
    def _():
        run_ref[...] = jnp.zeros_like(run_ref)

    hm = x_ref[...] * (1.0 + sc_ref[0]) + sh_ref[0]
    wr = wr_ref[...]
    h_hi = hm.astype(BF16)
    h_lo = (hm - h_hi.astype(F32)).astype(BF16)
    w_hi = wr.astype(BF16)
    w_lo = (wr - w_hi.astype(F32)).astype(BF16)
    logits = _dot(h_hi, w_hi) + (_dot(h_hi, w_lo) + _dot(h_lo, w_hi)) + br_ref[...]
    tm, width = logits.shape
    col = lax.broadcasted_iota(jnp.int32, (tm, width), 1).astype(F32)
    neg = jnp.float32(-3.0e38)

    gl = jnp.where(col < N_GROUPS, logits, neg)
    gmax = jnp.max(gl, axis=-1, keepdims=True)
    gsel = jnp.min(jnp.where(gl == gmax, col, float(width)), axis=-1, keepdims=True)
    p_g = 1.0 / jnp.sum(jnp.where(col < N_GROUPS, jnp.exp(gl - gmax), 0.0), axis=-1, keepdims=True)

    lo = N_GROUPS + N_EXP_PER_GROUP * gsel
    el = jnp.where((col >= lo) & (col < lo + N_EXP_PER_GROUP), logits, neg)
    e1 = jnp.max(el, axis=-1, keepdims=True)
    i1 = jnp.min(jnp.where(el == e1, col, float(width)), axis=-1, keepdims=True)
    el2 = jnp.where(col == i1, neg, el)
    e2 = jnp.max(el2, axis=-1, keepdims=True)
    i2 = jnp.min(jnp.where(el2 == e2, col, float(width)), axis=-1, keepdims=True)
    t = jnp.exp(e2 - e1)
    w1 = 1.0 / (1.0 + t)
    c1 = p_g * w1
    c2 = p_g * (t * w1)
    id1 = i1 - N_GROUPS
    id2 = i2 - N_GROUPS

    onehot = jnp.where((col == id1) | (col == id2), 1.0, 0.0)
    ri = lax.broadcasted_iota(jnp.int32, (tm, tm), 0)
    ci = lax.broadcasted_iota(jnp.int32, (tm, tm), 1)
    tri = jnp.where(ri > ci, 1.0, 0.0).astype(BF16)
    before = _dot(tri, onehot.astype(BF16)) + run_ref[...]
    r1 = jnp.sum(jnp.where(col == id1, before, 0.0), axis=-1, keepdims=True)
    r2 = jnp.sum(jnp.where(col == id2, before, 0.0), axis=-1, keepdims=True)
    run_ref[...] += jnp.sum(onehot, axis=0, keepdims=True)
    cnt_ref[...] = run_ref[...]

    info = jnp.where(col == 0, id1, 0.0)
    info = jnp.where(col == 1, id2, info)
    info = jnp.where(col == 2, c1, info)
    info = jnp.where(col == 3, c2, info)
    info = jnp.where(col == 4, r1, info)
    info = jnp.where(col == 5, r2, info)
    info_ref[...] = info


def _router(x, mods3, mod_row, layer, wr, br, tm=512):
    m, d = x.shape
    base = layer * MOD_ROWS
    return pl.pallas_call(
        _router_kernel,
        grid=(m // tm,),
        in_specs=[
            pl.BlockSpec((tm, d), lambda i: (i, 0)),
            pl.BlockSpec((1, 1, d), lambda i: (base + mod_row(i, tm), 0, 4)),
            pl.BlockSpec((1, 1, d), lambda i: (base + mod_row(i, tm), 0, 3)),
            pl.BlockSpec((d, V7X_LANES), lambda i: (0, 0)),
            pl.BlockSpec((1, V7X_LANES), lambda i: (0, 0)),
        ],
        out_specs=[
            pl.BlockSpec((tm, V7X_LANES), lambda i: (i, 0)),
            pl.BlockSpec((1, V7X_LANES), lambda i: (0, 0)),
        ],
        out_shape=[
            jax.ShapeDtypeStruct((m, V7X_LANES), F32),
            jax.ShapeDtypeStruct((1, V7X_LANES), F32),
        ],
        scratch_shapes=[pltpu.VMEM((1, V7X_LANES), F32)],
        compiler_params=_cparams(("arbitrary",), 4 * tm * d * 4 + 2 * d * V7X_LANES * 4 + (8 << 20)),
    )(x, mods3, mods3, wr, br)


def _pos_kernel(info_ref, off_ref, pos_ref):
    info = info_ref[...]
    tm, width = info.shape
    col = lax.broadcasted_iota(jnp.int32, (tm, width), 1).astype(F32)
    off = off_ref[...]
    p1 = jnp.sum(jnp.where(col == info[:, 0:1], off, 0.0), axis=-1, keepdims=True) + info[:, 4:5]
    p2 = jnp.sum(jnp.where(col == info[:, 1:2], off, 0.0), axis=-1, keepdims=True) + info[:, 5:6]
    pm = jnp.where(col == 0, p1, jnp.where(col == 1, p2, 0.0))
    pos_ref[0] = jnp.transpose(pm)[:8, :].astype(jnp.int32)


def _positions(info, row_off, tm):
    m = info.shape[0]
    return pl.pallas_call(
        _pos_kernel,
        grid=(m // tm,),
        in_specs=[
            pl.BlockSpec((tm, V7X_LANES), lambda i: (i, 0)),
            pl.BlockSpec((1, V7X_LANES), lambda i: (0, 0)),
        ],
        out_specs=pl.BlockSpec((1, 8, tm), lambda i: (i, 0, 0)),
        out_shape=jax.ShapeDtypeStruct((m // tm, 8, tm), jnp.int32),
        compiler_params=_cparams(("arbitrary",), 16 << 20),
    )(info, row_off)


def _dispatch_kernel(tm, pad_bits, ps_ref, pl_ref, pos_ref, x_ref, sc_ref, sh_ref, xs_hbm,
                     hbuf, zbuf, sems, zsem):
    i = pl.program_id(0)
    n_steps = pl.num_programs(0)
    n_experts = ps_ref.shape[0] - 1
    zrows = zbuf.shape[0]
    max_tail = (xs_hbm.shape[0] - 2 * x_ref.shape[0] * n_steps) // zrows

    def fill_copies(do):
        for e in range(n_experts):
            start = ps_ref[e]
            rem = pl_ref[e]
            for r in range(SUBLANES - 1):
                @pl.when(r < (rem & (SUBLANES - 1)))
                def _(start=start, r=r):
                    do(pltpu.make_async_copy(zbuf.at[pl.ds(0, 1)], xs_hbm.at[pl.ds(start + r, 1)], zsem))

            off = start + rem
            for bit in pad_bits:
                off = off - (rem & bit)

                @pl.when((rem & bit) != 0)
                def _(off=off, bit=bit):
                    dst = xs_hbm.at[pl.ds(pl.multiple_of(off, SUBLANES), bit)]
                    do(pltpu.make_async_copy(zbuf.at[pl.ds(0, bit)], dst, zsem))
        for t in range(max_tail):
            @pl.when(t < pl_ref[n_experts])
            def _(t=t):
                dst = xs_hbm.at[pl.ds(pl.multiple_of(ps_ref[n_experts] + t * zrows, SUBLANES), zrows)]
                do(pltpu.make_async_copy(zbuf, dst, zsem))

    @pl.when(i == 0)
    def _():
        zbuf[...] = jnp.zeros_like(zbuf)
        fill_copies(lambda cp: cp.start())

    def scatter_start(sub):
        for r in range(tm):
            for slot in range(2):
                pltpu.make_async_copy(hbuf.at[sub, pl.ds(r, 1)],
                                      xs_hbm.at[pl.ds(pos_ref[0, slot, sub * tm + r], 1)],
                                      sems.at[sub]).start(priority=slot)

    def scatter_wait(sub):
        for _ in range(2):
            pltpu.make_async_copy(hbuf.at[sub], xs_hbm.at[pl.ds(0, tm)], sems.at[sub]).wait()

    for sub in range(2):
        rows = pl.ds(sub * tm, tm)
        hbuf[sub] = x_ref[rows, :] * (1.0 + sc_ref[0]) + sh_ref[0]
        scatter_start(sub)
        if sub == 0:
            @pl.when(i > 0)
            def _():
                scatter_wait(1)
        else:
            scatter_wait(0)

    @pl.when(i == n_steps - 1)
    def _():
        scatter_wait(1)
        fill_copies(lambda cp: cp.wait())


def _dispatch(x, posT, pad_start, pad_len, mods3, mod_row, layer, n_rows, tm, moe_tm):
    m, d = x.shape
    base = layer * MOD_ROWS
    step = 2 * tm
    pad_bits = tuple(1 << b for b in range(moe_tm.bit_length() - 2, SUBLANES.bit_length() - 2, -1))
    grid_spec = pltpu.PrefetchScalarGridSpec(
        num_scalar_prefetch=2,
        grid=(m // step,),
        in_specs=[
            pl.BlockSpec((1, 8, step), lambda i, ps, pn: (i, 0, 0), memory_space=pltpu.SMEM),
            pl.BlockSpec((step, d), lambda i, ps, pn: (i, 0)),
            pl.BlockSpec((1, 1, d), lambda i, ps, pn: (base + mod_row(i, step), 0, 4)),
            pl.BlockSpec((1, 1, d), lambda i, ps, pn: (base + mod_row(i, step), 0, 3)),
        ],
        out_specs=pl.BlockSpec(memory_space=pl.ANY),
        scratch_shapes=[
            pltpu.VMEM((2, tm, d), F32),
            pltpu.VMEM((moe_tm // 2, d), F32),
            pltpu.SemaphoreType.DMA((2,)),
            pltpu.SemaphoreType.DMA(()),
        ],
    )
    return pl.pallas_call(
        functools.partial(_dispatch_kernel, tm, pad_bits),
        grid_spec=grid_spec,
        out_shape=jax.ShapeDtypeStruct((n_rows, d), F32),
        compiler_params=_cparams(("arbitrary",), 2 * step * d * 4 + 3 * tm * d * 4 + (8 << 20)),
    )(pad_start, pad_len, posT, x, mods3, mods3)


def _moe_kernel(layer, te_ref, nu_ref, nxt_ref, par_ref, xs_ref, wg_hbm, wu_hbm, wd_hbm, ys_ref,
                wg_f, wu_f, wd_f, wg_b, wu_b, wd_b, sems):
    i = pl.program_id(0)
    used = i < nu_ref[0]
    expert = te_ref[i]
    first = used & ((i == 0) | (expert != te_ref[jnp.maximum(i - 1, 0)]))

    def weight_copies(e, slot):
        return (pltpu.make_async_copy(wg_hbm.at[layer, e], wg_f.at[slot], sems.at[slot]),
                pltpu.make_async_copy(wu_hbm.at[layer, e], wu_f.at[slot], sems.at[slot]),
                pltpu.make_async_copy(wd_hbm.at[layer, e], wd_f.at[slot], sems.at[slot]))

    @pl.when(jnp.logical_not(used))
    def _():
        ys_ref[...] = jnp.zeros_like(ys_ref)

    @pl.when(i == 0)
    def _():
        for cp in weight_copies(expert, par_ref[0]):
            cp.start(priority=1)

    @pl.when(first)
    def _():
        slot = par_ref[i]
        for cp in weight_copies(expert, slot):
            cp.wait()

        @pl.when(nxt_ref[i] >= 0)
        def _():
            for cp in weight_copies(nxt_ref[i], 1 - slot):
                cp.start(priority=1)

        wg_b[...] = wg_f[slot].astype(BF16)
        wu_b[...] = wu_f[slot].astype(BF16)
        wd_b[...] = wd_f[slot].astype(BF16)

    @pl.when(used)
    def _():
        x = xs_ref[...].astype(BF16)
        g = _dot(x, wg_b[...])
        u = _dot(x, wu_b[...])
        hid = (_silu(g) * u).astype(BF16)
        ys_ref[...] = _dot(hid, wd_b[...])


def _moe_experts(xs, tile_expert, n_used, next_expert, slot, w_gate, w_up, w_down, layer, tm):
    n_rows, d = xs.shape
    f = w_gate.shape[-1]
    n_tiles = n_rows // tm
    grid_spec = pltpu.PrefetchScalarGridSpec(
        num_scalar_prefetch=4,
        grid=(n_tiles,),
        in_specs=[
            pl.BlockSpec((tm, d), lambda i, te, nu, nx, pr: (jnp.maximum(jnp.minimum(i, nu[0] - 1), 0), 0)),
            pl.BlockSpec(memory_space=pl.ANY),
            pl.BlockSpec(memory_space=pl.ANY),
            pl.BlockSpec(memory_space=pl.ANY),
        ],
        out_specs=pl.BlockSpec((tm, d), lambda i, te, nu, nx, pr: (i, 0)),
        scratch_shapes=[
            pltpu.VMEM((2, d, f), F32), pltpu.VMEM((2, d, f), F32), pltpu.VMEM((2, f, d), F32),
            pltpu.VMEM((d, f), BF16), pltpu.VMEM((d, f), BF16), pltpu.VMEM((f, d), BF16),
            pltpu.SemaphoreType.DMA((2,)),
        ],
    )
    return pl.pallas_call(
        functools.partial(_moe_kernel, layer),
        grid_spec=grid_spec,
        out_shape=jax.ShapeDtypeStruct((n_rows, d), F32),
        compiler_params=_cparams(("arbitrary",), 6 * d * f * 4 + 3 * d * f * 2 + 8 * tm * d * 4 + (8 << 20)),
    )(tile_expert, n_used, next_expert, slot, xs, w_gate, w_up, w_down)


def _combine_ln_kernel(alpha, tm, p_steps, pos_ref, nxt_ref, ys_hbm, info_ref, x_ref, gate_ref, g_ref, b_ref,
                       *rest):
    i = pl.program_id(0)
    n_steps = pl.num_programs(0)
    if p_steps is None:
        nsc_ref, nsh_ref, o_ref, hb_ref, buf, sems = rest
    else:
        op_ref, os_ref, buf, sems = rest

    def gather_start(sub, p_ref):
        for r in range(tm):
            for slot in range(2):
                pltpu.make_async_copy(ys_hbm.at[pl.ds(p_ref[0, slot, sub * tm + r], 1)],
                                      buf.at[sub, slot, pl.ds(r, 1)], sems.at[sub]).start(priority=slot)

    def gather_wait(sub):
        for slot in range(2):
            pltpu.make_async_copy(ys_hbm.at[pl.ds(0, tm)], buf.at[sub, slot], sems.at[sub]).wait()

    def finish(sub):
        rows = pl.ds(sub * tm, tm)
        info = info_ref[rows, :]
        y = info[:, 2:3] * buf[sub, 0] + info[:, 3:4] * buf[sub, 1]
        z = alpha * x_ref[rows, :] + gate_ref[0] * y
        res = _layer_norm(z, g_ref[0], b_ref[0])
        if p_steps is None:
            o_ref[rows, :] = res
            hb_ref[rows, :] = (res * (1.0 + nsc_ref[0]) + nsh_ref[0]).astype(BF16)
        else:
            @pl.when(i < p_steps)
            def _():
                op_ref[rows, :] = res

            @pl.when(i >= p_steps)
            def _():
                os_ref[rows, :] = res

    @pl.when(i == 0)
    def _():
        gather_start(0, pos_ref)

    gather_start(1, pos_ref)
    gather_wait(0)
    finish(0)

    @pl.when(i + 1 < n_steps)
    def _():
        gather_start(0, nxt_ref)

    gather_wait(1)
    finish(1)


def _combine_ln(ys, posT, info, x, mods3, mod_row, layer, ln_g, ln_b, alpha, tm, last, p_rows):
    m, d = x.shape
    step = 2 * tm
    n_steps = m // step
    base = layer * MOD_ROWS
    in_specs = [
        pl.BlockSpec((1, 8, step), lambda i: (i, 0, 0), memory_space=pltpu.SMEM),
        pl.BlockSpec((1, 8, step), lambda i: (jnp.minimum(i + 1, n_steps - 1), 0, 0), memory_space=pltpu.SMEM),
        pl.BlockSpec(memory_space=pl.ANY),
        pl.BlockSpec((step, V7X_LANES), lambda i: (i, 0)),
        pl.BlockSpec((step, d), lambda i: (i, 0)),
        pl.BlockSpec((1, 1, d), lambda i: (base + mod_row(i, step), 0, 5)),
        pl.BlockSpec((1, 1, d), lambda i: (layer, 0, 0)),
        pl.BlockSpec((1, 1, d), lambda i: (layer, 0, 0)),
    ]
    args = [posT, posT, ys, info, x, mods3, ln_g, ln_b]
    if last:
        p_steps = p_rows // step
        out_specs = [
            pl.BlockSpec((step, d), lambda i: (jnp.minimum(i, p_steps - 1), 0)),
            pl.BlockSpec((step, d), lambda i: (jnp.maximum(i - p_steps, 0), 0)),
        ]
        out_shape = [jax.ShapeDtypeStruct((p_rows, d), F32), jax.ShapeDtypeStruct((m - p_rows, d), F32)]
    else:
        p_steps = None
        nbase = (layer + 1) * MOD_ROWS
        in_specs += [
            pl.BlockSpec((1, 1, d), lambda i: (nbase + mod_row(i, step), 0, 1)),
            pl.BlockSpec((1, 1, d), lambda i: (nbase + mod_row(i, step), 0, 0)),
        ]
        args += [mods3, mods3]
        out_specs = [pl.BlockSpec((step, d), lambda i: (i, 0)), pl.BlockSpec((step, d), lambda i: (i, 0))]
        out_shape = [jax.ShapeDtypeStruct((m, d), F32), jax.ShapeDtypeStruct((m, d), BF16)]
    return pl.pallas_call(
        functools.partial(_combine_ln_kernel, alpha, tm, p_steps),
        grid=(n_steps,),
        in_specs=in_specs,
        out_specs=out_specs,
        out_shape=out_shape,
        scratch_shapes=[pltpu.VMEM((2, 2, tm, d), F32), pltpu.SemaphoreType.DMA((2,))],
        compiler_params=_cparams(("arbitrary",), 6 * step * d * 4 + 4 * tm * d * 4 + 6 * tm * d * 4 + (8 << 20)),
    )(*args)


def _hier_moe_ln(x, mods3, mod_row, layer, w_rg, b_rg, w_re, b_re, w_gate, w_up, w_down, ln_g, ln_b, alpha,
                 last, p_rows):
    m, d = x.shape
    n_e = N_EXPERTS
    pad = V7X_LANES - N_GROUPS - n_e
    wr = jnp.concatenate(
        [w_rg[layer], w_re[layer].transpose(1, 0, 2).reshape(d, n_e), jnp.zeros((d, pad), F32)], axis=1)
    br = jnp.concatenate([b_rg[layer], b_re[layer].reshape(n_e), jnp.zeros((pad,), F32)])[None, :]
    info, cnt = _router(x, mods3, mod_row, layer, wr, br)

    tm = MOE_TM
    n_tiles = 2 * m // tm + n_e
    counts = cnt[0, :n_e].astype(jnp.int32)
    tiles_e = (counts + tm - 1) // tm
    tile_end = jnp.cumsum(tiles_e)
    n_used = tile_end[-1]
    row_off = (tile_end - tiles_e) * tm
    tile_ids = jnp.minimum(jnp.arange(n_tiles, dtype=jnp.int32), n_used - 1)
    tile_expert = jnp.sum((tile_ids[:, None] >= tile_end[None, :]).astype(jnp.int32), axis=1)
    group_end = tile_end[tile_expert]
    next_expert = jnp.where(group_end < n_used, tile_expert[jnp.minimum(group_end, n_tiles - 1)], -1)
    is_first = jnp.concatenate([jnp.ones((1,), jnp.int32), (tile_expert[1:] != tile_expert[:-1]).astype(jnp.int32)])
    slot = (jnp.cumsum(is_first) - 1) % 2
    zrows = tm // 2
    pad_start = jnp.concatenate([row_off + counts, (n_used * tm)[None]])
    pad_len = jnp.concatenate([tiles_e * tm - counts, ((n_tiles - n_used) * (tm // zrows))[None]])
    row_off_f = jnp.concatenate([row_off.astype(F32), jnp.zeros((V7X_LANES - n_e,), F32)])[None, :]

    ct = COMB_TM
    posT = _positions(info, row_off_f, 2 * ct)
    xs = _dispatch(x, posT, pad_start, pad_len, mods3, mod_row, layer, n_tiles * tm, ct, tm)
    ys = _moe_experts(xs, tile_expert, n_used.reshape(1), next_expert.astype(jnp.int32), slot.astype(jnp.int32),
                      w_gate, w_up, w_down, layer, tm)
    return _combine_ln(ys, posT, info, x, mods3, mod_row, layer, ln_g, ln_b, alpha, ct, last, p_rows)


def kernel(x_prompt, x_sample, state_ret, cache_na_k, cache_na_v, c, c_ctx, w_mod, b_mod, ln1_g, ln1_b, ln2_g, ln2_b, w_ret_in, ret_decay_logit, ret_gn_g, ret_gn_b, w_ret_out, w_na_in, na_rpb, w_na_out, w_rg, b_rg, w_re, b_re, w_gate, w_up, w_down):
    bp, seq, d = x_prompt.shape
    bs, dec_seq, _ = x_sample.shape
    depth = w_mod.shape[0]
    slab = dec_seq
    assert slab % seq == 0 and (bp * seq) % slab == 0 and bs + 1 <= MOD_ROWS
    p_rows = bp * seq
    p_slabs = p_rows // slab
    alpha = (2.0 * depth) ** 0.25

    def mod_row(i, tm):
        start = i * tm
        return jnp.where(start < p_rows, 0, 1 + (start - p_rows) // dec_seq)

    cc = jnp.concatenate([c_ctx[None, :], c, jnp.zeros((MOD_ROWS - 1 - bs, d), F32)], axis=0)
    mods = _modulation(cc, w_mod, b_mod)
    mods3 = mods.reshape(depth * MOD_ROWS, 1, 6 * d)
    x, hb = _prologue(x_prompt.reshape(p_rows, d), x_sample.reshape(bs * dec_seq, d), mods3, mod_row)

    ln1_g3, ln1_b3 = ln1_g.reshape(depth, 1, d), ln1_b.reshape(depth, 1, d)
    ln2_g3, ln2_b3 = ln2_g.reshape(depth, 1, d), ln2_b.reshape(depth, 1, d)

    new_ret, new_k, new_v = [], [], []
    for l in range(depth):
        j = l // 2
        if l % 2 == 0:
            dk = state_ret.shape[-2]
            proj = _inproj(hb, w_ret_in, j)
            log_g = -jax.nn.softplus(-ret_decay_logit[j].astype(F32))
            t = jnp.arange(slab)
            nf = dk // 4
            inv_freq = ROPE_BASE ** (-jnp.arange(nf, dtype=F32) / nf)
            ang_r = (t // GRID_W).astype(F32)[:, None] * inv_freq
            ang_c = (t % GRID_W).astype(F32)[:, None] * inv_freq
            cos = jnp.concatenate([jnp.cos(ang_r)] * 2 + [jnp.cos(ang_c)] * 2, axis=1)
            sin = jnp.concatenate([-jnp.sin(ang_r), jnp.sin(ang_r), -jnp.sin(ang_c), jnp.sin(ang_c)], axis=1)
            y, st = _retention(proj, log_g, cos, sin, state_ret, j, ret_gn_g[j][:, None, :],
                               ret_gn_b[j][:, None, :], p_slabs, slab)
            new_ret.append(st)
            x = _outproj_ln(y, w_ret_out[j].astype(BF16), x, mods3, mod_row, l, ln1_g3, ln1_b3, alpha)
        else:
            proj = _inproj(hb, w_na_in, j)
            o, nk, nv = _attention(proj, cache_na_k, cache_na_v, j, _na_strips(na_rpb[j]), p_slabs, slab, seq)
            new_k.append(nk[:, 0])
            new_v.append(nv[:, 0])
            x = _outproj_ln(o, w_na_out[j].astype(BF16), x, mods3, mod_row, l, ln1_g3, ln1_b3, alpha)
        x, hb = _hier_moe_ln(x, mods3, mod_row, l, w_rg, b_rg, w_re, b_re, w_gate, w_up, w_down,
                             ln2_g3, ln2_b3, alpha, l == depth - 1, p_rows)

    y_prompt = x.reshape(bp, seq, d)
    y_sample = hb.reshape(bs, dec_seq, d)
    return (y_prompt, y_sample, jnp.stack(new_ret, axis=1), jnp.stack(new_k, axis=1), jnp.stack(new_v, axis=1))
```

```python
import functools

import jax
import jax.numpy as jnp
from jax import lax
from jax.experimental import pallas as pl
from jax.experimental.pallas import tpu as pltpu

F32 = jnp.float32
BF16 = jnp.bfloat16

GRID_W = 64
WIN_H = 8
WIN_W = 16
N_GROUPS = 4
N_EXP_PER_GROUP = 8
N_EXPERTS = N_GROUPS * N_EXP_PER_GROUP
ROPE_BASE = 10000.0
LN_EPS = 1e-5
NEG_INF = -1e30

V7X_VMEM_BYTES = 64 * 1024 * 1024
V7X_LANES = 128
SUBLANES = 8
MOD_ROWS = 8

RET_CHUNK = 256
ATTN_QB = 256
ATTN_HEADS_PER_STEP = 2
MOE_TM = 256
COMB_TM = 256


def _cparams(sem, vmem_bytes):
    limit = int(min(max(vmem_bytes, 16 * 1024 * 1024), V7X_VMEM_BYTES - 6 * 1024 * 1024))
    return pltpu.CompilerParams(dimension_semantics=sem, vmem_limit_bytes=limit)


def _silu(x):
    return x / (1.0 + jnp.exp(-x))


def _dot(a, b):
    return jnp.dot(a, b, preferred_element_type=F32)


def _dot_nt(a, b):
    return lax.dot_general(a, b, (((1,), (1,)), ((), ())), preferred_element_type=F32)


def _layer_norm(z, g, b):
    mu = jnp.mean(z, axis=-1, keepdims=True)
    zc = z - mu
    var = jnp.mean(zc * zc, axis=-1, keepdims=True)
    return zc * lax.rsqrt(var + LN_EPS) * g + b


def _mod_kernel(c_ref, w_ref, b_ref, o_ref):
    s = _silu(c_ref[...]).astype(BF16)
    o_ref[0] = _dot(s, w_ref[0].astype(BF16)) + b_ref[0]


def _modulation(cc, w_mod, b_mod):
    depth, d, n6 = w_mod.shape
    tn = 1024
    return pl.pallas_call(
        _mod_kernel,
        grid=(depth, n6 // tn),
        in_specs=[
            pl.BlockSpec((MOD_ROWS, d), lambda l, j: (0, 0)),
            pl.BlockSpec((1, d, tn), lambda l, j: (l, 0, j)),
            pl.BlockSpec((1, 1, tn), lambda l, j: (l, 0, j)),
        ],
        out_specs=pl.BlockSpec((1, MOD_ROWS, tn), lambda l, j: (l, 0, j)),
        out_shape=jax.ShapeDtypeStruct((depth, MOD_ROWS, n6), F32),
        compiler_params=_cparams(("arbitrary", "arbitrary"), 3 * d * tn * 4 + (4 << 20)),
    )(cc, w_mod, b_mod.reshape(depth, 1, n6))


def _prologue_kernel(p_tiles, xp_ref, xs_ref, sc_ref, sh_ref, x_ref, hb_ref):
    @pl.when(pl.program_id(0) < p_tiles)
    def _():
        x = xp_ref[...]
        x_ref[...] = x
        hb_ref[...] = (x * (1.0 + sc_ref[0]) + sh_ref[0]).astype(BF16)

    @pl.when(pl.program_id(0) >= p_tiles)
    def _():
        x = xs_ref[...]
        x_ref[...] = x
        hb_ref[...] = (x * (1.0 + sc_ref[0]) + sh_ref[0]).astype(BF16)


def _prologue(xp, xs, mods3, mod_row, tm=512):
    p_rows, d = xp.shape
    m = p_rows + xs.shape[0]
    p_tiles = p_rows // tm
    return pl.pallas_call(
        functools.partial(_prologue_kernel, p_tiles),
        grid=(m // tm,),
        in_specs=[
            pl.BlockSpec((tm, d), lambda i: (jnp.minimum(i, p_tiles - 1), 0)),
            pl.BlockSpec((tm, d), lambda i: (jnp.maximum(i - p_tiles, 0), 0)),
            pl.BlockSpec((1, 1, d), lambda i: (mod_row(i, tm), 0, 1)),
            pl.BlockSpec((1, 1, d), lambda i: (mod_row(i, tm), 0, 0)),
        ],
        out_specs=[pl.BlockSpec((tm, d), lambda i: (i, 0)), pl.BlockSpec((tm, d), lambda i: (i, 0))],
        out_shape=[jax.ShapeDtypeStruct((m, d), F32), jax.ShapeDtypeStruct((m, d), BF16)],
        compiler_params=_cparams(("arbitrary",), 8 * tm * d * 4 + (8 << 20)),
    )(xp, xs, mods3, mods3)


def _inproj_kernel(h_ref, w_ref, o_ref):
    o_ref[...] = _dot(h_ref[...], w_ref[...].astype(BF16))


def _inproj(hb, w_all, w_idx, tm=2048, tn=512):
    m, d = hb.shape
    n = w_all.shape[-1]
    return pl.pallas_call(
        _inproj_kernel,
        grid=(m // tm, n // tn),
        in_specs=[
            pl.BlockSpec((tm, d), lambda i, j: (i, 0)),
            pl.BlockSpec((None, d, tn), lambda i, j: (w_idx, 0, j)),
        ],
        out_specs=pl.BlockSpec((tm, tn), lambda i, j: (i, j)),
        out_shape=jax.ShapeDtypeStruct((m, n), F32),
        compiler_params=_cparams(
            ("arbitrary", "arbitrary"),
            2 * tm * d * 2 + 2 * d * tn * 4 + d * tn * 2 + 3 * tm * tn * 4 + (4 << 20)),
    )(hb, w_all)


def _outproj_ln_kernel(alpha, y_ref, w_ref, x_ref, gate_ref, g_ref, b_ref, sc_ref, sh_ref, wr_ref, br_ref,
                       o_ref, info_ref, cnt_ref, run_ref):
    @pl.when(pl.program_id(0) == 0)
    def _():
        run_ref[...] = jnp.zeros_like(run_ref)

    z = alpha * x_ref[...] + gate_ref[0] * _dot(y_ref[...], w_ref[...])
    x1 = _layer_norm(z, g_ref[0], b_ref[0])
    o_ref[...] = x1
    info, run = _route(x1 * (1.0 + sc_ref[0]) + sh_ref[0], wr_ref[...], br_ref[...], run_ref[...])
    info_ref[...] = info
    run_ref[...] = run
    cnt_ref[...] = run


def _outproj_ln(y, w, x, mods3, mod_row, layer, ln_g, ln_b, wr, br, alpha, tm=512):
    m, kdim = y.shape
    d = x.shape[1]
    base = layer * MOD_ROWS
    return pl.pallas_call(
        functools.partial(_outproj_ln_kernel, alpha),
        grid=(m // tm,),
        in_specs=[
            pl.BlockSpec((tm, kdim), lambda i: (i, 0)),
            pl.BlockSpec((kdim, d), lambda i: (0, 0)),
            pl.BlockSpec((tm, d), lambda i: (i, 0)),
            pl.BlockSpec((1, 1, d), lambda i: (base + mod_row(i, tm), 0, 2)),
            pl.BlockSpec((1, 1, d), lambda i: (layer, 0, 0)),
            pl.BlockSpec((1, 1, d), lambda i: (layer, 0, 0)),
            pl.BlockSpec((1, 1, d), lambda i: (base + mod_row(i, tm), 0, 4)),
            pl.BlockSpec((1, 1, d), lambda i: (base + mod_row(i, tm), 0, 3)),
            pl.BlockSpec((d, V7X_LANES), lambda i: (0, 0)),
            pl.BlockSpec((1, V7X_LANES), lambda i: (0, 0)),
        ],
        out_specs=[
            pl.BlockSpec((tm, d), lambda i: (i, 0)),
            pl.BlockSpec((tm, V7X_LANES), lambda i: (i, 0)),
            pl.BlockSpec((1, V7X_LANES), lambda i: (0, 0)),
        ],
        out_shape=[
            jax.ShapeDtypeStruct((m, d), F32),
            jax.ShapeDtypeStruct((m, V7X_LANES), F32),
            jax.ShapeDtypeStruct((1, V7X_LANES), F32),
        ],
        scratch_shapes=[pltpu.VMEM((1, V7X_LANES), F32)],
        compiler_params=_cparams(
            ("arbitrary",), kdim * d * 2 + 2 * tm * kdim * 2 + 9 * tm * d * 4 + d * V7X_LANES * 4 + (4 << 20)),
    )(y, w, x, mods3, ln_g, ln_b, mods3, mods3, wr, br)


def _rope(x, cos, sin):
    half = V7X_LANES // 2
    sw = jnp.concatenate(
        [pltpu.roll(x[:, :V7X_LANES], half, 1), pltpu.roll(x[:, V7X_LANES:], half, 1)], axis=1)
    return x * cos + sw * sin


def _group_norm(o, g, b):
    mu = jnp.mean(o, axis=-1, keepdims=True)
    oc = o - mu
    var = jnp.mean(oc * oc, axis=-1, keepdims=True)
    return oc * lax.rsqrt(var + LN_EPS) * g + b


def _ret_kernel(p_slabs, dk, lg_ref, q_ref, k_ref, v_ref, gf_ref, gb_ref, cos_ref, sin_ref, s0_ref,
                gng_ref, gnb_ref, y_ref, st_ref, q_s, k_s, yacc):
    h = pl.program_id(0)
    s = pl.program_id(1)
    is_prompt = s < p_slabs
    c = RET_CHUNK
    n_chunks = q_ref.shape[0] // c

    def run(prompt):
        ri = lax.broadcasted_iota(jnp.int32, (c, c), 0)
        ci = lax.broadcasted_iota(jnp.int32, (c, c), 1)
        diff = (ri - ci).astype(F32)
        pos = lax.broadcasted_iota(jnp.int32, (c, 1), 0).astype(F32)
        gn_g = gng_ref[0]
        gn_b = gnb_ref[0]
        scale = dk ** -0.5
        if not prompt:
            q_s[...] = _rope(q_ref[...], cos_ref[...], sin_ref[...])
            k_s[...] = _rope(k_ref[...], cos_ref[...], sin_ref[...]) * scale

        for direction in (0, 1):
            lg = jnp.full((1, 1), lg_ref[direction, h], F32)
            if direction == 0:
                mask = jnp.where(diff >= 0, jnp.exp(jnp.maximum(diff, 0.0) * lg), 0.0)
                q_dec = jnp.exp((pos + 1.0) * lg)
                k_dec = jnp.exp((c - 1.0 - pos) * lg)
                order = range(n_chunks)
                gate_ref = gf_ref
            else:
                mask = jnp.where(diff <= 0, jnp.exp(jnp.maximum(-diff, 0.0) * lg), 0.0)
                q_dec = jnp.exp((c - pos) * lg)
                k_dec = jnp.exp(pos * lg)
                order = range(n_chunks - 1, -1, -1)
                gate_ref = gb_ref
            c_dec = jnp.exp(c * lg)
            state = None if prompt else s0_ref[0, direction, 0]
            for ch in order:
                rows = pl.ds(ch * c, c)
                if prompt:
                    qc = q_ref[rows, :]
                    kc = k_ref[rows, :] * scale
                else:
                    qc = q_s[rows, :]
                    kc = k_s[rows, :]
                vb = v_ref[rows, :].astype(BF16)
                inner = _dot_nt(qc.astype(BF16), kc.astype(BF16)) * mask
                o = _dot(inner.astype(BF16), vb)
                kv = _dot(jnp.transpose(kc * k_dec).astype(BF16), vb)
                if prompt:
                    st_ref[ch, direction, 0] = kv
                else:
                    o = o + _dot((qc * q_dec).astype(BF16), state.astype(BF16))
                    state = c_dec * state + kv
                contrib = _group_norm(o, gn_g, gn_b) * _silu(gate_ref[rows, :])
                if direction == 0:
                    yacc[rows, :] = contrib
                else:
                    y_ref[rows, :] = (yacc[rows, :] + contrib).astype(BF16)

    @pl.when(is_prompt)
    def _():
        run(True)

    @pl.when(jnp.logical_not(is_prompt))
    def _():
        run(False)


def _retention(proj, log_g, rope_cos, rope_sin, state_ret, ret_idx, gn_g, gn_b, p_slabs, slab):
    m = proj.shape[0]
    heads, dk, dv = state_ret.shape[-3:]
    n_slabs = m // slab
    per_slab = slab // RET_CHUNK
    v_blk0 = 2 * heads * dk // dv
    p_last = p_slabs - 1

    def s0_map(h, s):
        return (jnp.maximum(s - p_slabs, 0), ret_idx, 0, h, 0, 0)

    kernel = functools.partial(_ret_kernel, p_slabs, dk)
    return pl.pallas_call(
        kernel,
        grid=(heads, n_slabs),
        in_specs=[
            pl.BlockSpec(memory_space=pltpu.SMEM),
            pl.BlockSpec((slab, dk), lambda h, s: (s, h)),
            pl.BlockSpec((slab, dk), lambda h, s: (s, heads + h)),
            pl.BlockSpec((slab, dv), lambda h, s: (s, v_blk0 + h)),
            pl.BlockSpec((slab, dv), lambda h, s: (s, v_blk0 + heads + h)),
            pl.BlockSpec((slab, dv), lambda h, s: (s, v_blk0 + 2 * heads + h)),
            pl.BlockSpec((slab, dk), lambda h, s: (0, 0)),
            pl.BlockSpec((slab, dk), lambda h, s: (0, 0)),
            pl.BlockSpec((1, None, 2, 1, dk, dv), s0_map),
            pl.BlockSpec((1, 1, dv), lambda h, s: (h, 0, 0)),
            pl.BlockSpec((1, 1, dv), lambda h, s: (h, 0, 0)),
        ],
        out_specs=[
            pl.BlockSpec((slab, dv), lambda h, s: (s, h)),
            pl.BlockSpec((per_slab, 2, 1, dk, dv), lambda h, s: (jnp.minimum(s, p_last), 0, h, 0, 0)),
        ],
        out_shape=[
            jax.ShapeDtypeStruct((m, heads * dv), BF16),
            jax.ShapeDtypeStruct((p_slabs * per_slab, 2, heads, dk, dv), F32),
        ],
        scratch_shapes=[
            pltpu.VMEM((slab, dk), F32),
            pltpu.VMEM((slab, dk), F32),
            pltpu.VMEM((slab, dv), F32),
        ],
        compiler_params=_cparams(
            ("arbitrary", "arbitrary"),
            2 * (2 * slab * dk + 3 * slab * dv) * 4 + 4 * slab * dk * 4 + 2 * 2 * dk * dv * 4
            + 2 * slab * dv * 2 + 2 * per_slab * 2 * dk * dv * 4 + 2 * slab * dk * 4 + slab * dv * 4
            + (8 << 20)),
    )(log_g, proj, proj, proj, proj, proj, rope_cos, rope_sin, state_ret, gn_g, gn_b)


def _na_window(rq, rows):
    kh = min(WIN_H, rows)
    return min(max(rq - kh // 2, 0), rows - kh), kh


def _attn_kernel(p_slabs, seq, dh, q_ref, k_ref, v_ref, ck_ref, cv_ref, strip_ref, o_ref, nk_ref, nv_ref):
    s = pl.program_id(1)
    is_prompt = s < p_slabs
    slab = q_ref.shape[0]
    scale = dh ** -0.5

    def prompt_head(hh):
        cols = pl.ds(hh * dh, dh)
        for b in range(slab // seq):
            rows = pl.ds(b * seq, seq)
            k = k_ref[rows, cols]
            v = v_ref[rows, cols]
            nk_ref[b, 0, hh] = k
            nv_ref[b, 0, hh] = v
            q = (q_ref[rows, cols] * scale).astype(BF16)
            sc = _dot_nt(q, k.astype(BF16))
            p = jnp.exp(sc - jnp.max(sc, axis=-1, keepdims=True))
            denom = jnp.sum(p, axis=-1, keepdims=True)
            o_ref[rows, cols] = (_dot(p.astype(BF16), v.astype(BF16)) / denom).astype(BF16)

    def latent_head(hh):
        cols = pl.ds(hh * dh, dh)
        grid_rows = slab // GRID_W
        rows_per_block = ATTN_QB // GRID_W
        ckb = ck_ref[0, 0, hh].astype(BF16)
        cvb = cv_ref[0, 0, hh].astype(BF16)
        for qb in range(slab // ATTN_QB):
            rq0 = qb * rows_per_block
            ka = _na_window(rq0, grid_rows)[0] // 2 * 2
            last0, kh = _na_window(rq0 + rows_per_block - 1, grid_rows)
            kb = -((last0 + kh) // -2) * 2
            n_keys = (kb - ka) * GRID_W
            keys = pl.ds(ka * GRID_W, n_keys)
            pieces = []
            for i in range(rows_per_block):
                rq = rq0 + i
                r0, kh = _na_window(rq, grid_rows)
                first = ka - rq + WIN_H - 1
                piece = strip_ref[hh, first % 2, :, pl.ds((first - first % 2) * GRID_W, n_keys)]
                if r0 != ka or r0 + kh != kb:
                    key_row = ka + (lax.broadcasted_iota(jnp.int32, piece.shape, 1) // GRID_W)
                    piece = jnp.where((key_row >= r0) & (key_row < r0 + kh), piece, NEG_INF)
                pieces.append(piece)
            bias = jnp.concatenate(pieces, axis=0)
            rows = pl.ds(qb * ATTN_QB, ATTN_QB)
            q = (q_ref[rows, cols] * scale).astype(BF16)
            s_loc = _dot_nt(q, k_ref[keys, cols].astype(BF16)) + bias
            s_ctx = _dot_nt(q, ckb)
            mx = jnp.maximum(jnp.max(s_loc, axis=-1, keepdims=True), jnp.max(s_ctx, axis=-1, keepdims=True))
            p_loc = jnp.exp(s_loc - mx)
            p_ctx = jnp.exp(s_ctx - mx)
            denom = jnp.sum(p_loc, axis=-1, keepdims=True) + jnp.sum(p_ctx, axis=-1, keepdims=True)
            o = _dot(p_loc.astype(BF16), v_ref[keys, cols].astype(BF16)) + _dot(p_ctx.astype(BF16), cvb)
            o_ref[rows, cols] = (o / denom).astype(BF16)

    heads_per_step = q_ref.shape[1] // dh

    @pl.when(is_prompt)
    def _():
        for hh in range(heads_per_step):
            prompt_head(hh)

    @pl.when(jnp.logical_not(is_prompt))
    def _():
        for hh in range(heads_per_step):
            latent_head(hh)


def _attention(proj, cache_k, cache_v, cache_idx, strips, p_slabs, slab, seq):
    m = proj.shape[0]
    heads = cache_k.shape[2]
    dh = cache_k.shape[-1]
    past = cache_k.shape[-2]
    n_slabs = m // slab
    per_slab = slab // seq
    p_last = p_slabs - 1

    hp = ATTN_HEADS_PER_STEP
    groups = heads // hp

    def ctx_map(h, s):
        return (jnp.maximum(s - p_slabs, 0), cache_idx, h, 0, 0)

    def new_map(h, s):
        return (jnp.minimum(s, p_last), 0, h, 0, 0)

    kernel = functools.partial(_attn_kernel, p_slabs, seq, dh)
    return pl.pallas_call(
        kernel,
        grid=(groups, n_slabs),
        in_specs=[
            pl.BlockSpec((slab, hp * dh), lambda h, s: (s, h)),
            pl.BlockSpec((slab, hp * dh), lambda h, s: (s, groups + h)),
            pl.BlockSpec((slab, hp * dh), lambda h, s: (s, 2 * groups + h)),
            pl.BlockSpec((1, 1, hp, past, dh), ctx_map),
            pl.BlockSpec((1, 1, hp, past, dh), ctx_map),
            pl.BlockSpec((hp,) + strips.shape[1:], lambda h, s: (h, 0, 0, 0)),
        ],
        out_specs=[
            pl.BlockSpec((slab, hp * dh), lambda h, s: (s, h)),
            pl.BlockSpec((per_slab, 1, hp, seq, dh), new_map),
            pl.BlockSpec((per_slab, 1, hp, seq, dh), new_map),
        ],
        out_shape=[
            jax.ShapeDtypeStruct((m, heads * dh), BF16),
            jax.ShapeDtypeStruct((p_slabs * per_slab, 1, heads, seq, dh), F32),
            jax.ShapeDtypeStruct((p_slabs * per_slab, 1, heads, seq, dh), F32),
        ],
        compiler_params=_cparams(
            ("arbitrary", "arbitrary"),
            hp * (2 * strips.shape[1] * strips.shape[2] * strips.shape[3] * 4 + 16 * slab * dh * 4)
            + 10 * ATTN_QB * (slab + past) * 4 + (8 << 20)),
    )(proj, proj, proj, cache_k, cache_v, strips)


def _na_strips(rpb):
    heads, n_dr, n_dc = rpb.shape
    col = jnp.arange(GRID_W)
    c0 = jnp.clip(col - WIN_W // 2, 0, GRID_W - WIN_W)
    col_ok = (col[None, :] >= c0[:, None]) & (col[None, :] < c0[:, None] + WIN_W)
    dc_idx = jnp.clip(col[None, :] - col[:, None], -(WIN_W - 1), WIN_W - 1) + WIN_W - 1
    onehot = (dc_idx[:, :, None] == jnp.arange(n_dc)[None, None, :]).astype(F32)
    tiles = jnp.einsum("hdk,qck->hqdc", rpb.astype(F32), onehot, precision=lax.Precision.HIGHEST)
    tiles = jnp.where(col_ok[None, :, None, :], tiles, NEG_INF)
    n_tiles = 2 * WIN_H
    neg = jnp.full((heads, GRID_W, n_tiles + 1 - n_dr, GRID_W), NEG_INF, F32)
    tiles = jnp.concatenate([tiles, neg], axis=2)
    both = jnp.stack([tiles[:, :, :n_tiles], tiles[:, :, 1:]], axis=1)
    return both.reshape(heads, 2, GRID_W, n_tiles * GRID_W)


def _route(hm, wr, br, run):
    h_hi = hm.astype(BF16)
    h_lo = (hm - h_hi.astype(F32)).astype(BF16)
    w_hi = wr.astype(BF16)
    w_lo = (wr - w_hi.astype(F32)).astype(BF16)
    logits = _dot(h_hi, w_hi) + (_dot(h_hi, w_lo) + _dot(h_lo, w_hi)) + br
    tm, width = logits.shape
    col = lax.broadcasted_iota(jnp.int32, (tm, width), 1).astype(F32)
    neg = jnp.float32(-3.0e38)

    gl = jnp.where(col < N_GROUPS, logits, neg)
    gmax = jnp.max(gl, axis=-1, keepdims=True)
    gsel = jnp.min(jnp.where(gl == gmax, col, float(width)), axis=-1, keepdims=True)
    p_g = 1.0 / jnp.sum(jnp.where(col < N_GROUPS, jnp.exp(gl - gmax), 0.0), axis=-1, keepdims=True)

    lo = N_GROUPS + N_EXP_PER_GROUP * gsel
    el = jnp.where((col >= lo) & (col < lo + N_EXP_PER_GROUP), logits, neg)
    e1 = jnp.max(el, axis=-1, keepdims=True)
    i1 = jnp.min(jnp.where(el == e1, col, float(width)), axis=-1, keepdims=True)
    el2 = jnp.where(col == i1, neg, el)
    e2 = jnp.max(el2, axis=-1, keepdims=True)
    i2 = jnp.min(jnp.where(el2 == e2, col, float(width)), axis=-1, keepdims=True)
    t = jnp.exp(e2 - e1)
    w1 = 1.0 / (1.0 + t)
    c1 = p_g * w1
    c2 = p_g * (t * w1)
    id1 = i1 - N_GROUPS
    id2 = i2 - N_GROUPS

    onehot = jnp.where((col == id1) | (col == id2), 1.0, 0.0)
    ri = lax.broadcasted_iota(jnp.int32, (tm, tm), 0)
    ci = lax.broadcasted_iota(jnp.int32, (tm, tm), 1)
    tri = jnp.where(ri > ci, 1.0, 0.0).astype(BF16)
    before = _dot(tri, onehot.astype(BF16)) + run
    r1 = jnp.sum(jnp.where(col == id1, before, 0.0), axis=-1, keepdims=True)
    r2 = jnp.sum(jnp.where(col == id2, before, 0.0), axis=-1, keepdims=True)

    info = jnp.where(col == 0, id1, 0.0)
    info = jnp.where(col == 1, id2, info)
    info = jnp.where(col == 2, c1, info)
    info = jnp.where(col == 3, c2, info)
    info = jnp.where(col == 4, r1, info)
    info = jnp.where(col == 5, r2, info)
    return info, run + jnp.sum(onehot, axis=0, keepdims=True)


def _pos_kernel(info_ref, off_ref, pos_ref):
    info = info_ref[...]
    tm, width = info.shape
    col = lax.broadcasted_iota(jnp.int32, (tm, width), 1).astype(F32)
    off = off_ref[...]
    p1 = jnp.sum(jnp.where(col == info[:, 0:1], off, 0.0), axis=-1, keepdims=True) + info[:, 4:5]
    p2 = jnp.sum(jnp.where(col == info[:, 1:2], off, 0.0), axis=-1, keepdims=True) + info[:, 5:6]
    pm = jnp.where(col == 0, p1, jnp.where(col == 1, p2, 0.0))
    pos_ref[0] = jnp.transpose(pm)[:8, :].astype(jnp.int32)


def _positions(info, row_off, tm):
    m = info.shape[0]
    return pl.pallas_call(
        _pos_kernel,
        grid=(m // tm,),
        in_specs=[
            pl.BlockSpec((tm, V7X_LANES), lambda i: (i, 0)),
            pl.BlockSpec((1, V7X_LANES), lambda i: (0, 0)),
        ],
        out_specs=pl.BlockSpec((1, 8, tm), lambda i: (i, 0, 0)),
        out_shape=jax.ShapeDtypeStruct((m // tm, 8, tm), jnp.int32),
        compiler_params=_cparams(("arbitrary",), 16 << 20),
    )(info, row_off)


def _dispatch_kernel(tm, pad_bits, ps_ref, pl_ref, pos_ref, x_ref, sc_ref, sh_ref, xs_hbm,
                     hbuf, zbuf, sems, zsem):
    i = pl.program_id(0)
    n_steps = pl.num_programs(0)
    n_experts = ps_ref.shape[0] - 1
    zrows = zbuf.shape[0]
    max_tail = (xs_hbm.shape[0] - 2 * x_ref.shape[0] * n_steps) // zrows

    def fill_copies(do):
        for e in range(n_experts):
            start = ps_ref[e]
            rem = pl_ref[e]
            for r in range(SUBLANES - 1):
                @pl.when(r < (rem & (SUBLANES - 1)))
                def _(start=start, r=r):
                    do(pltpu.make_async_copy(zbuf.at[pl.ds(0, 1)], xs_hbm.at[pl.ds(start + r, 1)], zsem))

            off = start + rem
            for bit in pad_bits:
                off = off - (rem & bit)

                @pl.when((rem & bit) != 0)
                def _(off=off, bit=bit):
                    dst = xs_hbm.at[pl.ds(pl.multiple_of(off, SUBLANES), bit)]
                    do(pltpu.make_async_copy(zbuf.at[pl.ds(0, bit)], dst, zsem))
        for t in range(max_tail):
            @pl.when(t < pl_ref[n_experts])
            def _(t=t):
                dst = xs_hbm.at[pl.ds(pl.multiple_of(ps_ref[n_experts] + t * zrows, SUBLANES), zrows)]
                do(pltpu.make_async_copy(zbuf, dst, zsem))

    @pl.when(i == 0)
    def _():
        zbuf[...] = jnp.zeros_like(zbuf)
        fill_copies(lambda cp: cp.start())

    def scatter_start(sub):
        for r in range(tm):
            for slot in range(2):
                pltpu.make_async_copy(hbuf.at[sub, pl.ds(r, 1)],
                                      xs_hbm.at[pl.ds(pos_ref[0, slot, sub * tm + r], 1)],
                                      sems.at[sub]).start(priority=slot)

    def scatter_wait(sub):
        for _ in range(2):
            pltpu.make_async_copy(hbuf.at[sub], xs_hbm.at[pl.ds(0, tm)], sems.at[sub]).wait()

    for sub in range(2):
        rows = pl.ds(sub * tm, tm)
        hbuf[sub] = x_ref[rows, :] * (1.0 + sc_ref[0]) + sh_ref[0]
        scatter_start(sub)
        if sub == 0:
            @pl.when(i > 0)
            def _():
                scatter_wait(1)
        else:
            scatter_wait(0)

    @pl.when(i == n_steps - 1)
    def _():
        scatter_wait(1)
        fill_copies(lambda cp: cp.wait())


def _dispatch(x, posT, pad_start, pad_len, mods3, mod_row, layer, n_rows, tm, moe_tm):
    m, d = x.shape
    base = layer * MOD_ROWS
    step = 2 * tm
    pad_bits = tuple(1 << b for b in range(moe_tm.bit_length() - 2, SUBLANES.bit_length() - 2, -1))
    grid_spec = pltpu.PrefetchScalarGridSpec(
        num_scalar_prefetch=2,
        grid=(m // step,),
        in_specs=[
            pl.BlockSpec((1, 8, step), lambda i, ps, pn: (i, 0, 0), memory_space=pltpu.SMEM),
            pl.BlockSpec((step, d), lambda i, ps, pn: (i, 0)),
            pl.BlockSpec((1, 1, d), lambda i, ps, pn: (base + mod_row(i, step), 0, 4)),
            pl.BlockSpec((1, 1, d), lambda i, ps, pn: (base + mod_row(i, step), 0, 3)),
        ],
        out_specs=pl.BlockSpec(memory_space=pl.ANY),
        scratch_shapes=[
            pltpu.VMEM((2, tm, d), F32),
            pltpu.VMEM((moe_tm // 2, d), F32),
            pltpu.SemaphoreType.DMA((2,)),
            pltpu.SemaphoreType.DMA(()),
        ],
    )
    return pl.pallas_call(
        functools.partial(_dispatch_kernel, tm, pad_bits),
        grid_spec=grid_spec,
        out_shape=jax.ShapeDtypeStruct((n_rows, d), F32),
        compiler_params=_cparams(("arbitrary",), 2 * step * d * 4 + 3 * tm * d * 4 + (8 << 20)),
    )(pad_start, pad_len, posT, x, mods3, mods3)


def _moe_kernel(layer, te_ref, nu_ref, nxt_ref, par_ref, xs_ref, wg_hbm, wu_hbm, wd_hbm, ys_ref,
                wg_f, wu_f, wd_f, wg_b, wu_b, wd_b, sems):
    i = pl.program_id(0)
    used = i < nu_ref[0]
    expert = te_ref[i]
    first = used & ((i == 0) | (expert != te_ref[jnp.maximum(i - 1, 0)]))

    def weight_copies(e, slot):
        return (pltpu.make_async_copy(wg_hbm.at[layer, e], wg_f.at[slot], sems.at[slot]),
                pltpu.make_async_copy(wu_hbm.at[layer, e], wu_f.at[slot], sems.at[slot]),
                pltpu.make_async_copy(wd_hbm.at[layer, e], wd_f.at[slot], sems.at[slot]))

    @pl.when(jnp.logical_not(used))
    def _():
        ys_ref[...] = jnp.zeros_like(ys_ref)

    @pl.when(i == 0)
    def _():
        for cp in weight_copies(expert, par_ref[0]):
            cp.start(priority=1)

    @pl.when(first)
    def _():
        slot = par_ref[i]
        for cp in weight_copies(expert, slot):
            cp.wait()

        @pl.when(nxt_ref[i] >= 0)
        def _():
            for cp in weight_copies(nxt_ref[i], 1 - slot):
                cp.start(priority=1)

        wg_b[...] = wg_f[slot].astype(BF16)
        wu_b[...] = wu_f[slot].astype(BF16)
        wd_b[...] = wd_f[slot].astype(BF16)

    @pl.when(used)
    def _():
        x = xs_ref[...].astype(BF16)
        g = _dot(x, wg_b[...])
        u = _dot(x, wu_b[...])
        hid = (_silu(g) * u).astype(BF16)
        ys_ref[...] = _dot(hid, wd_b[...])


def _moe_experts(xs, tile_expert, n_used, next_expert, slot, w_gate, w_up, w_down, layer, tm):
    n_rows, d = xs.shape
    f = w_gate.shape[-1]
    n_tiles = n_rows // tm
    grid_spec = pltpu.PrefetchScalarGridSpec(
        num_scalar_prefetch=4,
        grid=(n_tiles,),
        in_specs=[
            pl.BlockSpec((tm, d), lambda i, te, nu, nx, pr: (jnp.maximum(jnp.minimum(i, nu[0] - 1), 0), 0)),
            pl.BlockSpec(memory_space=pl.ANY),
            pl.BlockSpec(memory_space=pl.ANY),
            pl.BlockSpec(memory_space=pl.ANY),
        ],
        out_specs=pl.BlockSpec((tm, d), lambda i, te, nu, nx, pr: (i, 0)),
        scratch_shapes=[
            pltpu.VMEM((2, d, f), F32), pltpu.VMEM((2, d, f), F32), pltpu.VMEM((2, f, d), F32),
            pltpu.VMEM((d, f), BF16), pltpu.VMEM((d, f), BF16), pltpu.VMEM((f, d), BF16),
            pltpu.SemaphoreType.DMA((2,)),
        ],
    )
    return pl.pallas_call(
        functools.partial(_moe_kernel, layer),
        grid_spec=grid_spec,
        out_shape=jax.ShapeDtypeStruct((n_rows, d), F32),
        compiler_params=_cparams(("arbitrary",), 6 * d * f * 4 + 3 * d * f * 2 + 8 * tm * d * 4 + (8 << 20)),
    )(tile_expert, n_used, next_expert, slot, xs, w_gate, w_up, w_down)


def _combine_ln_kernel(alpha, tm, p_steps, pos_ref, nxt_ref, ys_hbm, info_ref, x_ref, gate_ref, g_ref, b_ref,
                       *rest):
    i = pl.program_id(0)
    n_steps = pl.num_programs(0)
    if p_steps is None:
        nsc_ref, nsh_ref, o_ref, hb_ref, buf, sems = rest
    else:
        op_ref, os_ref, buf, sems = rest

    def gather_start(sub, p_ref):
        for r in range(tm):
            for slot in range(2):
                pltpu.make_async_copy(ys_hbm.at[pl.ds(p_ref[0, slot, sub * tm + r], 1)],
                                      buf.at[sub, slot, pl.ds(r, 1)], sems.at[sub]).start(priority=slot)

    def gather_wait(sub):
        for slot in range(2):
            pltpu.make_async_copy(ys_hbm.at[pl.ds(0, tm)], buf.at[sub, slot], sems.at[sub]).wait()

    def finish(sub):
        rows = pl.ds(sub * tm, tm)
        info = info_ref[rows, :]
        y = info[:, 2:3] * buf[sub, 0] + info[:, 3:4] * buf[sub, 1]
        z = alpha * x_ref[rows, :] + gate_ref[0] * y
        res = _layer_norm(z, g_ref[0], b_ref[0])
        if p_steps is None:
            o_ref[rows, :] = res
            hb_ref[rows, :] = (res * (1.0 + nsc_ref[0]) + nsh_ref[0]).astype(BF16)
        else:
            @pl.when(i < p_steps)
            def _():
                op_ref[rows, :] = res

            @pl.when(i >= p_steps)
            def _():
                os_ref[rows, :] = res

    @pl.when(i == 0)
    def _():
        gather_start(0, pos_ref)

    gather_start(1, pos_ref)
    gather_wait(0)
    finish(0)

    @pl.when(i + 1 < n_steps)
    def _():
        gather_start(0, nxt_ref)

    gather_wait(1)
    finish(1)


def _combine_ln(ys, posT, info, x, mods3, mod_row, layer, ln_g, ln_b, alpha, tm, last, p_rows):
    m, d = x.shape
    step = 2 * tm
    n_steps = m // step
    base = layer * MOD_ROWS
    in_specs = [
        pl.BlockSpec((1, 8, step), lambda i: (i, 0, 0), memory_space=pltpu.SMEM),
        pl.BlockSpec((1, 8, step), lambda i: (jnp.minimum(i + 1, n_steps - 1), 0, 0), memory_space=pltpu.SMEM),
        pl.BlockSpec(memory_space=pl.ANY),
        pl.BlockSpec((step, V7X_LANES), lambda i: (i, 0)),
        pl.BlockSpec((step, d), lambda i: (i, 0)),
        pl.BlockSpec((1, 1, d), lambda i: (base + mod_row(i, step), 0, 5)),
        pl.BlockSpec((1, 1, d), lambda i: (layer, 0, 0)),
        pl.BlockSpec((1, 1, d), lambda i: (layer, 0, 0)),
    ]
    args = [posT, posT, ys, info, x, mods3, ln_g, ln_b]
    if last:
        p_steps = p_rows // step
        out_specs = [
            pl.BlockSpec((step, d), lambda i: (jnp.minimum(i, p_steps - 1), 0)),
            pl.BlockSpec((step, d), lambda i: (jnp.maximum(i - p_steps, 0), 0)),
        ]
        out_shape = [jax.ShapeDtypeStruct((p_rows, d), F32), jax.ShapeDtypeStruct((m - p_rows, d), F32)]
    else:
        p_steps = None
        nbase = (layer + 1) * MOD_ROWS
        in_specs += [
            pl.BlockSpec((1, 1, d), lambda i: (nbase + mod_row(i, step), 0, 1)),
            pl.BlockSpec((1, 1, d), lambda i: (nbase + mod_row(i, step), 0, 0)),
        ]
        args += [mods3, mods3]
        out_specs = [pl.BlockSpec((step, d), lambda i: (i, 0)), pl.BlockSpec((step, d), lambda i: (i, 0))]
        out_shape = [jax.ShapeDtypeStruct((m, d), F32), jax.ShapeDtypeStruct((m, d), BF16)]
    return pl.pallas_call(
        functools.partial(_combine_ln_kernel, alpha, tm, p_steps),
        grid=(n_steps,),
        in_specs=in_specs,
        out_specs=out_specs,
        out_shape=out_shape,
        scratch_shapes=[pltpu.VMEM((2, 2, tm, d), F32), pltpu.SemaphoreType.DMA((2,))],
        compiler_params=_cparams(("arbitrary",), 6 * step * d * 4 + 4 * tm * d * 4 + 6 * tm * d * 4 + (8 << 20)),
    )(*args)


def _router_params(layer, w_rg, b_rg, w_re, b_re):
    d = w_rg.shape[1]
    n_e = N_EXPERTS
    pad = V7X_LANES - N_GROUPS - n_e
    wr = jnp.concatenate(
        [w_rg[layer], w_re[layer].transpose(1, 0, 2).reshape(d, n_e), jnp.zeros((d, pad), F32)], axis=1)
    br = jnp.concatenate([b_rg[layer], b_re[layer].reshape(n_e), jnp.zeros((pad,), F32)])[None, :]
    return wr, br


def _hier_moe_ln(x, info, cnt, mods3, mod_row, layer, w_gate, w_up, w_down, ln_g, ln_b, alpha, last, p_rows):
    m, d = x.shape
    n_e = N_EXPERTS

    tm = MOE_TM
    n_tiles = 2 * m // tm + n_e
    counts = cnt[0, :n_e].astype(jnp.int32)
    tiles_e = (counts + tm - 1) // tm
    tile_end = jnp.cumsum(tiles_e)
    n_used = tile_end[-1]
    row_off = (tile_end - tiles_e) * tm
    tile_ids = jnp.minimum(jnp.arange(n_tiles, dtype=jnp.int32), n_used - 1)
    tile_expert = jnp.sum((tile_ids[:, None] >= tile_end[None, :]).astype(jnp.int32), axis=1)
    group_end = tile_end[tile_expert]
    next_expert = jnp.where(group_end < n_used, tile_expert[jnp.minimum(group_end, n_tiles - 1)], -1)
    is_first = jnp.concatenate([jnp.ones((1,), jnp.int32), (tile_expert[1:] != tile_expert[:-1]).astype(jnp.int32)])
    slot = (jnp.cumsum(is_first) - 1) % 2
    zrows = tm // 2
    pad_start = jnp.concatenate([row_off + counts, (n_used * tm)[None]])
    pad_len = jnp.concatenate([tiles_e * tm - counts, ((n_tiles - n_used) * (tm // zrows))[None]])
    row_off_f = jnp.concatenate([row_off.astype(F32), jnp.zeros((V7X_LANES - n_e,), F32)])[None, :]

    ct = COMB_TM
    posT = _positions(info, row_off_f, 2 * ct)
    xs = _dispatch(x, posT, pad_start, pad_len, mods3, mod_row, layer, n_tiles * tm, ct, tm)
    ys = _moe_experts(xs, tile_expert, n_used.reshape(1), next_expert.astype(jnp.int32), slot.astype(jnp.int32),
                      w_gate, w_up, w_down, layer, tm)
    return _combine_ln(ys, posT, info, x, mods3, mod_row, layer, ln_g, ln_b, alpha, ct, last, p_rows)


def kernel(x_prompt, x_sample, state_ret, cache_na_k, cache_na_v, c, c_ctx, w_mod, b_mod, ln1_g, ln1_b, ln2_g, ln2_b, w_ret_in, ret_decay_logit, ret_gn_g, ret_gn_b, w_ret_out, w_na_in, na_rpb, w_na_out, w_rg, b_rg, w_re, b_re, w_gate, w_up, w_down):
    bp, seq, d = x_prompt.shape
    bs, dec_seq, _ = x_sample.shape
    depth = w_mod.shape[0]
    slab = dec_seq
    assert slab % seq == 0 and (bp * seq) % slab == 0 and bs + 1 <= MOD_ROWS
    p_rows = bp * seq
    p_slabs = p_rows // slab
    alpha = (2.0 * depth) ** 0.25

    def mod_row(i, tm):
        start = i * tm
        return jnp.where(start < p_rows, 0, 1 + (start - p_rows) // dec_seq)

    cc = jnp.concatenate([c_ctx[None, :], c, jnp.zeros((MOD_ROWS - 1 - bs, d), F32)], axis=0)
    mods = _modulation(cc, w_mod, b_mod)
    mods3 = mods.reshape(depth * MOD_ROWS, 1, 6 * d)
    x, hb = _prologue(x_prompt.reshape(p_rows, d), x_sample.reshape(bs * dec_seq, d), mods3, mod_row)

    ln1_g3, ln1_b3 = ln1_g.reshape(depth, 1, d), ln1_b.reshape(depth, 1, d)
    ln2_g3, ln2_b3 = ln2_g.reshape(depth, 1, d), ln2_b.reshape(depth, 1, d)

    new_ret, new_k, new_v = [], [], []
    for l in range(depth):
        j = l // 2
        if l % 2 == 0:
            dk = state_ret.shape[-2]
            proj = _inproj(hb, w_ret_in, j)
            log_g = -jax.nn.softplus(-ret_decay_logit[j].astype(F32))
            t = jnp.arange(slab)
            nf = dk // 4
            inv_freq = ROPE_BASE ** (-jnp.arange(nf, dtype=F32) / nf)
            ang_r = (t // GRID_W).astype(F32)[:, None] * inv_freq
            ang_c = (t % GRID_W).astype(F32)[:, None] * inv_freq
            cos = jnp.concatenate([jnp.cos(ang_r)] * 2 + [jnp.cos(ang_c)] * 2, axis=1)
            sin = jnp.concatenate([-jnp.sin(ang_r), jnp.sin(ang_r), -jnp.sin(ang_c), jnp.sin(ang_c)], axis=1)
            y, st = _retention(proj, log_g, cos, sin, state_ret, j, ret_gn_g[j][:, None, :],
                               ret_gn_b[j][:, None, :], p_slabs, slab)
            new_ret.append(st)
            mixed, w_out = y, w_ret_out[j]
        else:
            proj = _inproj(hb, w_na_in, j)
            o, nk, nv = _attention(proj, cache_na_k, cache_na_v, j, _na_strips(na_rpb[j]), p_slabs, slab, seq)
            new_k.append(nk[:, 0])
            new_v.append(nv[:, 0])
            mixed, w_out = o, w_na_out[j]
        wr, br = _router_params(l, w_rg, b_rg, w_re, b_re)
        x, info, cnt = _outproj_ln(mixed, w_out.astype(BF16), x, mods3, mod_row, l, ln1_g3, ln1_b3, wr, br, alpha)
        x, hb = _hier_moe_ln(x, info, cnt, mods3, mod_row, l, w_gate, w_up, w_down, ln2_g3, ln2_b3, alpha,
                             l == depth - 1, p_rows)

    y_prompt = x.reshape(bp, seq, d)
    y_sample = hb.reshape(bs, dec_seq, d)
    return (y_prompt, y_sample, jnp.stack(new_ret, axis=1), jnp.stack(new_k, axis=1), jnp.stack(new_v, axis=1))
```

```python
import functools

import jax
import jax.numpy as jnp
from jax import lax
from jax.experimental import pallas as pl
from jax.experimental.pallas import tpu as pltpu

F32 = jnp.float32
BF16 = jnp.bfloat16

GRID_W = 64
WIN_H = 8
WIN_W = 16
N_GROUPS = 4
N_EXP_PER_GROUP = 8
N_EXPERTS = N_GROUPS * N_EXP_PER_GROUP
ROPE_BASE = 10000.0
LN_EPS = 1e-5
NEG_INF = -1e30

V7X_VMEM_BYTES = 64 * 1024 * 1024
V7X_LANES = 128
SUBLANES = 8
MOD_ROWS = 8

RET_CHUNK = 256
ATTN_QB = 256
ATTN_HEADS_PER_STEP = 2
MOE_TM = 256
COMB_TM = 256


def _cparams(sem, vmem_bytes):
    limit = int(min(max(vmem_bytes, 16 * 1024 * 1024), V7X_VMEM_BYTES - 6 * 1024 * 1024))
    return pltpu.CompilerParams(dimension_semantics=sem, vmem_limit_bytes=limit)


def _silu(x):
    return x / (1.0 + jnp.exp(-x))


U32 = jnp.uint32


def _pack_halves(x):
    n = x.shape[1] // 2
    return pltpu.pack_elementwise([x[:, :n], x[:, n:]], packed_dtype=BF16)


def _unpack_halves(p):
    return (pltpu.unpack_elementwise(p, index=0, packed_dtype=BF16, unpacked_dtype=F32),
            pltpu.unpack_elementwise(p, index=1, packed_dtype=BF16, unpacked_dtype=F32))


def _dot(a, b):
    return jnp.dot(a, b, preferred_element_type=F32)


def _dot_nt(a, b):
    return lax.dot_general(a, b, (((1,), (1,)), ((), ())), preferred_element_type=F32)


def _layer_norm(z, g, b):
    mu = jnp.mean(z, axis=-1, keepdims=True)
    zc = z - mu
    var = jnp.mean(zc * zc, axis=-1, keepdims=True)
    return zc * lax.rsqrt(var + LN_EPS) * g + b


def _mod_kernel(c_ref, w_ref, b_ref, o_ref):
    s = _silu(c_ref[...]).astype(BF16)
    o_ref[0] = _dot(s, w_ref[0].astype(BF16)) + b_ref[0]


def _modulation(cc, w_mod, b_mod):
    depth, d, n6 = w_mod.shape
    tn = 1024
    return pl.pallas_call(
        _mod_kernel,
        grid=(depth, n6 // tn),
        in_specs=[
            pl.BlockSpec((MOD_ROWS, d), lambda l, j: (0, 0)),
            pl.BlockSpec((1, d, tn), lambda l, j: (l, 0, j)),
            pl.BlockSpec((1, 1, tn), lambda l, j: (l, 0, j)),
        ],
        out_specs=pl.BlockSpec((1, MOD_ROWS, tn), lambda l, j: (l, 0, j)),
        out_shape=jax.ShapeDtypeStruct((depth, MOD_ROWS, n6), F32),
        compiler_params=_cparams(("arbitrary", "arbitrary"), 3 * d * tn * 4 + (4 << 20)),
    )(cc, w_mod, b_mod.reshape(depth, 1, n6))


def _prologue_kernel(p_tiles, xp_ref, xs_ref, sc_ref, sh_ref, x_ref, hb_ref):
    @pl.when(pl.program_id(0) < p_tiles)
    def _():
        x = xp_ref[...]
        x_ref[...] = x
        hb_ref[...] = (x * (1.0 + sc_ref[0]) + sh_ref[0]).astype(BF16)

    @pl.when(pl.program_id(0) >= p_tiles)
    def _():
        x = xs_ref[...]
        x_ref[...] = x
        hb_ref[...] = (x * (1.0 + sc_ref[0]) + sh_ref[0]).astype(BF16)


def _prologue(xp, xs, mods3, mod_row, tm=512):
    p_rows, d = xp.shape
    m = p_rows + xs.shape[0]
    p_tiles = p_rows // tm
    return pl.pallas_call(
        functools.partial(_prologue_kernel, p_tiles),
        grid=(m // tm,),
        in_specs=[
            pl.BlockSpec((tm, d), lambda i: (jnp.minimum(i, p_tiles - 1), 0)),
            pl.BlockSpec((tm, d), lambda i: (jnp.maximum(i - p_tiles, 0), 0)),
            pl.BlockSpec((1, 1, d), lambda i: (mod_row(i, tm), 0, 1)),
            pl.BlockSpec((1, 1, d), lambda i: (mod_row(i, tm), 0, 0)),
        ],
        out_specs=[pl.BlockSpec((tm, d), lambda i: (i, 0)), pl.BlockSpec((tm, d), lambda i: (i, 0))],
        out_shape=[jax.ShapeDtypeStruct((m, d), F32), jax.ShapeDtypeStruct((m, d), BF16)],
        compiler_params=_cparams(("arbitrary",), 8 * tm * d * 4 + (8 << 20)),
    )(xp, xs, mods3, mods3)


def _inproj_kernel(h_ref, w_ref, o_ref):
    o_ref[...] = _dot(h_ref[...], w_ref[...].astype(BF16))


def _inproj(hb, w_all, w_idx, tm=2048, tn=512):
    m, d = hb.shape
    n = w_all.shape[-1]
    return pl.pallas_call(
        _inproj_kernel,
        grid=(m // tm, n // tn),
        in_specs=[
            pl.BlockSpec((tm, d), lambda i, j: (i, 0)),
            pl.BlockSpec((None, d, tn), lambda i, j: (w_idx, 0, j)),
        ],
        out_specs=pl.BlockSpec((tm, tn), lambda i, j: (i, j)),
        out_shape=jax.ShapeDtypeStruct((m, n), F32),
        compiler_params=_cparams(
            ("arbitrary", "arbitrary"),
            2 * tm * d * 2 + 2 * d * tn * 4 + d * tn * 2 + 3 * tm * tn * 4 + (4 << 20)),
    )(hb, w_all)


def _outproj_ln_kernel(alpha, y_ref, w_ref, x_ref, gate_ref, g_ref, b_ref, sc_ref, sh_ref, wr_ref, br_ref,
                       o_ref, info_ref, cnt_ref, run_ref):
    @pl.when(pl.program_id(0) == 0)
    def _():
        run_ref[...] = jnp.zeros_like(run_ref)

    z = alpha * x_ref[...] + gate_ref[0] * _dot(y_ref[...], w_ref[...])
    x1 = _layer_norm(z, g_ref[0], b_ref[0])
    o_ref[...] = x1
    info, run = _route(x1 * (1.0 + sc_ref[0]) + sh_ref[0], wr_ref[...], br_ref[...], run_ref[...])
    info_ref[...] = info
    run_ref[...] = run
    cnt_ref[...] = run


def _outproj_ln(y, w, x, mods3, mod_row, layer, ln_g, ln_b, wr, br, alpha, tm=512):
    m, kdim = y.shape
    d = x.shape[1]
    base = layer * MOD_ROWS
    return pl.pallas_call(
        functools.partial(_outproj_ln_kernel, alpha),
        grid=(m // tm,),
        in_specs=[
            pl.BlockSpec((tm, kdim), lambda i: (i, 0)),
            pl.BlockSpec((kdim, d), lambda i: (0, 0)),
            pl.BlockSpec((tm, d), lambda i: (i, 0)),
            pl.BlockSpec((1, 1, d), lambda i: (base + mod_row(i, tm), 0, 2)),
            pl.BlockSpec((1, 1, d), lambda i: (layer, 0, 0)),
            pl.BlockSpec((1, 1, d), lambda i: (layer, 0, 0)),
            pl.BlockSpec((1, 1, d), lambda i: (base + mod_row(i, tm), 0, 4)),
            pl.BlockSpec((1, 1, d), lambda i: (base + mod_row(i, tm), 0, 3)),
            pl.BlockSpec((d, V7X_LANES), lambda i: (0, 0)),
            pl.BlockSpec((1, V7X_LANES), lambda i: (0, 0)),
        ],
        out_specs=[
            pl.BlockSpec((tm, d), lambda i: (i, 0)),
            pl.BlockSpec((tm, V7X_LANES), lambda i: (i, 0)),
            pl.BlockSpec((1, V7X_LANES), lambda i: (0, 0)),
        ],
        out_shape=[
            jax.ShapeDtypeStruct((m, d), F32),
            jax.ShapeDtypeStruct((m, V7X_LANES), F32),
            jax.ShapeDtypeStruct((1, V7X_LANES), F32),
        ],
        scratch_shapes=[pltpu.VMEM((1, V7X_LANES), F32)],
        compiler_params=_cparams(
            ("arbitrary",), kdim * d * 2 + 2 * tm * kdim * 2 + 9 * tm * d * 4 + d * V7X_LANES * 4 + (4 << 20)),
    )(y, w, x, mods3, ln_g, ln_b, mods3, mods3, wr, br)


def _rope(x, cos, sin):
    half = V7X_LANES // 2
    sw = jnp.concatenate(
        [pltpu.roll(x[:, :V7X_LANES], half, 1), pltpu.roll(x[:, V7X_LANES:], half, 1)], axis=1)
    return x * cos + sw * sin


def _group_norm(o, g, b):
    mu = jnp.mean(o, axis=-1, keepdims=True)
    oc = o - mu
    var = jnp.mean(oc * oc, axis=-1, keepdims=True)
    return oc * lax.rsqrt(var + LN_EPS) * g + b


def _ret_kernel(p_slabs, dk, lg_ref, q_ref, k_ref, v_ref, gf_ref, gb_ref, cos_ref, sin_ref, s0_ref,
                gng_ref, gnb_ref, y_ref, st_ref, q_s, k_s, yacc):
    h = pl.program_id(0)
    s = pl.program_id(1)
    is_prompt = s < p_slabs
    c = RET_CHUNK
    n_chunks = q_ref.shape[0] // c

    def run(prompt):
        ri = lax.broadcasted_iota(jnp.int32, (c, c), 0)
        ci = lax.broadcasted_iota(jnp.int32, (c, c), 1)
        diff = (ri - ci).astype(F32)
        pos = lax.broadcasted_iota(jnp.int32, (c, 1), 0).astype(F32)
        gn_g = gng_ref[0]
        gn_b = gnb_ref[0]
        scale = dk ** -0.5
        if not prompt:
            q_s[...] = _rope(q_ref[...], cos_ref[...], sin_ref[...])
            k_s[...] = _rope(k_ref[...], cos_ref[...], sin_ref[...]) * scale

        for direction in (0, 1):
            lg = jnp.full((1, 1), lg_ref[direction, h], F32)
            if direction == 0:
                mask = jnp.where(diff >= 0, jnp.exp(jnp.maximum(diff, 0.0) * lg), 0.0)
                q_dec = jnp.exp((pos + 1.0) * lg)
                k_dec = jnp.exp((c - 1.0 - pos) * lg)
                order = range(n_chunks)
                gate_ref = gf_ref
            else:
                mask = jnp.where(diff <= 0, jnp.exp(jnp.maximum(-diff, 0.0) * lg), 0.0)
                q_dec = jnp.exp((c - pos) * lg)
                k_dec = jnp.exp(pos * lg)
                order = range(n_chunks - 1, -1, -1)
                gate_ref = gb_ref
            c_dec = jnp.exp(c * lg)
            state = None if prompt else s0_ref[0, direction, 0]
            for ch in order:
                rows = pl.ds(ch * c, c)
                if prompt:
                    qc = q_ref[rows, :]
                    kc = k_ref[rows, :] * scale
                else:
                    qc = q_s[rows, :]
                    kc = k_s[rows, :]
                vb = v_ref[rows, :].astype(BF16)
                inner = _dot_nt(qc.astype(BF16), kc.astype(BF16)) * mask
                o = _dot(inner.astype(BF16), vb)
                kv = _dot(jnp.transpose(kc * k_dec).astype(BF16), vb)
                if prompt:
                    st_ref[ch, direction, 0] = kv
                else:
                    o = o + _dot((qc * q_dec).astype(BF16), state.astype(BF16))
                    state = c_dec * state + kv
                contrib = _group_norm(o, gn_g, gn_b) * _silu(gate_ref[rows, :])
                if direction == 0:
                    yacc[rows, :] = contrib
                else:
                    y_ref[rows, :] = (yacc[rows, :] + contrib).astype(BF16)

    @pl.when(is_prompt)
    def _():
        run(True)

    @pl.when(jnp.logical_not(is_prompt))
    def _():
        run(False)


def _retention(proj, log_g, rope_cos, rope_sin, state_ret, ret_idx, gn_g, gn_b, p_slabs, slab):
    m = proj.shape[0]
    heads, dk, dv = state_ret.shape[-3:]
    n_slabs = m // slab
    per_slab = slab // RET_CHUNK
    v_blk0 = 2 * heads * dk // dv
    p_last = p_slabs - 1

    def s0_map(h, s):
        return (jnp.maximum(s - p_slabs, 0), ret_idx, 0, h, 0, 0)

    kernel = functools.partial(_ret_kernel, p_slabs, dk)
    return pl.pallas_call(
        kernel,
        grid=(heads, n_slabs),
        in_specs=[
            pl.BlockSpec(memory_space=pltpu.SMEM),
            pl.BlockSpec((slab, dk), lambda h, s: (s, h)),
            pl.BlockSpec((slab, dk), lambda h, s: (s, heads + h)),
            pl.BlockSpec((slab, dv), lambda h, s: (s, v_blk0 + h)),
            pl.BlockSpec((slab, dv), lambda h, s: (s, v_blk0 + heads + h)),
            pl.BlockSpec((slab, dv), lambda h, s: (s, v_blk0 + 2 * heads + h)),
            pl.BlockSpec((slab, dk), lambda h, s: (0, 0)),
            pl.BlockSpec((slab, dk), lambda h, s: (0, 0)),
            pl.BlockSpec((1, None, 2, 1, dk, dv), s0_map),
            pl.BlockSpec((1, 1, dv), lambda h, s: (h, 0, 0)),
            pl.BlockSpec((1, 1, dv), lambda h, s: (h, 0, 0)),
        ],
        out_specs=[
            pl.BlockSpec((slab, dv), lambda h, s: (s, h)),
            pl.BlockSpec((per_slab, 2, 1, dk, dv), lambda h, s: (jnp.minimum(s, p_last), 0, h, 0, 0)),
        ],
        out_shape=[
            jax.ShapeDtypeStruct((m, heads * dv), BF16),
            jax.ShapeDtypeStruct((p_slabs * per_slab, 2, heads, dk, dv), F32),
        ],
        scratch_shapes=[
            pltpu.VMEM((slab, dk), F32),
            pltpu.VMEM((slab, dk), F32),
            pltpu.VMEM((slab, dv), F32),
        ],
        compiler_params=_cparams(
            ("arbitrary", "arbitrary"),
            2 * (2 * slab * dk + 3 * slab * dv) * 4 + 4 * slab * dk * 4 + 2 * 2 * dk * dv * 4
            + 2 * slab * dv * 2 + 2 * per_slab * 2 * dk * dv * 4 + 2 * slab * dk * 4 + slab * dv * 4
            + (8 << 20)),
    )(log_g, proj, proj, proj, proj, proj, rope_cos, rope_sin, state_ret, gn_g, gn_b)


def _na_window(rq, rows):
    kh = min(WIN_H, rows)
    return min(max(rq - kh // 2, 0), rows - kh), kh


def _attn_kernel(p_slabs, seq, dh, q_ref, k_ref, v_ref, ck_ref, cv_ref, strip_ref, o_ref, nk_ref, nv_ref):
    s = pl.program_id(1)
    is_prompt = s < p_slabs
    slab = q_ref.shape[0]
    scale = dh ** -0.5

    def prompt_head(hh):
        cols = pl.ds(hh * dh, dh)
        for b in range(slab // seq):
            rows = pl.ds(b * seq, seq)
            k = k_ref[rows, cols]
            v = v_ref[rows, cols]
            nk_ref[b, 0, hh] = k
            nv_ref[b, 0, hh] = v
            q = (q_ref[rows, cols] * scale).astype(BF16)
            sc = _dot_nt(q, k.astype(BF16))
            p = jnp.exp(sc - jnp.max(sc, axis=-1, keepdims=True))
            denom = jnp.sum(p, axis=-1, keepdims=True)
            o_ref[rows, cols] = (_dot(p.astype(BF16), v.astype(BF16)) / denom).astype(BF16)

    def latent_head(hh):
        cols = pl.ds(hh * dh, dh)
        grid_rows = slab // GRID_W
        rows_per_block = ATTN_QB // GRID_W
        ckb = ck_ref[0, 0, hh].astype(BF16)
        cvb = cv_ref[0, 0, hh].astype(BF16)
        for qb in range(slab // ATTN_QB):
            rq0 = qb * rows_per_block
            ka = _na_window(rq0, grid_rows)[0] // 2 * 2
            last0, kh = _na_window(rq0 + rows_per_block - 1, grid_rows)
            kb = -((last0 + kh) // -2) * 2
            n_keys = (kb - ka) * GRID_W
            keys = pl.ds(ka * GRID_W, n_keys)
            pieces = []
            for i in range(rows_per_block):
                rq = rq0 + i
                r0, kh = _na_window(rq, grid_rows)
                first = ka - rq + WIN_H - 1
                piece = strip_ref[hh, first % 2, :, pl.ds((first - first % 2) * GRID_W, n_keys)]
                if r0 != ka or r0 + kh != kb:
                    key_row = ka + (lax.broadcasted_iota(jnp.int32, piece.shape, 1) // GRID_W)
                    piece = jnp.where((key_row >= r0) & (key_row < r0 + kh), piece, NEG_INF)
                pieces.append(piece)
            bias = jnp.concatenate(pieces, axis=0)
            rows = pl.ds(qb * ATTN_QB, ATTN_QB)
            q = (q_ref[rows, cols] * scale).astype(BF16)
            s_loc = _dot_nt(q, k_ref[keys, cols].astype(BF16)) + bias
            s_ctx = _dot_nt(q, ckb)
            mx = jnp.maximum(jnp.max(s_loc, axis=-1, keepdims=True), jnp.max(s_ctx, axis=-1, keepdims=True))
            p_loc = jnp.exp(s_loc - mx)
            p_ctx = jnp.exp(s_ctx - mx)
            denom = jnp.sum(p_loc, axis=-1, keepdims=True) + jnp.sum(p_ctx, axis=-1, keepdims=True)
            o = _dot(p_loc.astype(BF16), v_ref[keys, cols].astype(BF16)) + _dot(p_ctx.astype(BF16), cvb)
            o_ref[rows, cols] = (o / denom).astype(BF16)

    heads_per_step = q_ref.shape[1] // dh

    @pl.when(is_prompt)
    def _():
        for hh in range(heads_per_step):
            prompt_head(hh)

    @pl.when(jnp.logical_not(is_prompt))
    def _():
        for hh in range(heads_per_step):
            latent_head(hh)


def _attention(proj, cache_k, cache_v, cache_idx, strips, p_slabs, slab, seq):
    m = proj.shape[0]
    heads = cache_k.shape[2]
    dh = cache_k.shape[-1]
    past = cache_k.shape[-2]
    n_slabs = m // slab
    per_slab = slab // seq
    p_last = p_slabs - 1

    hp = ATTN_HEADS_PER_STEP
    groups = heads // hp

    def ctx_map(h, s):
        return (jnp.maximum(s - p_slabs, 0), cache_idx, h, 0, 0)

    def new_map(h, s):
        return (jnp.minimum(s, p_last), 0, h, 0, 0)

    kernel = functools.partial(_attn_kernel, p_slabs, seq, dh)
    return pl.pallas_call(
        kernel,
        grid=(groups, n_slabs),
        in_specs=[
            pl.BlockSpec((slab, hp * dh), lambda h, s: (s, h)),
            pl.BlockSpec((slab, hp * dh), lambda h, s: (s, groups + h)),
            pl.BlockSpec((slab, hp * dh), lambda h, s: (s, 2 * groups + h)),
            pl.BlockSpec((1, 1, hp, past, dh), ctx_map),
            pl.BlockSpec((1, 1, hp, past, dh), ctx_map),
            pl.BlockSpec((hp,) + strips.shape[1:], lambda h, s: (h, 0, 0, 0)),
        ],
        out_specs=[
            pl.BlockSpec((slab, hp * dh), lambda h, s: (s, h)),
            pl.BlockSpec((per_slab, 1, hp, seq, dh), new_map),
            pl.BlockSpec((per_slab, 1, hp, seq, dh), new_map),
        ],
        out_shape=[
            jax.ShapeDtypeStruct((m, heads * dh), BF16),
            jax.ShapeDtypeStruct((p_slabs * per_slab, 1, heads, seq, dh), F32),
            jax.ShapeDtypeStruct((p_slabs * per_slab, 1, heads, seq, dh), F32),
        ],
        compiler_params=_cparams(
            ("arbitrary", "arbitrary"),
            hp * (2 * strips.shape[1] * strips.shape[2] * strips.shape[3] * 4 + 16 * slab * dh * 4)
            + 10 * ATTN_QB * (slab + past) * 4 + (8 << 20)),
    )(proj, proj, proj, cache_k, cache_v, strips)


def _na_strips(rpb):
    heads, n_dr, n_dc = rpb.shape
    col = jnp.arange(GRID_W)
    c0 = jnp.clip(col - WIN_W // 2, 0, GRID_W - WIN_W)
    col_ok = (col[None, :] >= c0[:, None]) & (col[None, :] < c0[:, None] + WIN_W)
    dc_idx = jnp.clip(col[None, :] - col[:, None], -(WIN_W - 1), WIN_W - 1) + WIN_W - 1
    onehot = (dc_idx[:, :, None] == jnp.arange(n_dc)[None, None, :]).astype(F32)
    tiles = jnp.einsum("hdk,qck->hqdc", rpb.astype(F32), onehot, precision=lax.Precision.HIGHEST)
    tiles = jnp.where(col_ok[None, :, None, :], tiles, NEG_INF)
    n_tiles = 2 * WIN_H
    neg = jnp.full((heads, GRID_W, n_tiles + 1 - n_dr, GRID_W), NEG_INF, F32)
    tiles = jnp.concatenate([tiles, neg], axis=2)
    both = jnp.stack([tiles[:, :, :n_tiles], tiles[:, :, 1:]], axis=1)
    return both.reshape(heads, 2, GRID_W, n_tiles * GRID_W)


def _route(hm, wr, br, run):
    h_hi = hm.astype(BF16)
    h_lo = (hm - h_hi.astype(F32)).astype(BF16)
    w_hi = wr.astype(BF16)
    w_lo = (wr - w_hi.astype(F32)).astype(BF16)
    logits = _dot(h_hi, w_hi) + (_dot(h_hi, w_lo) + _dot(h_lo, w_hi)) + br
    tm, width = logits.shape
    col = lax.broadcasted_iota(jnp.int32, (tm, width), 1).astype(F32)
    neg = jnp.float32(-3.0e38)

    gl = jnp.where(col < N_GROUPS, logits, neg)
    gmax = jnp.max(gl, axis=-1, keepdims=True)
    gsel = jnp.min(jnp.where(gl == gmax, col, float(width)), axis=-1, keepdims=True)
    p_g = 1.0 / jnp.sum(jnp.where(col < N_GROUPS, jnp.exp(gl - gmax), 0.0), axis=-1, keepdims=True)

    lo = N_GROUPS + N_EXP_PER_GROUP * gsel
    el = jnp.where((col >= lo) & (col < lo + N_EXP_PER_GROUP), logits, neg)
    e1 = jnp.max(el, axis=-1, keepdims=True)
    i1 = jnp.min(jnp.where(el == e1, col, float(width)), axis=-1, keepdims=True)
    el2 = jnp.where(col == i1, neg, el)
    e2 = jnp.max(el2, axis=-1, keepdims=True)
    i2 = jnp.min(jnp.where(el2 == e2, col, float(width)), axis=-1, keepdims=True)
    t = jnp.exp(e2 - e1)
    w1 = 1.0 / (1.0 + t)
    c1 = p_g * w1
    c2 = p_g * (t * w1)
    id1 = i1 - N_GROUPS
    id2 = i2 - N_GROUPS

    onehot = jnp.where((col == id1) | (col == id2), 1.0, 0.0)
    ri = lax.broadcasted_iota(jnp.int32, (tm, tm), 0)
    ci = lax.broadcasted_iota(jnp.int32, (tm, tm), 1)
    tri = jnp.where(ri > ci, 1.0, 0.0).astype(BF16)
    before = _dot(tri, onehot.astype(BF16)) + run
    r1 = jnp.sum(jnp.where(col == id1, before, 0.0), axis=-1, keepdims=True)
    r2 = jnp.sum(jnp.where(col == id2, before, 0.0), axis=-1, keepdims=True)

    info = jnp.where(col == 0, id1, 0.0)
    info = jnp.where(col == 1, id2, info)
    info = jnp.where(col == 2, c1, info)
    info = jnp.where(col == 3, c2, info)
    info = jnp.where(col == 4, r1, info)
    info = jnp.where(col == 5, r2, info)
    return info, run + jnp.sum(onehot, axis=0, keepdims=True)


def _pos_kernel(info_ref, off_ref, pos_ref):
    info = info_ref[...]
    tm, width = info.shape
    col = lax.broadcasted_iota(jnp.int32, (tm, width), 1).astype(F32)
    off = off_ref[...]
    p1 = jnp.sum(jnp.where(col == info[:, 0:1], off, 0.0), axis=-1, keepdims=True) + info[:, 4:5]
    p2 = jnp.sum(jnp.where(col == info[:, 1:2], off, 0.0), axis=-1, keepdims=True) + info[:, 5:6]
    pm = jnp.where(col == 0, p1, jnp.where(col == 1, p2, 0.0))
    pos_ref[0] = jnp.transpose(pm)[:8, :].astype(jnp.int32)


def _positions(info, row_off, tm):
    m = info.shape[0]
    return pl.pallas_call(
        _pos_kernel,
        grid=(m // tm,),
        in_specs=[
            pl.BlockSpec((tm, V7X_LANES), lambda i: (i, 0)),
            pl.BlockSpec((1, V7X_LANES), lambda i: (0, 0)),
        ],
        out_specs=pl.BlockSpec((1, 8, tm), lambda i: (i, 0, 0)),
        out_shape=jax.ShapeDtypeStruct((m // tm, 8, tm), jnp.int32),
        compiler_params=_cparams(("arbitrary",), 16 << 20),
    )(info, row_off)


def _dispatch_kernel(tm, pad_bits, ps_ref, pl_ref, pos_ref, x_ref, sc_ref, sh_ref, xs_hbm,
                     hbuf, zbuf, sems, zsem):
    i = pl.program_id(0)
    n_steps = pl.num_programs(0)
    n_experts = ps_ref.shape[0] - 1
    zrows = zbuf.shape[0]
    max_tail = (xs_hbm.shape[0] - 2 * x_ref.shape[0] * n_steps) // zrows

    def fill_copies(do):
        for e in range(n_experts):
            start = ps_ref[e]
            rem = pl_ref[e]
            for r in range(SUBLANES - 1):
                @pl.when(r < (rem & (SUBLANES - 1)))
                def _(start=start, r=r):
                    do(pltpu.make_async_copy(zbuf.at[pl.ds(0, 1)], xs_hbm.at[pl.ds(start + r, 1)], zsem))

            off = start + rem
            for bit in pad_bits:
                off = off - (rem & bit)

                @pl.when((rem & bit) != 0)
                def _(off=off, bit=bit):
                    dst = xs_hbm.at[pl.ds(pl.multiple_of(off, SUBLANES), bit)]
                    do(pltpu.make_async_copy(zbuf.at[pl.ds(0, bit)], dst, zsem))
        for t in range(max_tail):
            @pl.when(t < pl_ref[n_experts])
            def _(t=t):
                dst = xs_hbm.at[pl.ds(pl.multiple_of(ps_ref[n_experts] + t * zrows, SUBLANES), zrows)]
                do(pltpu.make_async_copy(zbuf, dst, zsem))

    @pl.when(i == 0)
    def _():
        zbuf[...] = jnp.zeros_like(zbuf)
        fill_copies(lambda cp: cp.start())

    def scatter_start(sub):
        for r in range(tm):
            for slot in range(2):
                pltpu.make_async_copy(hbuf.at[sub, pl.ds(r, 1)],
                                      xs_hbm.at[pl.ds(pos_ref[0, slot, sub * tm + r], 1)],
                                      sems.at[sub]).start(priority=slot)

    def scatter_wait(sub):
        for _ in range(2):
            pltpu.make_async_copy(hbuf.at[sub], xs_hbm.at[pl.ds(0, tm)], sems.at[sub]).wait()

    for sub in range(2):
        rows = pl.ds(sub * tm, tm)
        hbuf[sub] = _pack_halves(x_ref[rows, :] * (1.0 + sc_ref[0]) + sh_ref[0])
        scatter_start(sub)
        if sub == 0:
            @pl.when(i > 0)
            def _():
                scatter_wait(1)
        else:
            scatter_wait(0)

    @pl.when(i == n_steps - 1)
    def _():
        scatter_wait(1)
        fill_copies(lambda cp: cp.wait())


def _dispatch(x, posT, pad_start, pad_len, mods3, mod_row, layer, n_rows, tm, moe_tm):
    m, d = x.shape
    base = layer * MOD_ROWS
    step = 2 * tm
    pad_bits = tuple(1 << b for b in range(moe_tm.bit_length() - 2, SUBLANES.bit_length() - 2, -1))
    grid_spec = pltpu.PrefetchScalarGridSpec(
        num_scalar_prefetch=2,
        grid=(m // step,),
        in_specs=[
            pl.BlockSpec((1, 8, step), lambda i, ps, pn: (i, 0, 0), memory_space=pltpu.SMEM),
            pl.BlockSpec((step, d), lambda i, ps, pn: (i, 0)),
            pl.BlockSpec((1, 1, d), lambda i, ps, pn: (base + mod_row(i, step), 0, 4)),
            pl.BlockSpec((1, 1, d), lambda i, ps, pn: (base + mod_row(i, step), 0, 3)),
        ],
        out_specs=pl.BlockSpec(memory_space=pl.ANY),
        scratch_shapes=[
            pltpu.VMEM((2, tm, d // 2), U32),
            pltpu.VMEM((moe_tm // 2, d // 2), U32),
            pltpu.SemaphoreType.DMA((2,)),
            pltpu.SemaphoreType.DMA(()),
        ],
    )
    return pl.pallas_call(
        functools.partial(_dispatch_kernel, tm, pad_bits),
        grid_spec=grid_spec,
        out_shape=jax.ShapeDtypeStruct((n_rows, d // 2), U32),
        compiler_params=_cparams(("arbitrary",), 2 * step * d * 4 + 3 * tm * d * 4 + (8 << 20)),
    )(pad_start, pad_len, posT, x, mods3, mods3)


def _moe_kernel(layer, te_ref, nu_ref, nxt_ref, par_ref, xs_ref, wg_hbm, wu_hbm, wd_hbm, ys_ref,
                wg_f, wu_f, wd_f, wg_b, wu_b, wd_b, sems):
    i = pl.program_id(0)
    used = i < nu_ref[0]
    expert = te_ref[i]
    first = used & ((i == 0) | (expert != te_ref[jnp.maximum(i - 1, 0)]))

    def weight_copies(e, slot):
        return (pltpu.make_async_copy(wg_hbm.at[layer, e], wg_f.at[slot], sems.at[slot]),
                pltpu.make_async_copy(wu_hbm.at[layer, e], wu_f.at[slot], sems.at[slot]),
                pltpu.make_async_copy(wd_hbm.at[layer, e], wd_f.at[slot], sems.at[slot]))

    @pl.when(jnp.logical_not(used))
    def _():
        ys_ref[...] = jnp.zeros_like(ys_ref)

    @pl.when(i == 0)
    def _():
        for cp in weight_copies(expert, par_ref[0]):
            cp.start(priority=1)

    @pl.when(first)
    def _():
        slot = par_ref[i]
        for cp in weight_copies(expert, slot):
            cp.wait()

        @pl.when(nxt_ref[i] >= 0)
        def _():
            for cp in weight_copies(nxt_ref[i], 1 - slot):
                cp.start(priority=1)

        wg_b[...] = wg_f[slot].astype(BF16)
        wu_b[...] = wu_f[slot].astype(BF16)
        wd_b[...] = wd_f[slot].astype(BF16)

    @pl.when(used)
    def _():
        x = jnp.concatenate(_unpack_halves(xs_ref[...]), axis=1).astype(BF16)
        g = _dot(x, wg_b[...])
        u = _dot(x, wu_b[...])
        hid = (_silu(g) * u).astype(BF16)
        ys_ref[...] = _pack_halves(_dot(hid, wd_b[...]))


def _moe_experts(xs, tile_expert, n_used, next_expert, slot, w_gate, w_up, w_down, layer, tm):
    n_rows, dp = xs.shape
    d = 2 * dp
    f = w_gate.shape[-1]
    n_tiles = n_rows // tm
    grid_spec = pltpu.PrefetchScalarGridSpec(
        num_scalar_prefetch=4,
        grid=(n_tiles,),
        in_specs=[
            pl.BlockSpec((tm, dp), lambda i, te, nu, nx, pr: (jnp.maximum(jnp.minimum(i, nu[0] - 1), 0), 0)),
            pl.BlockSpec(memory_space=pl.ANY),
            pl.BlockSpec(memory_space=pl.ANY),
            pl.BlockSpec(memory_space=pl.ANY),
        ],
        out_specs=pl.BlockSpec((tm, dp), lambda i, te, nu, nx, pr: (i, 0)),
        scratch_shapes=[
            pltpu.VMEM((2, d, f), F32), pltpu.VMEM((2, d, f), F32), pltpu.VMEM((2, f, d), F32),
            pltpu.VMEM((d, f), BF16), pltpu.VMEM((d, f), BF16), pltpu.VMEM((f, d), BF16),
            pltpu.SemaphoreType.DMA((2,)),
        ],
    )
    return pl.pallas_call(
        functools.partial(_moe_kernel, layer),
        grid_spec=grid_spec,
        out_shape=jax.ShapeDtypeStruct((n_rows, dp), U32),
        compiler_params=_cparams(("arbitrary",), 6 * d * f * 4 + 3 * d * f * 2 + 8 * tm * d * 4 + (8 << 20)),
    )(tile_expert, n_used, next_expert, slot, xs, w_gate, w_up, w_down)


def _combine_ln_kernel(alpha, tm, p_steps, pos_ref, nxt_ref, ys_hbm, info_ref, x_ref, gate_ref, g_ref, b_ref,
                       *rest):
    i = pl.program_id(0)
    n_steps = pl.num_programs(0)
    if p_steps is None:
        nsc_ref, nsh_ref, o_ref, hb_ref, buf, sems = rest
    else:
        op_ref, os_ref, buf, sems = rest

    def gather_start(sub, p_ref):
        for r in range(tm):
            for slot in range(2):
                pltpu.make_async_copy(ys_hbm.at[pl.ds(p_ref[0, slot, sub * tm + r], 1)],
                                      buf.at[sub, slot, pl.ds(r, 1)], sems.at[sub]).start(priority=slot)

    def gather_wait(sub):
        for slot in range(2):
            pltpu.make_async_copy(ys_hbm.at[pl.ds(0, tm)], buf.at[sub, slot], sems.at[sub]).wait()

    def finish(sub):
        rows = pl.ds(sub * tm, tm)
        info = info_ref[rows, :]
        lo1, hi1 = _unpack_halves(buf[sub, 0])
        lo2, hi2 = _unpack_halves(buf[sub, 1])
        c1 = info[:, 2:3]
        c2 = info[:, 3:4]
        y = jnp.concatenate([c1 * lo1 + c2 * lo2, c1 * hi1 + c2 * hi2], axis=1)
        z = alpha * x_ref[rows, :] + gate_ref[0] * y
        res = _layer_norm(z, g_ref[0], b_ref[0])
        if p_steps is None:
            o_ref[rows, :] = res
            hb_ref[rows, :] = (res * (1.0 + nsc_ref[0]) + nsh_ref[0]).astype(BF16)
        else:
            @pl.when(i < p_steps)
            def _():
                op_ref[rows, :] = res

            @pl.when(i >= p_steps)
            def _():
                os_ref[rows, :] = res

    @pl.when(i == 0)
    def _():
        gather_start(0, pos_ref)

    gather_start(1, pos_ref)
    gather_wait(0)
    finish(0)

    @pl.when(i + 1 < n_steps)
    def _():
        gather_start(0, nxt_ref)

    gather_wait(1)
    finish(1)


def _combine_ln(ys, posT, info, x, mods3, mod_row, layer, ln_g, ln_b, alpha, tm, last, p_rows):
    m, d = x.shape
    step = 2 * tm
    n_steps = m // step
    base = layer * MOD_ROWS
    in_specs = [
        pl.BlockSpec((1, 8, step), lambda i: (i, 0, 0), memory_space=pltpu.SMEM),
        pl.BlockSpec((1, 8, step), lambda i: (jnp.minimum(i + 1, n_steps - 1), 0, 0), memory_space=pltpu.SMEM),
        pl.BlockSpec(memory_space=pl.ANY),
        pl.BlockSpec((step, V7X_LANES), lambda i: (i, 0)),
        pl.BlockSpec((step, d), lambda i: (i, 0)),
        pl.BlockSpec((1, 1, d), lambda i: (base + mod_row(i, step), 0, 5)),
        pl.BlockSpec((1, 1, d), lambda i: (layer, 0, 0)),
        pl.BlockSpec((1, 1, d), lambda i: (layer, 0, 0)),
    ]
    args = [posT, posT, ys, info, x, mods3, ln_g, ln_b]
    if last:
        p_steps = p_rows // step
        out_specs = [
            pl.BlockSpec((step, d), lambda i: (jnp.minimum(i, p_steps - 1), 0)),
            pl.BlockSpec((step, d), lambda i: (jnp.maximum(i - p_steps, 0), 0)),
        ]
        out_shape = [jax.ShapeDtypeStruct((p_rows, d), F32), jax.ShapeDtypeStruct((m - p_rows, d), F32)]
    else:
        p_steps = None
        nbase = (layer + 1) * MOD_ROWS
        in_specs += [
            pl.BlockSpec((1, 1, d), lambda i: (nbase + mod_row(i, step), 0, 1)),
            pl.BlockSpec((1, 1, d), lambda i: (nbase + mod_row(i, step), 0, 0)),
        ]
        args += [mods3, mods3]
        out_specs = [pl.BlockSpec((step, d), lambda i: (i, 0)), pl.BlockSpec((step, d), lambda i: (i, 0))]
        out_shape = [jax.ShapeDtypeStruct((m, d), F32), jax.ShapeDtypeStruct((m, d), BF16)]
    return pl.pallas_call(
        functools.partial(_combine_ln_kernel, alpha, tm, p_steps),
        grid=(n_steps,),
        in_specs=in_specs,
        out_specs=out_specs,
        out_shape=out_shape,
        scratch_shapes=[pltpu.VMEM((2, 2, tm, d // 2), U32), pltpu.SemaphoreType.DMA((2,))],
        compiler_params=_cparams(("arbitrary",), 6 * step * d * 4 + 4 * tm * d * 4 + 6 * tm * d * 4 + (8 << 20)),
    )(*args)


def _router_params(layer, w_rg, b_rg, w_re, b_re):
    d = w_rg.shape[1]
    n_e = N_EXPERTS
    pad = V7X_LANES - N_GROUPS - n_e
    wr = jnp.concatenate(
        [w_rg[layer], w_re[layer].transpose(1, 0, 2).reshape(d, n_e), jnp.zeros((d, pad), F32)], axis=1)
    br = jnp.concatenate([b_rg[layer], b_re[layer].reshape(n_e), jnp.zeros((pad,), F32)])[None, :]
    return wr, br


def _hier_moe_ln(x, info, cnt, mods3, mod_row, layer, w_gate, w_up, w_down, ln_g, ln_b, alpha, last, p_rows):
    m, d = x.shape
    n_e = N_EXPERTS

    tm = MOE_TM
    n_tiles = 2 * m // tm + n_e
    counts = cnt[0, :n_e].astype(jnp.int32)
    tiles_e = (counts + tm - 1) // tm
    tile_end = jnp.cumsum(tiles_e)
    n_used = tile_end[-1]
    row_off = (tile_end - tiles_e) * tm
    tile_ids = jnp.minimum(jnp.arange(n_tiles, dtype=jnp.int32), n_used - 1)
    tile_expert = jnp.sum((tile_ids[:, None] >= tile_end[None, :]).astype(jnp.int32), axis=1)
    group_end = tile_end[tile_expert]
    next_expert = jnp.where(group_end < n_used, tile_expert[jnp.minimum(group_end, n_tiles - 1)], -1)
    is_first = jnp.concatenate([jnp.ones((1,), jnp.int32), (tile_expert[1:] != tile_expert[:-1]).astype(jnp.int32)])
    slot = (jnp.cumsum(is_first) - 1) % 2
    zrows = tm // 2
    pad_start = jnp.concatenate([row_off + counts, (n_used * tm)[None]])
    pad_len = jnp.concatenate([tiles_e * tm - counts, ((n_tiles - n_used) * (tm // zrows))[None]])
    row_off_f = jnp.concatenate([row_off.astype(F32), jnp.zeros((V7X_LANES - n_e,), F32)])[None, :]

    ct = COMB_TM
    posT = _positions(info, row_off_f, 2 * ct)
    xs = _dispatch(x, posT, pad_start, pad_len, mods3, mod_row, layer, n_tiles * tm, ct, tm)
    ys = _moe_experts(xs, tile_expert, n_used.reshape(1), next_expert.astype(jnp.int32), slot.astype(jnp.int32),
                      w_gate, w_up, w_down, layer, tm)
    return _combine_ln(ys, posT, info, x, mods3, mod_row, layer, ln_g, ln_b, alpha, ct, last, p_rows)


def kernel(x_prompt, x_sample, state_ret, cache_na_k, cache_na_v, c, c_ctx, w_mod, b_mod, ln1_g, ln1_b, ln2_g, ln2_b, w_ret_in, ret_decay_logit, ret_gn_g, ret_gn_b, w_ret_out, w_na_in, na_rpb, w_na_out, w_rg, b_rg, w_re, b_re, w_gate, w_up, w_down):
    bp, seq, d = x_prompt.shape
    bs, dec_seq, _ = x_sample.shape
    depth = w_mod.shape[0]
    slab = dec_seq
    assert slab % seq == 0 and (bp * seq) % slab == 0 and bs + 1 <= MOD_ROWS
    p_rows = bp * seq
    p_slabs = p_rows // slab
    alpha = (2.0 * depth) ** 0.25

    def mod_row(i, tm):
        start = i * tm
        return jnp.where(start < p_rows, 0, 1 + (start - p_rows) // dec_seq)

    cc = jnp.concatenate([c_ctx[None, :], c, jnp.zeros((MOD_ROWS - 1 - bs, d), F32)], axis=0)
    mods = _modulation(cc, w_mod, b_mod)
    mods3 = mods.reshape(depth * MOD_ROWS, 1, 6 * d)
    x, hb = _prologue(x_prompt.reshape(p_rows, d), x_sample.reshape(bs * dec_seq, d), mods3, mod_row)

    ln1_g3, ln1_b3 = ln1_g.reshape(depth, 1, d), ln1_b.reshape(depth, 1, d)
    ln2_g3, ln2_b3 = ln2_g.reshape(depth, 1, d), ln2_b.reshape(depth, 1, d)

    new_ret, new_k, new_v = [], [], []
    for l in range(depth):
        j = l // 2
        if l % 2 == 0:
            dk = state_ret.shape[-2]
            proj = _inproj(hb, w_ret_in, j)
            log_g = -jax.nn.softplus(-ret_decay_logit[j].astype(F32))
            t = jnp.arange(slab)
            nf = dk // 4
            inv_freq = ROPE_BASE ** (-jnp.arange(nf, dtype=F32) / nf)
            ang_r = (t // GRID_W).astype(F32)[:, None] * inv_freq
            ang_c = (t % GRID_W).astype(F32)[:, None] * inv_freq
            cos = jnp.concatenate([jnp.cos(ang_r)] * 2 + [jnp.cos(ang_c)] * 2, axis=1)
            sin = jnp.concatenate([-jnp.sin(ang_r), jnp.sin(ang_r), -jnp.sin(ang_c), jnp.sin(ang_c)], axis=1)
            y, st = _retention(proj, log_g, cos, sin, state_ret, j, ret_gn_g[j][:, None, :],
                               ret_gn_b[j][:, None, :], p_slabs, slab)
            new_ret.append(st)
            mixed, w_out = y, w_ret_out[j]
        else:
            proj = _inproj(hb, w_na_in, j)
            o, nk, nv = _attention(proj, cache_na_k, cache_na_v, j, _na_strips(na_rpb[j]), p_slabs, slab, seq)
            new_k.append(nk[:, 0])
            new_v.append(nv[:, 0])
            mixed, w_out = o, w_na_out[j]
        wr, br = _router_params(l, w_rg, b_rg, w_re, b_re)
        x, info, cnt = _outproj_ln(mixed, w_out.astype(BF16), x, mods3, mod_row, l, ln1_g3, ln1_b3, wr, br, alpha)
        x, hb = _hier_moe_ln(x, info, cnt, mods3, mod_row, l, w_gate, w_up, w_down, ln2_g3, ln2_b3, alpha,
                             l == depth - 1, p_rows)

    y_prompt = x.reshape(bp, seq, d)
    y_sample = hb.reshape(bs, dec_seq, d)
    return (y_prompt, y_sample, jnp.stack(new_ret, axis=1), jnp.stack(new_k, axis=1), jnp.stack(new_v, axis=1))
```

```python
import functools

import jax
import jax.numpy as jnp
from jax import lax
from jax.experimental import pallas as pl
from jax.experimental.pallas import tpu as pltpu

F32 = jnp.float32
BF16 = jnp.bfloat16

GRID_W = 64
WIN_H = 8
WIN_W = 16
N_GROUPS = 4
N_EXP_PER_GROUP = 8
N_EXPERTS = N_GROUPS * N_EXP_PER_GROUP
ROPE_BASE = 10000.0
LN_EPS = 1e-5
NEG_INF = -1e30

V7X_VMEM_BYTES = 64 * 1024 * 1024
V7X_LANES = 128
SUBLANES = 8
MOD_ROWS = 8

RET_CHUNK = 256
ATTN_QB = 256
ATTN_HEADS_PER_STEP = 2
MOE_TM = 256
COMB_TM = 256


def _cparams(sem, vmem_bytes):
    limit = int(min(max(vmem_bytes, 16 * 1024 * 1024), V7X_VMEM_BYTES - 6 * 1024 * 1024))
    return pltpu.CompilerParams(dimension_semantics=sem, vmem_limit_bytes=limit)


def _silu(x):
    return x / (1.0 + jnp.exp(-x))


U32 = jnp.uint32


def _pack_halves(x):
    n = x.shape[1] // 2
    return pltpu.pack_elementwise([x[:, :n], x[:, n:]], packed_dtype=BF16)


def _unpack_halves(p):
    return (pltpu.unpack_elementwise(p, index=0, packed_dtype=BF16, unpacked_dtype=F32),
            pltpu.unpack_elementwise(p, index=1, packed_dtype=BF16, unpacked_dtype=F32))


def _dot(a, b):
    return jnp.dot(a, b, preferred_element_type=F32)


def _dot_nt(a, b):
    return lax.dot_general(a, b, (((1,), (1,)), ((), ())), preferred_element_type=F32)


def _layer_norm(z, g, b):
    mu = jnp.mean(z, axis=-1, keepdims=True)
    zc = z - mu
    var = jnp.mean(zc * zc, axis=-1, keepdims=True)
    return zc * lax.rsqrt(var + LN_EPS) * g + b


def _mod_kernel(c_ref, w_ref, b_ref, o_ref):
    s = _silu(c_ref[...]).astype(BF16)
    o_ref[0] = _dot(s, w_ref[0].astype(BF16)) + b_ref[0]


def _modulation(cc, w_mod, b_mod):
    depth, d, n6 = w_mod.shape
    tn = 1024
    return pl.pallas_call(
        _mod_kernel,
        grid=(depth, n6 // tn),
        in_specs=[
            pl.BlockSpec((MOD_ROWS, d), lambda l, j: (0, 0)),
            pl.BlockSpec((1, d, tn), lambda l, j: (l, 0, j)),
            pl.BlockSpec((1, 1, tn), lambda l, j: (l, 0, j)),
        ],
        out_specs=pl.BlockSpec((1, MOD_ROWS, tn), lambda l, j: (l, 0, j)),
        out_shape=jax.ShapeDtypeStruct((depth, MOD_ROWS, n6), F32),
        compiler_params=_cparams(("arbitrary", "arbitrary"), 3 * d * tn * 4 + (4 << 20)),
    )(cc, w_mod, b_mod.reshape(depth, 1, n6))


def _prologue_kernel(p_tiles, xp_ref, xs_ref, sc_ref, sh_ref, x_ref, hb_ref):
    @pl.when(pl.program_id(0) < p_tiles)
    def _():
        x = xp_ref[...]
        x_ref[...] = x
        hb_ref[...] = (x * (1.0 + sc_ref[0]) + sh_ref[0]).astype(BF16)

    @pl.when(pl.program_id(0) >= p_tiles)
    def _():
        x = xs_ref[...]
        x_ref[...] = x
        hb_ref[...] = (x * (1.0 + sc_ref[0]) + sh_ref[0]).astype(BF16)


def _prologue(xp, xs, mods3, mod_row, tm=512):
    p_rows, d = xp.shape
    m = p_rows + xs.shape[0]
    p_tiles = p_rows // tm
    return pl.pallas_call(
        functools.partial(_prologue_kernel, p_tiles),
        grid=(m // tm,),
        in_specs=[
            pl.BlockSpec((tm, d), lambda i: (jnp.minimum(i, p_tiles - 1), 0)),
            pl.BlockSpec((tm, d), lambda i: (jnp.maximum(i - p_tiles, 0), 0)),
            pl.BlockSpec((1, 1, d), lambda i: (mod_row(i, tm), 0, 1)),
            pl.BlockSpec((1, 1, d), lambda i: (mod_row(i, tm), 0, 0)),
        ],
        out_specs=[pl.BlockSpec((tm, d), lambda i: (i, 0)), pl.BlockSpec((tm, d), lambda i: (i, 0))],
        out_shape=[jax.ShapeDtypeStruct((m, d), F32), jax.ShapeDtypeStruct((m, d), BF16)],
        compiler_params=_cparams(("arbitrary",), 8 * tm * d * 4 + (8 << 20)),
    )(xp, xs, mods3, mods3)


def _inproj_kernel(h_ref, w_ref, o_ref):
    o_ref[...] = _dot(h_ref[...], w_ref[...].astype(BF16))


def _inproj(hb, w_all, w_idx, tm=2048, tn=512):
    m, d = hb.shape
    n = w_all.shape[-1]
    return pl.pallas_call(
        _inproj_kernel,
        grid=(m // tm, n // tn),
        in_specs=[
            pl.BlockSpec((tm, d), lambda i, j: (i, 0)),
            pl.BlockSpec((None, d, tn), lambda i, j: (w_idx, 0, j)),
        ],
        out_specs=pl.BlockSpec((tm, tn), lambda i, j: (i, j)),
        out_shape=jax.ShapeDtypeStruct((m, n), F32),
        compiler_params=_cparams(
            ("arbitrary", "arbitrary"),
            2 * tm * d * 2 + 2 * d * tn * 4 + d * tn * 2 + 3 * tm * tn * 4 + (4 << 20)),
    )(hb, w_all)


def _outproj_ln_kernel(alpha, y_ref, w_ref, x_ref, gate_ref, g_ref, b_ref, sc_ref, sh_ref, wr_ref, br_ref,
                       o_ref, info_ref, cnt_ref, run_ref):
    @pl.when(pl.program_id(0) == 0)
    def _():
        run_ref[...] = jnp.zeros_like(run_ref)

    z = alpha * x_ref[...] + gate_ref[0] * _dot(y_ref[...], w_ref[...])
    x1 = _layer_norm(z, g_ref[0], b_ref[0])
    o_ref[...] = x1
    info, run = _route(x1 * (1.0 + sc_ref[0]) + sh_ref[0], wr_ref[...], br_ref[...], run_ref[...])
    info_ref[...] = info
    run_ref[...] = run
    cnt_ref[...] = run


def _outproj_ln(y, w, x, mods3, mod_row, layer, ln_g, ln_b, wr, br, alpha, tm=512):
    m, kdim = y.shape
    d = x.shape[1]
    base = layer * MOD_ROWS
    return pl.pallas_call(
        functools.partial(_outproj_ln_kernel, alpha),
        grid=(m // tm,),
        in_specs=[
            pl.BlockSpec((tm, kdim), lambda i: (i, 0)),
            pl.BlockSpec((kdim, d), lambda i: (0, 0)),
            pl.BlockSpec((tm, d), lambda i: (i, 0)),
            pl.BlockSpec((1, 1, d), lambda i: (base + mod_row(i, tm), 0, 2)),
            pl.BlockSpec((1, 1, d), lambda i: (layer, 0, 0)),
            pl.BlockSpec((1, 1, d), lambda i: (layer, 0, 0)),
            pl.BlockSpec((1, 1, d), lambda i: (base + mod_row(i, tm), 0, 4)),
            pl.BlockSpec((1, 1, d), lambda i: (base + mod_row(i, tm), 0, 3)),
            pl.BlockSpec((d, V7X_LANES), lambda i: (0, 0)),
            pl.BlockSpec((1, V7X_LANES), lambda i: (0, 0)),
        ],
        out_specs=[
            pl.BlockSpec((tm, d), lambda i: (i, 0)),
            pl.BlockSpec((tm, V7X_LANES), lambda i: (i, 0)),
            pl.BlockSpec((1, V7X_LANES), lambda i: (0, 0)),
        ],
        out_shape=[
            jax.ShapeDtypeStruct((m, d), F32),
            jax.ShapeDtypeStruct((m, V7X_LANES), F32),
            jax.ShapeDtypeStruct((1, V7X_LANES), F32),
        ],
        scratch_shapes=[pltpu.VMEM((1, V7X_LANES), F32)],
        compiler_params=_cparams(
            ("arbitrary",), kdim * d * 2 + 2 * tm * kdim * 2 + 9 * tm * d * 4 + d * V7X_LANES * 4 + (4 << 20)),
    )(y, w, x, mods3, ln_g, ln_b, mods3, mods3, wr, br)


def _rope(x, cos, sin):
    half = V7X_LANES // 2
    sw = jnp.concatenate(
        [pltpu.roll(x[:, :V7X_LANES], half, 1), pltpu.roll(x[:, V7X_LANES:], half, 1)], axis=1)
    return x * cos + sw * sin


def _group_norm(o, g, b):
    mu = jnp.mean(o, axis=-1, keepdims=True)
    oc = o - mu
    var = jnp.mean(oc * oc, axis=-1, keepdims=True)
    return oc * lax.rsqrt(var + LN_EPS) * g + b


def _ret_kernel(p_slabs, dk, lg_ref, q_ref, k_ref, v_ref, gf_ref, gb_ref, cos_ref, sin_ref, s0_ref,
                gng_ref, gnb_ref, y_ref, st_ref, q_s, k_s, yacc):
    h = pl.program_id(0)
    s = pl.program_id(1)
    is_prompt = s < p_slabs
    c = RET_CHUNK
    n_chunks = q_ref.shape[0] // c

    def run(prompt):
        ri = lax.broadcasted_iota(jnp.int32, (c, c), 0)
        ci = lax.broadcasted_iota(jnp.int32, (c, c), 1)
        diff = (ri - ci).astype(F32)
        pos = lax.broadcasted_iota(jnp.int32, (c, 1), 0).astype(F32)
        gn_g = gng_ref[0]
        gn_b = gnb_ref[0]
        scale = dk ** -0.5
        if not prompt:
            q_s[...] = _rope(q_ref[...], cos_ref[...], sin_ref[...])
            k_s[...] = _rope(k_ref[...], cos_ref[...], sin_ref[...]) * scale

        for direction in (0, 1):
            lg = jnp.full((1, 1), lg_ref[direction, h], F32)
            if direction == 0:
                mask = jnp.where(diff >= 0, jnp.exp(jnp.maximum(diff, 0.0) * lg), 0.0)
                q_dec = jnp.exp((pos + 1.0) * lg)
                k_dec = jnp.exp((c - 1.0 - pos) * lg)
                order = range(n_chunks)
                gate_ref = gf_ref
            else:
                mask = jnp.where(diff <= 0, jnp.exp(jnp.maximum(-diff, 0.0) * lg), 0.0)
                q_dec = jnp.exp((c - pos) * lg)
                k_dec = jnp.exp(pos * lg)
                order = range(n_chunks - 1, -1, -1)
                gate_ref = gb_ref
            c_dec = jnp.exp(c * lg)
            state = None if prompt else s0_ref[0, direction, 0]
            for ch in order:
                rows = pl.ds(ch * c, c)
                if prompt:
                    qc = q_ref[rows, :]
                    kc = k_ref[rows, :] * scale
                else:
                    qc = q_s[rows, :]
                    kc = k_s[rows, :]
                vb = v_ref[rows, :].astype(BF16)
                inner = _dot_nt(qc.astype(BF16), kc.astype(BF16)) * mask
                o = _dot(inner.astype(BF16), vb)
                kv = _dot(jnp.transpose(kc * k_dec).astype(BF16), vb)
                if prompt:
                    st_ref[ch, direction, 0] = kv
                else:
                    o = o + _dot((qc * q_dec).astype(BF16), state.astype(BF16))
                    state = c_dec * state + kv
                contrib = _group_norm(o, gn_g, gn_b) * _silu(gate_ref[rows, :])
                if direction == 0:
                    yacc[rows, :] = contrib
                else:
                    y_ref[rows, :] = (yacc[rows, :] + contrib).astype(BF16)

    @pl.when(is_prompt)
    def _():
        run(True)

    @pl.when(jnp.logical_not(is_prompt))
    def _():
        run(False)


def _retention(proj, log_g, rope_cos, rope_sin, state_ret, ret_idx, gn_g, gn_b, p_slabs, slab):
    m = proj.shape[0]
    heads, dk, dv = state_ret.shape[-3:]
    n_slabs = m // slab
    per_slab = slab // RET_CHUNK
    v_blk0 = 2 * heads * dk // dv
    p_last = p_slabs - 1

    def s0_map(h, s):
        return (jnp.maximum(s - p_slabs, 0), ret_idx, 0, h, 0, 0)

    kernel = functools.partial(_ret_kernel, p_slabs, dk)
    return pl.pallas_call(
        kernel,
        grid=(heads, n_slabs),
        in_specs=[
            pl.BlockSpec(memory_space=pltpu.SMEM),
            pl.BlockSpec((slab, dk), lambda h, s: (s, h)),
            pl.BlockSpec((slab, dk), lambda h, s: (s, heads + h)),
            pl.BlockSpec((slab, dv), lambda h, s: (s, v_blk0 + h)),
            pl.BlockSpec((slab, dv), lambda h, s: (s, v_blk0 + heads + h)),
            pl.BlockSpec((slab, dv), lambda h, s: (s, v_blk0 + 2 * heads + h)),
            pl.BlockSpec((slab, dk), lambda h, s: (0, 0)),
            pl.BlockSpec((slab, dk), lambda h, s: (0, 0)),
            pl.BlockSpec((1, None, 2, 1, dk, dv), s0_map),
            pl.BlockSpec((1, 1, dv), lambda h, s: (h, 0, 0)),
            pl.BlockSpec((1, 1, dv), lambda h, s: (h, 0, 0)),
        ],
        out_specs=[
            pl.BlockSpec((slab, dv), lambda h, s: (s, h)),
            pl.BlockSpec((per_slab, 2, 1, dk, dv), lambda h, s: (jnp.minimum(s, p_last), 0, h, 0, 0)),
        ],
        out_shape=[
            jax.ShapeDtypeStruct((m, heads * dv), BF16),
            jax.ShapeDtypeStruct((p_slabs * per_slab, 2, heads, dk, dv), F32),
        ],
        scratch_shapes=[
            pltpu.VMEM((slab, dk), F32),
            pltpu.VMEM((slab, dk), F32),
            pltpu.VMEM((slab, dv), F32),
        ],
        compiler_params=_cparams(
            ("arbitrary", "arbitrary"),
            2 * (2 * slab * dk + 3 * slab * dv) * 4 + 4 * slab * dk * 4 + 2 * 2 * dk * dv * 4
            + 2 * slab * dv * 2 + 2 * per_slab * 2 * dk * dv * 4 + 2 * slab * dk * 4 + slab * dv * 4
            + (8 << 20)),
    )(log_g, proj, proj, proj, proj, proj, rope_cos, rope_sin, state_ret, gn_g, gn_b)


def _na_window(rq, rows):
    kh = min(WIN_H, rows)
    return min(max(rq - kh // 2, 0), rows - kh), kh


def _attn_kernel(p_slabs, seq, dh, q_ref, k_ref, v_ref, ck_ref, cv_ref, strip_ref, o_ref, nk_ref, nv_ref):
    s = pl.program_id(1)
    is_prompt = s < p_slabs
    slab = q_ref.shape[0]
    scale = dh ** -0.5

    def prompt_head(hh):
        cols = pl.ds(hh * dh, dh)
        for b in range(slab // seq):
            rows = pl.ds(b * seq, seq)
            k = k_ref[rows, cols]
            v = v_ref[rows, cols]
            nk_ref[b, 0, hh] = k
            nv_ref[b, 0, hh] = v
            q = (q_ref[rows, cols] * scale).astype(BF16)
            sc = _dot_nt(q, k.astype(BF16))
            p = jnp.exp(sc - jnp.max(sc, axis=-1, keepdims=True))
            denom = jnp.sum(p, axis=-1, keepdims=True)
            o_ref[rows, cols] = (_dot(p.astype(BF16), v.astype(BF16)) / denom).astype(BF16)

    def latent_head(hh):
        cols = pl.ds(hh * dh, dh)
        grid_rows = slab // GRID_W
        rows_per_block = ATTN_QB // GRID_W
        ckb = ck_ref[0, 0, hh].astype(BF16)
        cvb = cv_ref[0, 0, hh].astype(BF16)
        for qb in range(slab // ATTN_QB):
            rq0 = qb * rows_per_block
            ka = _na_window(rq0, grid_rows)[0] // 2 * 2
            last0, kh = _na_window(rq0 + rows_per_block - 1, grid_rows)
            kb = -((last0 + kh) // -2) * 2
            n_keys = (kb - ka) * GRID_W
            keys = pl.ds(ka * GRID_W, n_keys)
            pieces = []
            for i in range(rows_per_block):
                rq = rq0 + i
                r0, kh = _na_window(rq, grid_rows)
                first = ka - rq + WIN_H - 1
                piece = strip_ref[hh, first % 2, :, pl.ds((first - first % 2) * GRID_W, n_keys)]
                if r0 != ka or r0 + kh != kb:
                    key_row = ka + (lax.broadcasted_iota(jnp.int32, piece.shape, 1) // GRID_W)
                    piece = jnp.where((key_row >= r0) & (key_row < r0 + kh), piece, NEG_INF)
                pieces.append(piece)
            bias = jnp.concatenate(pieces, axis=0)
            rows = pl.ds(qb * ATTN_QB, ATTN_QB)
            q = (q_ref[rows, cols] * scale).astype(BF16)
            s_loc = _dot_nt(q, k_ref[keys, cols].astype(BF16)) + bias
            s_ctx = _dot_nt(q, ckb)
            mx = jnp.maximum(jnp.max(s_loc, axis=-1, keepdims=True), jnp.max(s_ctx, axis=-1, keepdims=True))
            p_loc = jnp.exp(s_loc - mx)
            p_ctx = jnp.exp(s_ctx - mx)
            denom = jnp.sum(p_loc, axis=-1, keepdims=True) + jnp.sum(p_ctx, axis=-1, keepdims=True)
            o = _dot(p_loc.astype(BF16), v_ref[keys, cols].astype(BF16)) + _dot(p_ctx.astype(BF16), cvb)
            o_ref[rows, cols] = (o / denom).astype(BF16)

    heads_per_step = q_ref.shape[1] // dh

    @pl.when(is_prompt)
    def _():
        for hh in range(heads_per_step):
            prompt_head(hh)

    @pl.when(jnp.logical_not(is_prompt))
    def _():
        for hh in range(heads_per_step):
            latent_head(hh)


def _attention(proj, cache_k, cache_v, cache_idx, strips, p_slabs, slab, seq):
    m = proj.shape[0]
    heads = cache_k.shape[2]
    dh = cache_k.shape[-1]
    past = cache_k.shape[-2]
    n_slabs = m // slab
    per_slab = slab // seq
    p_last = p_slabs - 1

    hp = ATTN_HEADS_PER_STEP
    groups = heads // hp

    def ctx_map(h, s):
        return (jnp.maximum(s - p_slabs, 0), cache_idx, h, 0, 0)

    def new_map(h, s):
        return (jnp.minimum(s, p_last), 0, h, 0, 0)

    kernel = functools.partial(_attn_kernel, p_slabs, seq, dh)
    return pl.pallas_call(
        kernel,
        grid=(groups, n_slabs),
        in_specs=[
            pl.BlockSpec((slab, hp * dh), lambda h, s: (s, h)),
            pl.BlockSpec((slab, hp * dh), lambda h, s: (s, groups + h)),
            pl.BlockSpec((slab, hp * dh), lambda h, s: (s, 2 * groups + h)),
            pl.BlockSpec((1, 1, hp, past, dh), ctx_map),
            pl.BlockSpec((1, 1, hp, past, dh), ctx_map),
            pl.BlockSpec((hp,) + strips.shape[1:], lambda h, s: (h, 0, 0, 0)),
        ],
        out_specs=[
            pl.BlockSpec((slab, hp * dh), lambda h, s: (s, h)),
            pl.BlockSpec((per_slab, 1, hp, seq, dh), new_map),
            pl.BlockSpec((per_slab, 1, hp, seq, dh), new_map),
        ],
        out_shape=[
            jax.ShapeDtypeStruct((m, heads * dh), BF16),
            jax.ShapeDtypeStruct((p_slabs * per_slab, 1, heads, seq, dh), F32),
            jax.ShapeDtypeStruct((p_slabs * per_slab, 1, heads, seq, dh), F32),
        ],
        compiler_params=_cparams(
            ("arbitrary", "arbitrary"),
            hp * (2 * strips.shape[1] * strips.shape[2] * strips.shape[3] * 4 + 16 * slab * dh * 4)
            + 10 * ATTN_QB * (slab + past) * 4 + (8 << 20)),
    )(proj, proj, proj, cache_k, cache_v, strips)


def _na_strips(rpb):
    heads, n_dr, n_dc = rpb.shape
    col = jnp.arange(GRID_W)
    c0 = jnp.clip(col - WIN_W // 2, 0, GRID_W - WIN_W)
    col_ok = (col[None, :] >= c0[:, None]) & (col[None, :] < c0[:, None] + WIN_W)
    dc_idx = jnp.clip(col[None, :] - col[:, None], -(WIN_W - 1), WIN_W - 1) + WIN_W - 1
    onehot = (dc_idx[:, :, None] == jnp.arange(n_dc)[None, None, :]).astype(F32)
    tiles = jnp.einsum("hdk,qck->hqdc", rpb.astype(F32), onehot, precision=lax.Precision.HIGHEST)
    tiles = jnp.where(col_ok[None, :, None, :], tiles, NEG_INF)
    n_tiles = 2 * WIN_H
    neg = jnp.full((heads, GRID_W, n_tiles + 1 - n_dr, GRID_W), NEG_INF, F32)
    tiles = jnp.concatenate([tiles, neg], axis=2)
    both = jnp.stack([tiles[:, :, :n_tiles], tiles[:, :, 1:]], axis=1)
    return both.reshape(heads, 2, GRID_W, n_tiles * GRID_W)


def _route(hm, wr, br, run):
    h_hi = hm.astype(BF16)
    h_lo = (hm - h_hi.astype(F32)).astype(BF16)
    w_hi = wr.astype(BF16)
    w_lo = (wr - w_hi.astype(F32)).astype(BF16)
    logits = _dot(h_hi, w_hi) + (_dot(h_hi, w_lo) + _dot(h_lo, w_hi)) + br
    tm, width = logits.shape
    col = lax.broadcasted_iota(jnp.int32, (tm, width), 1).astype(F32)
    neg = jnp.float32(-3.0e38)

    gl = jnp.where(col < N_GROUPS, logits, neg)
    gmax = jnp.max(gl, axis=-1, keepdims=True)
    gsel = jnp.min(jnp.where(gl == gmax, col, float(width)), axis=-1, keepdims=True)
    p_g = 1.0 / jnp.sum(jnp.where(col < N_GROUPS, jnp.exp(gl - gmax), 0.0), axis=-1, keepdims=True)

    lo = N_GROUPS + N_EXP_PER_GROUP * gsel
    el = jnp.where((col >= lo) & (col < lo + N_EXP_PER_GROUP), logits, neg)
    e1 = jnp.max(el, axis=-1, keepdims=True)
    i1 = jnp.min(jnp.where(el == e1, col, float(width)), axis=-1, keepdims=True)
    el2 = jnp.where(col == i1, neg, el)
    e2 = jnp.max(el2, axis=-1, keepdims=True)
    i2 = jnp.min(jnp.where(el2 == e2, col, float(width)), axis=-1, keepdims=True)
    t = jnp.exp(e2 - e1)
    w1 = 1.0 / (1.0 + t)
    c1 = p_g * w1
    c2 = p_g * (t * w1)
    id1 = i1 - N_GROUPS
    id2 = i2 - N_GROUPS

    onehot = jnp.where((col == id1) | (col == id2), 1.0, 0.0)
    ri = lax.broadcasted_iota(jnp.int32, (tm, tm), 0)
    ci = lax.broadcasted_iota(jnp.int32, (tm, tm), 1)
    tri = jnp.where(ri > ci, 1.0, 0.0).astype(BF16)
    before = _dot(tri, onehot.astype(BF16)) + run
    r1 = jnp.sum(jnp.where(col == id1, before, 0.0), axis=-1, keepdims=True)
    r2 = jnp.sum(jnp.where(col == id2, before, 0.0), axis=-1, keepdims=True)

    info = jnp.where(col == 0, id1, 0.0)
    info = jnp.where(col == 1, id2, info)
    info = jnp.where(col == 2, c1, info)
    info = jnp.where(col == 3, c2, info)
    info = jnp.where(col == 4, r1, info)
    info = jnp.where(col == 5, r2, info)
    return info, run + jnp.sum(onehot, axis=0, keepdims=True)


def _pos_kernel(info_ref, off_ref, pos_ref):
    info = info_ref[...]
    tm, width = info.shape
    col = lax.broadcasted_iota(jnp.int32, (tm, width), 1).astype(F32)
    off = off_ref[...]
    p1 = jnp.sum(jnp.where(col == info[:, 0:1], off, 0.0), axis=-1, keepdims=True) + info[:, 4:5]
    p2 = jnp.sum(jnp.where(col == info[:, 1:2], off, 0.0), axis=-1, keepdims=True) + info[:, 5:6]
    pm = jnp.where(col == 0, p1, jnp.where(col == 1, p2, 0.0))
    pos_ref[0] = jnp.transpose(pm)[:8, :].astype(jnp.int32)


def _positions(info, row_off, tm):
    m = info.shape[0]
    return pl.pallas_call(
        _pos_kernel,
        grid=(m // tm,),
        in_specs=[
            pl.BlockSpec((tm, V7X_LANES), lambda i: (i, 0)),
            pl.BlockSpec((1, V7X_LANES), lambda i: (0, 0)),
        ],
        out_specs=pl.BlockSpec((1, 8, tm), lambda i: (i, 0, 0)),
        out_shape=jax.ShapeDtypeStruct((m // tm, 8, tm), jnp.int32),
        compiler_params=_cparams(("arbitrary",), 16 << 20),
    )(info, row_off)


def _dispatch_kernel(tm, pad_bits, ps_ref, pl_ref, pos_ref, x_ref, sc_ref, sh_ref, xs_hbm,
                     hbuf, zbuf, sems, zsem):
    i = pl.program_id(0)
    n_steps = pl.num_programs(0)
    n_experts = ps_ref.shape[0] - 1
    zrows = zbuf.shape[0]
    max_tail = (xs_hbm.shape[0] - 2 * x_ref.shape[0] * n_steps) // zrows

    def fill_copies(do):
        for e in range(n_experts):
            start = ps_ref[e]
            rem = pl_ref[e]
            for r in range(SUBLANES - 1):
                @pl.when(r < (rem & (SUBLANES - 1)))
                def _(start=start, r=r):
                    do(pltpu.make_async_copy(zbuf.at[pl.ds(0, 1)], xs_hbm.at[pl.ds(start + r, 1)], zsem))

            off = start + rem
            for bit in pad_bits:
                off = off - (rem & bit)

                @pl.when((rem & bit) != 0)
                def _(off=off, bit=bit):
                    dst = xs_hbm.at[pl.ds(pl.multiple_of(off, SUBLANES), bit)]
                    do(pltpu.make_async_copy(zbuf.at[pl.ds(0, bit)], dst, zsem))
        for t in range(max_tail):
            @pl.when(t < pl_ref[n_experts])
            def _(t=t):
                dst = xs_hbm.at[pl.ds(pl.multiple_of(ps_ref[n_experts] + t * zrows, SUBLANES), zrows)]
                do(pltpu.make_async_copy(zbuf, dst, zsem))

    @pl.when(i == 0)
    def _():
        zbuf[...] = jnp.zeros_like(zbuf)
        fill_copies(lambda cp: cp.start())

    def scatter_start(sub):
        for r in range(tm):
            for slot in range(2):
                pltpu.make_async_copy(hbuf.at[sub, pl.ds(r, 1)],
                                      xs_hbm.at[pl.ds(pos_ref[0, slot, sub * tm + r], 1)],
                                      sems.at[sub]).start(priority=slot)

    def scatter_wait(sub):
        for _ in range(2):
            pltpu.make_async_copy(hbuf.at[sub], xs_hbm.at[pl.ds(0, tm)], sems.at[sub]).wait()

    for sub in range(2):
        rows = pl.ds(sub * tm, tm)
        hbuf[sub] = _pack_halves(x_ref[rows, :] * (1.0 + sc_ref[0]) + sh_ref[0])
        scatter_start(sub)
        if sub == 0:
            @pl.when(i > 0)
            def _():
                scatter_wait(1)
        else:
            scatter_wait(0)

    @pl.when(i == n_steps - 1)
    def _():
        scatter_wait(1)
        fill_copies(lambda cp: cp.wait())


def _dispatch(x, posT, pad_start, pad_len, mods3, mod_row, layer, n_rows, tm, moe_tm):
    m, d = x.shape
    base = layer * MOD_ROWS
    step = 2 * tm
    pad_bits = tuple(1 << b for b in range(moe_tm.bit_length() - 2, SUBLANES.bit_length() - 2, -1))
    grid_spec = pltpu.PrefetchScalarGridSpec(
        num_scalar_prefetch=2,
        grid=(m // step,),
        in_specs=[
            pl.BlockSpec((1, 8, step), lambda i, ps, pn: (i, 0, 0), memory_space=pltpu.SMEM),
            pl.BlockSpec((step, d), lambda i, ps, pn: (i, 0)),
            pl.BlockSpec((1, 1, d), lambda i, ps, pn: (base + mod_row(i, step), 0, 4)),
            pl.BlockSpec((1, 1, d), lambda i, ps, pn: (base + mod_row(i, step), 0, 3)),
        ],
        out_specs=pl.BlockSpec(memory_space=pl.ANY),
        scratch_shapes=[
            pltpu.VMEM((2, tm, d // 2), U32),
            pltpu.VMEM((moe_tm // 2, d // 2), U32),
            pltpu.SemaphoreType.DMA((2,)),
            pltpu.SemaphoreType.DMA(()),
        ],
    )
    return pl.pallas_call(
        functools.partial(_dispatch_kernel, tm, pad_bits),
        grid_spec=grid_spec,
        out_shape=jax.ShapeDtypeStruct((n_rows, d // 2), U32),
        compiler_params=_cparams(("arbitrary",), 2 * step * d * 4 + 3 * tm * d * 4 + (8 << 20)),
    )(pad_start, pad_len, posT, x, mods3, mods3)


def _moe_kernel(layer, te_ref, nu_ref, nxt_ref, par_ref, xs_ref, wg_hbm, wu_hbm, wd_hbm, ys_ref,
                wg_f, wu_f, wd_f, wg_b, wu_b, wd_b, sems):
    i = pl.program_id(0)
    used = i < nu_ref[0]
    expert = te_ref[i]
    first = used & ((i == 0) | (expert != te_ref[jnp.maximum(i - 1, 0)]))

    def weight_copies(e, slot):
        return (pltpu.make_async_copy(wg_hbm.at[layer, e], wg_f.at[slot], sems.at[slot]),
                pltpu.make_async_copy(wu_hbm.at[layer, e], wu_f.at[slot], sems.at[slot]),
                pltpu.make_async_copy(wd_hbm.at[layer, e], wd_f.at[slot], sems.at[slot]))

    @pl.when(jnp.logical_not(used))
    def _():
        ys_ref[...] = jnp.zeros_like(ys_ref)

    @pl.when(i == 0)
    def _():
        for cp in weight_copies(expert, par_ref[0]):
            cp.start(priority=1)

    def mlp(recast_slot):
        x = jnp.concatenate(_unpack_halves(xs_ref[...]), axis=1).astype(BF16)
        if recast_slot is not None:
            wg_b[...] = wg_f[recast_slot].astype(BF16)
        g = _dot(x, wg_b[...])
        if recast_slot is not None:
            wu_b[...] = wu_f[recast_slot].astype(BF16)
        u = _dot(x, wu_b[...])
        if recast_slot is not None:
            wd_b[...] = wd_f[recast_slot].astype(BF16)
        hid = (_silu(g) * u).astype(BF16)
        ys_ref[...] = _pack_halves(_dot(hid, wd_b[...]))

    @pl.when(first)
    def _():
        slot = par_ref[i]
        for cp in weight_copies(expert, slot):
            cp.wait()

        @pl.when(nxt_ref[i] >= 0)
        def _():
            for cp in weight_copies(nxt_ref[i], 1 - slot):
                cp.start(priority=1)

        mlp(slot)

    @pl.when(used & jnp.logical_not(first))
    def _():
        mlp(None)


def _moe_experts(xs, tile_expert, n_used, next_expert, slot, w_gate, w_up, w_down, layer, tm):
    n_rows, dp = xs.shape
    d = 2 * dp
    f = w_gate.shape[-1]
    n_tiles = n_rows // tm
    grid_spec = pltpu.PrefetchScalarGridSpec(
        num_scalar_prefetch=4,
        grid=(n_tiles,),
        in_specs=[
            pl.BlockSpec((tm, dp), lambda i, te, nu, nx, pr: (jnp.maximum(jnp.minimum(i, nu[0] - 1), 0), 0)),
            pl.BlockSpec(memory_space=pl.ANY),
            pl.BlockSpec(memory_space=pl.ANY),
            pl.BlockSpec(memory_space=pl.ANY),
        ],
        out_specs=pl.BlockSpec((tm, dp), lambda i, te, nu, nx, pr: (i, 0)),
        scratch_shapes=[
            pltpu.VMEM((2, d, f), F32), pltpu.VMEM((2, d, f), F32), pltpu.VMEM((2, f, d), F32),
            pltpu.VMEM((d, f), BF16), pltpu.VMEM((d, f), BF16), pltpu.VMEM((f, d), BF16),
            pltpu.SemaphoreType.DMA((2,)),
        ],
    )
    return pl.pallas_call(
        functools.partial(_moe_kernel, layer),
        grid_spec=grid_spec,
        out_shape=jax.ShapeDtypeStruct((n_rows, dp), U32),
        compiler_params=_cparams(("arbitrary",), 6 * d * f * 4 + 3 * d * f * 2 + 8 * tm * d * 4 + (8 << 20)),
    )(tile_expert, n_used, next_expert, slot, xs, w_gate, w_up, w_down)


def _combine_ln_kernel(alpha, tm, p_steps, pos_ref, nxt_ref, ys_hbm, info_ref, x_ref, gate_ref, g_ref, b_ref,
                       *rest):
    i = pl.program_id(0)
    n_steps = pl.num_programs(0)
    if p_steps is None:
        nsc_ref, nsh_ref, o_ref, hb_ref, buf, sems = rest
    else:
        op_ref, os_ref, buf, sems = rest

    def gather_start(sub, p_ref):
        for r in range(tm):
            for slot in range(2):
                pltpu.make_async_copy(ys_hbm.at[pl.ds(p_ref[0, slot, sub * tm + r], 1)],
                                      buf.at[sub, slot, pl.ds(r, 1)], sems.at[sub]).start(priority=slot)

    def gather_wait(sub):
        for slot in range(2):
            pltpu.make_async_copy(ys_hbm.at[pl.ds(0, tm)], buf.at[sub, slot], sems.at[sub]).wait()

    def finish(sub):
        rows = pl.ds(sub * tm, tm)
        info = info_ref[rows, :]
        lo1, hi1 = _unpack_halves(buf[sub, 0])
        lo2, hi2 = _unpack_halves(buf[sub, 1])
        c1 = info[:, 2:3]
        c2 = info[:, 3:4]
        y = jnp.concatenate([c1 * lo1 + c2 * lo2, c1 * hi1 + c2 * hi2], axis=1)
        z = alpha * x_ref[rows, :] + gate_ref[0] * y
        res = _layer_norm(z, g_ref[0], b_ref[0])
        if p_steps is None:
            o_ref[rows, :] = res
            hb_ref[rows, :] = (res * (1.0 + nsc_ref[0]) + nsh_ref[0]).astype(BF16)
        else:
            @pl.when(i < p_steps)
            def _():
                op_ref[rows, :] = res

            @pl.when(i >= p_steps)
            def _():
                os_ref[rows, :] = res

    @pl.when(i == 0)
    def _():
        gather_start(0, pos_ref)

    gather_start(1, pos_ref)
    gather_wait(0)
    finish(0)

    @pl.when(i + 1 < n_steps)
    def _():
        gather_start(0, nxt_ref)

    gather_wait(1)
    finish(1)


def _combine_ln(ys, posT, info, x, mods3, mod_row, layer, ln_g, ln_b, alpha, tm, last, p_rows):
    m, d = x.shape
    step = 2 * tm
    n_steps = m // step
    base = layer * MOD_ROWS
    in_specs = [
        pl.BlockSpec((1, 8, step), lambda i: (i, 0, 0), memory_space=pltpu.SMEM),
        pl.BlockSpec((1, 8, step), lambda i: (jnp.minimum(i + 1, n_steps - 1), 0, 0), memory_space=pltpu.SMEM),
        pl.BlockSpec(memory_space=pl.ANY),
        pl.BlockSpec((step, V7X_LANES), lambda i: (i, 0)),
        pl.BlockSpec((step, d), lambda i: (i, 0)),
        pl.BlockSpec((1, 1, d), lambda i: (base + mod_row(i, step), 0, 5)),
        pl.BlockSpec((1, 1, d), lambda i: (layer, 0, 0)),
        pl.BlockSpec((1, 1, d), lambda i: (layer, 0, 0)),
    ]
    args = [posT, posT, ys, info, x, mods3, ln_g, ln_b]
    if last:
        p_steps = p_rows // step
        out_specs = [
            pl.BlockSpec((step, d), lambda i: (jnp.minimum(i, p_steps - 1), 0)),
            pl.BlockSpec((step, d), lambda i: (jnp.maximum(i - p_steps, 0), 0)),
        ]
        out_shape = [jax.ShapeDtypeStruct((p_rows, d), F32), jax.ShapeDtypeStruct((m - p_rows, d), F32)]
    else:
        p_steps = None
        nbase = (layer + 1) * MOD_ROWS
        in_specs += [
            pl.BlockSpec((1, 1, d), lambda i: (nbase + mod_row(i, step), 0, 1)),
            pl.BlockSpec((1, 1, d), lambda i: (nbase + mod_row(i, step), 0, 0)),
        ]
        args += [mods3, mods3]
        out_specs = [pl.BlockSpec((step, d), lambda i: (i, 0)), pl.BlockSpec((step, d), lambda i: (i, 0))]
        out_shape = [jax.ShapeDtypeStruct((m, d), F32), jax.ShapeDtypeStruct((m, d), BF16)]
    return pl.pallas_call(
        functools.partial(_combine_ln_kernel, alpha, tm, p_steps),
        grid=(n_steps,),
        in_specs=in_specs,
        out_specs=out_specs,
        out_shape=out_shape,
        scratch_shapes=[pltpu.VMEM((2, 2, tm, d // 2), U32), pltpu.SemaphoreType.DMA((2,))],
        compiler_params=_cparams(("arbitrary",), 6 * step * d * 4 + 4 * tm * d * 4 + 6 * tm * d * 4 + (8 << 20)),
    )(*args)


def _router_params(layer, w_rg, b_rg, w_re, b_re):
    d = w_rg.shape[1]
    n_e = N_EXPERTS
    pad = V7X_LANES - N_GROUPS - n_e
    wr = jnp.concatenate(
        [w_rg[layer], w_re[layer].transpose(1, 0, 2).reshape(d, n_e), jnp.zeros((d, pad), F32)], axis=1)
    br = jnp.concatenate([b_rg[layer], b_re[layer].reshape(n_e), jnp.zeros((pad,), F32)])[None, :]
    return wr, br


def _hier_moe_ln(x, info, cnt, mods3, mod_row, layer, w_gate, w_up, w_down, ln_g, ln_b, alpha, last, p_rows):
    m, d = x.shape
    n_e = N_EXPERTS

    tm = MOE_TM
    n_tiles = 2 * m // tm + n_e
    counts = cnt[0, :n_e].astype(jnp.int32)
    tiles_e = (counts + tm - 1) // tm
    tile_end = jnp.cumsum(tiles_e)
    n_used = tile_end[-1]
    row_off = (tile_end - tiles_e) * tm
    tile_ids = jnp.minimum(jnp.arange(n_tiles, dtype=jnp.int32), n_used - 1)
    tile_expert = jnp.sum((tile_ids[:, None] >= tile_end[None, :]).astype(jnp.int32), axis=1)
    group_end = tile_end[tile_expert]
    next_expert = jnp.where(group_end < n_used, tile_expert[jnp.minimum(group_end, n_tiles - 1)], -1)
    is_first = jnp.concatenate([jnp.ones((1,), jnp.int32), (tile_expert[1:] != tile_expert[:-1]).astype(jnp.int32)])
    slot = (jnp.cumsum(is_first) - 1) % 2
    zrows = tm // 2
    pad_start = jnp.concatenate([row_off + counts, (n_used * tm)[None]])
    pad_len = jnp.concatenate([tiles_e * tm - counts, ((n_tiles - n_used) * (tm // zrows))[None]])
    row_off_f = jnp.concatenate([row_off.astype(F32), jnp.zeros((V7X_LANES - n_e,), F32)])[None, :]

    ct = COMB_TM
    posT = _positions(info, row_off_f, 2 * ct)
    xs = _dispatch(x, posT, pad_start, pad_len, mods3, mod_row, layer, n_tiles * tm, ct, tm)
    ys = _moe_experts(xs, tile_expert, n_used.reshape(1), next_expert.astype(jnp.int32), slot.astype(jnp.int32),
                      w_gate, w_up, w_down, layer, tm)
    return _combine_ln(ys, posT, info, x, mods3, mod_row, layer, ln_g, ln_b, alpha, ct, last, p_rows)


def kernel(x_prompt, x_sample, state_ret, cache_na_k, cache_na_v, c, c_ctx, w_mod, b_mod, ln1_g, ln1_b, ln2_g, ln2_b, w_ret_in, ret_decay_logit, ret_gn_g, ret_gn_b, w_ret_out, w_na_in, na_rpb, w_na_out, w_rg, b_rg, w_re, b_re, w_gate, w_up, w_down):
    bp, seq, d = x_prompt.shape
    bs, dec_seq, _ = x_sample.shape
    depth = w_mod.shape[0]
    slab = dec_seq
    assert slab % seq == 0 and (bp * seq) % slab == 0 and bs + 1 <= MOD_ROWS
    p_rows = bp * seq
    p_slabs = p_rows // slab
    alpha = (2.0 * depth) ** 0.25

    def mod_row(i, tm):
        start = i * tm
        return jnp.where(start < p_rows, 0, 1 + (start - p_rows) // dec_seq)

    cc = jnp.concatenate([c_ctx[None, :], c, jnp.zeros((MOD_ROWS - 1 - bs, d), F32)], axis=0)
    mods = _modulation(cc, w_mod, b_mod)
    mods3 = mods.reshape(depth * MOD_ROWS, 1, 6 * d)
    x, hb = _prologue(x_prompt.reshape(p_rows, d), x_sample.reshape(bs * dec_seq, d), mods3, mod_row)

    ln1_g3, ln1_b3 = ln1_g.reshape(depth, 1, d), ln1_b.reshape(depth, 1, d)
    ln2_g3, ln2_b3 = ln2_g.reshape(depth, 1, d), ln2_b.reshape(depth, 1, d)

    new_ret, new_k, new_v = [], [], []
    for l in range(depth):
        j = l // 2
        if l % 2 == 0:
            dk = state_ret.shape[-2]
            proj = _inproj(hb, w_ret_in, j)
            log_g = -jax.nn.softplus(-ret_decay_logit[j].astype(F32))
            t = jnp.arange(slab)
            nf = dk // 4
            inv_freq = ROPE_BASE ** (-jnp.arange(nf, dtype=F32) / nf)
            ang_r = (t // GRID_W).astype(F32)[:, None] * inv_freq
            ang_c = (t % GRID_W).astype(F32)[:, None] * inv_freq
            cos = jnp.concatenate([jnp.cos(ang_r)] * 2 + [jnp.cos(ang_c)] * 2, axis=1)
            sin = jnp.concatenate([-jnp.sin(ang_r), jnp.sin(ang_r), -jnp.sin(ang_c), jnp.sin(ang_c)], axis=1)
            y, st = _retention(proj, log_g, cos, sin, state_ret, j, ret_gn_g[j][:, None, :],
                               ret_gn_b[j][:, None, :], p_slabs, slab)
            new_ret.append(st)
            mixed, w_out = y, w_ret_out[j]
        else:
            proj = _inproj(hb, w_na_in, j)
            o, nk, nv = _attention(proj, cache_na_k, cache_na_v, j, _na_strips(na_rpb[j]), p_slabs, slab, seq)
            new_k.append(nk[:, 0])
            new_v.append(nv[:, 0])
            mixed, w_out = o, w_na_out[j]
        wr, br = _router_params(l, w_rg, b_rg, w_re, b_re)
        x, info, cnt = _outproj_ln(mixed, w_out.astype(BF16), x, mods3, mod_row, l, ln1_g3, ln1_b3, wr, br, alpha)
        x, hb = _hier_moe_ln(x, info, cnt, mods3, mod_row, l, w_gate, w_up, w_down, ln2_g3, ln2_b3, alpha,
                             l == depth - 1, p_rows)

    y_prompt = x.reshape(bp, seq, d)
    y_sample = hb.reshape(bs, dec_seq, d)
    return (y_prompt, y_sample, jnp.stack(new_ret, axis=1), jnp.stack(new_k, axis=1), jnp.stack(new_v, axis=1))
```

```python
import functools

import jax
import jax.numpy as jnp
from jax import lax
from jax.experimental import pallas as pl
from jax.experimental.pallas import tpu as pltpu

F32 = jnp.float32
BF16 = jnp.bfloat16
U32 = jnp.uint32
MIB = 1024 * 1024

GRID_W = 64
WIN_H = 8
WIN_W = 16
N_GROUPS = 4
N_EXP_PER_GROUP = 8
N_EXPERTS = N_GROUPS * N_EXP_PER_GROUP
ROPE_BASE = 10000.0
LN_EPS = 1e-5
NEG_INF = -1e30
LOG2_E = 1.4426950408889634

V7X_VMEM_BYTES = 64 * MIB
V7X_LANES = 128
SUBLANES = 8
MOD_ROWS = 8
MIX_SHIFT, MIX_SCALE, MIX_GATE, MOE_SHIFT, MOE_SCALE, MOE_GATE = range(6)

RET_CHUNK = 256
ATTN_QB = 256
ATTN_HEADS_PER_STEP = 4
MOE_TM = 256
COMB_TM = 256


def _cparams(sem, vmem_bytes):
    limit = int(min(max(vmem_bytes, 16 * MIB), V7X_VMEM_BYTES - 6 * MIB))
    return pltpu.CompilerParams(dimension_semantics=sem, vmem_limit_bytes=limit)


def _silu(x):
    return x / (1.0 + jnp.exp(-x))


def _pack_halves(x):
    n = x.shape[1] // 2
    return pltpu.pack_elementwise([x[:, :n], x[:, n:]], packed_dtype=BF16)


def _unpack_halves(p):
    return (pltpu.unpack_elementwise(p, index=0, packed_dtype=BF16, unpacked_dtype=F32),
            pltpu.unpack_elementwise(p, index=1, packed_dtype=BF16, unpacked_dtype=F32))


def _dot(a, b):
    return jnp.dot(a, b, preferred_element_type=F32)


def _dot_nt(a, b):
    return lax.dot_general(a, b, (((1,), (1,)), ((), ())), preferred_element_type=F32)


def _layer_norm(z, g, b):
    mu = jnp.mean(z, axis=-1, keepdims=True)
    zc = z - mu
    var = jnp.mean(zc * zc, axis=-1, keepdims=True)
    return zc * lax.rsqrt(var + LN_EPS) * g + b


def _mod_kernel(c_ref, w_ref, b_ref, o_ref):
    s = _silu(c_ref[...]).astype(BF16)
    o_ref[0] = _dot(s, w_ref[0].astype(BF16)) + b_ref[0]


def _modulation(cc, w_mod, b_mod):
    depth, d, n6 = w_mod.shape
    tn = 1024
    return pl.pallas_call(
        _mod_kernel,
        grid=(depth, n6 // tn),
        in_specs=[
            pl.BlockSpec((MOD_ROWS, d), lambda l, j: (0, 0)),
            pl.BlockSpec((1, d, tn), lambda l, j: (l, 0, j)),
            pl.BlockSpec((1, 1, tn), lambda l, j: (l, 0, j)),
        ],
        out_specs=pl.BlockSpec((1, MOD_ROWS, tn), lambda l, j: (l, 0, j)),
        out_shape=jax.ShapeDtypeStruct((depth, MOD_ROWS, n6), F32),
        compiler_params=_cparams(("arbitrary", "arbitrary"), 3 * d * tn * 4 + 4 * MIB),
    )(cc, w_mod, b_mod.reshape(depth, 1, n6))


def _prologue_kernel(p_tiles, xp_ref, xs_ref, sc_ref, sh_ref, x_ref, hb_ref):
    @pl.when(pl.program_id(0) < p_tiles)
    def _():
        x = xp_ref[...]
        x_ref[...] = x
        hb_ref[...] = (x * (1.0 + sc_ref[0]) + sh_ref[0]).astype(BF16)

    @pl.when(pl.program_id(0) >= p_tiles)
    def _():
        x = xs_ref[...]
        x_ref[...] = x
        hb_ref[...] = (x * (1.0 + sc_ref[0]) + sh_ref[0]).astype(BF16)


def _prologue(xp, xs, mods3, mod_row, tm=512):
    p_rows, d = xp.shape
    m = p_rows + xs.shape[0]
    p_tiles = p_rows // tm
    return pl.pallas_call(
        functools.partial(_prologue_kernel, p_tiles),
        grid=(m // tm,),
        in_specs=[
            pl.BlockSpec((tm, d), lambda i: (jnp.minimum(i, p_tiles - 1), 0)),
            pl.BlockSpec((tm, d), lambda i: (jnp.maximum(i - p_tiles, 0), 0)),
            pl.BlockSpec((1, 1, d), lambda i: (mod_row(i, tm), 0, MIX_SCALE)),
            pl.BlockSpec((1, 1, d), lambda i: (mod_row(i, tm), 0, MIX_SHIFT)),
        ],
        out_specs=[pl.BlockSpec((tm, d), lambda i: (i, 0)), pl.BlockSpec((tm, d), lambda i: (i, 0))],
        out_shape=[jax.ShapeDtypeStruct((m, d), F32), jax.ShapeDtypeStruct((m, d), BF16)],
        compiler_params=_cparams(("arbitrary",), 8 * tm * d * 4 + 8 * MIB),
    )(xp, xs, mods3, mods3)


def _inproj_kernel(h_ref, w_ref, o_ref):
    o_ref[...] = _dot(h_ref[...], w_ref[...].astype(BF16))


def _inproj(hb, w_all, w_idx, tm=2048, tn=512):
    m, d = hb.shape
    n = w_all.shape[-1]
    return pl.pallas_call(
        _inproj_kernel,
        grid=(m // tm, n // tn),
        in_specs=[
            pl.BlockSpec((tm, d), lambda i, j: (i, 0)),
            pl.BlockSpec((None, d, tn), lambda i, j: (w_idx, 0, j)),
        ],
        out_specs=pl.BlockSpec((tm, tn), lambda i, j: (i, j)),
        out_shape=jax.ShapeDtypeStruct((m, n), F32),
        compiler_params=_cparams(
            ("arbitrary", "arbitrary"),
            2 * tm * d * 2 + 2 * d * tn * 4 + d * tn * 2 + 3 * tm * tn * 4 + 4 * MIB),
    )(hb, w_all)


def _outproj_ln_kernel(alpha, y_ref, w_ref, x_ref, gate_ref, g_ref, b_ref, sc_ref, sh_ref, wr_ref, br_ref,
                       o_ref, info_ref, cnt_ref, run_ref):
    @pl.when(pl.program_id(0) == 0)
    def _():
        run_ref[...] = jnp.zeros_like(run_ref)

    z = alpha * x_ref[...] + gate_ref[0] * _dot(y_ref[...], w_ref[...])
    x1 = _layer_norm(z, g_ref[0], b_ref[0])
    o_ref[...] = x1
    info, run = _route(x1 * (1.0 + sc_ref[0]) + sh_ref[0], wr_ref[...], br_ref[...], run_ref[...])
    info_ref[...] = info
    run_ref[...] = run
    cnt_ref[...] = run


def _outproj_ln(y, w, x, mods3, mod_row, layer, ln_g, ln_b, wr, br, alpha, tm=512):
    m, kdim = y.shape
    d = x.shape[1]
    base = layer * MOD_ROWS
    return pl.pallas_call(
        functools.partial(_outproj_ln_kernel, alpha),
        grid=(m // tm,),
        in_specs=[
            pl.BlockSpec((tm, kdim), lambda i: (i, 0)),
            pl.BlockSpec((kdim, d), lambda i: (0, 0)),
            pl.BlockSpec((tm, d), lambda i: (i, 0)),
            pl.BlockSpec((1, 1, d), lambda i: (base + mod_row(i, tm), 0, MIX_GATE)),
            pl.BlockSpec((1, 1, d), lambda i: (layer, 0, 0)),
            pl.BlockSpec((1, 1, d), lambda i: (layer, 0, 0)),
            pl.BlockSpec((1, 1, d), lambda i: (base + mod_row(i, tm), 0, MOE_SCALE)),
            pl.BlockSpec((1, 1, d), lambda i: (base + mod_row(i, tm), 0, MOE_SHIFT)),
            pl.BlockSpec((d, V7X_LANES), lambda i: (0, 0)),
            pl.BlockSpec((1, V7X_LANES), lambda i: (0, 0)),
        ],
        out_specs=[
            pl.BlockSpec((tm, d), lambda i: (i, 0)),
            pl.BlockSpec((tm, V7X_LANES), lambda i: (i, 0)),
            pl.BlockSpec((1, V7X_LANES), lambda i: (0, 0)),
        ],
        out_shape=[
            jax.ShapeDtypeStruct((m, d), F32),
            jax.ShapeDtypeStruct((m, V7X_LANES), F32),
            jax.ShapeDtypeStruct((1, V7X_LANES), F32),
        ],
        scratch_shapes=[pltpu.VMEM((1, V7X_LANES), F32)],
        compiler_params=_cparams(
            ("arbitrary",), kdim * d * 2 + 2 * tm * kdim * 2 + 9 * tm * d * 4 + d * V7X_LANES * 4 + 4 * MIB),
    )(y, w, x, mods3, ln_g, ln_b, mods3, mods3, wr, br)


def _rope(x, cos, sin):
    half = V7X_LANES // 2
    sw = jnp.concatenate(
        [pltpu.roll(x[:, :V7X_LANES], half, 1), pltpu.roll(x[:, V7X_LANES:], half, 1)], axis=1)
    return x * cos + sw * sin


def _group_norm(o, g, b):
    mu = jnp.mean(o, axis=-1, keepdims=True)
    oc = o - mu
    var = jnp.mean(oc * oc, axis=-1, keepdims=True)
    return oc * lax.rsqrt(var + LN_EPS) * g + b


def _ret_kernel(p_slabs, dk, lg_ref, q_ref, k_ref, v_ref, gf_ref, gb_ref, cos_ref, sin_ref, s0_ref,
                gng_ref, gnb_ref, y_ref, st_ref, q_s, k_s, yacc):
    h = pl.program_id(0)
    s = pl.program_id(1)
    is_prompt = s < p_slabs
    c = RET_CHUNK
    n_chunks = q_ref.shape[0] // c

    def run(prompt):
        ri = lax.broadcasted_iota(jnp.int32, (c, c), 0)
        ci = lax.broadcasted_iota(jnp.int32, (c, c), 1)
        diff = (ri - ci).astype(F32)
        pos = lax.broadcasted_iota(jnp.int32, (c, 1), 0).astype(F32)
        gn_g = gng_ref[0]
        gn_b = gnb_ref[0]
        scale = dk ** -0.5
        if not prompt:
            q_s[...] = _rope(q_ref[...], cos_ref[...], sin_ref[...])
            k_s[...] = _rope(k_ref[...], cos_ref[...], sin_ref[...]) * scale

        for direction in (0, 1):
            lg = jnp.full((1, 1), lg_ref[direction, h], F32)
            if direction == 0:
                mask = jnp.where(diff >= 0, jnp.exp(jnp.maximum(diff, 0.0) * lg), 0.0)
                q_dec = jnp.exp((pos + 1.0) * lg)
                k_dec = jnp.exp((c - 1.0 - pos) * lg)
                order = range(n_chunks)
                gate_ref = gf_ref
            else:
                mask = jnp.where(diff <= 0, jnp.exp(jnp.maximum(-diff, 0.0) * lg), 0.0)
                q_dec = jnp.exp((c - pos) * lg)
                k_dec = jnp.exp(pos * lg)
                order = range(n_chunks - 1, -1, -1)
                gate_ref = gb_ref
            c_dec = jnp.exp(c * lg)
            state = None if prompt else s0_ref[0, direction, 0]
            for ch in order:
                rows = pl.ds(ch * c, c)
                if prompt:
                    qc = q_ref[rows, :]
                    kc = k_ref[rows, :] * scale
                else:
                    qc = q_s[rows, :]
                    kc = k_s[rows, :]
                vb = v_ref[rows, :].astype(BF16)
                inner = _dot_nt(qc.astype(BF16), kc.astype(BF16)) * mask
                o = _dot(inner.astype(BF16), vb)
                kv = _dot(jnp.transpose(kc * k_dec).astype(BF16), vb)
                if prompt:
                    st_ref[ch, direction, 0] = kv
                else:
                    o = o + _dot((qc * q_dec).astype(BF16), state.astype(BF16))
                    state = c_dec * state + kv
                contrib = _group_norm(o, gn_g, gn_b) * _silu(gate_ref[rows, :])
                if direction == 0:
                    yacc[rows, :] = contrib
                else:
                    y_ref[rows, :] = (yacc[rows, :] + contrib).astype(BF16)

    @pl.when(is_prompt)
    def _():
        run(True)

    @pl.when(jnp.logical_not(is_prompt))
    def _():
        run(False)


def _retention(proj, log_g, rope_cos, rope_sin, state_ret, ret_idx, gn_g, gn_b, p_slabs, slab):
    m = proj.shape[0]
    heads, dk, dv = state_ret.shape[-3:]
    n_slabs = m // slab
    per_slab = slab // RET_CHUNK
    v_blk0 = 2 * heads * dk // dv
    p_last = p_slabs - 1

    def s0_map(h, s):
        return (jnp.maximum(s - p_slabs, 0), ret_idx, 0, h, 0, 0)

    kernel = functools.partial(_ret_kernel, p_slabs, dk)
    return pl.pallas_call(
        kernel,
        grid=(heads, n_slabs),
        in_specs=[
            pl.BlockSpec(memory_space=pltpu.SMEM),
            pl.BlockSpec((slab, dk), lambda h, s: (s, h)),
            pl.BlockSpec((slab, dk), lambda h, s: (s, heads + h)),
            pl.BlockSpec((slab, dv), lambda h, s: (s, v_blk0 + h)),
            pl.BlockSpec((slab, dv), lambda h, s: (s, v_blk0 + heads + h)),
            pl.BlockSpec((slab, dv), lambda h, s: (s, v_blk0 + 2 * heads + h)),
            pl.BlockSpec((slab, dk), lambda h, s: (0, 0)),
            pl.BlockSpec((slab, dk), lambda h, s: (0, 0)),
            pl.BlockSpec((1, None, 2, 1, dk, dv), s0_map),
            pl.BlockSpec((1, 1, dv), lambda h, s: (h, 0, 0)),
            pl.BlockSpec((1, 1, dv), lambda h, s: (h, 0, 0)),
        ],
        out_specs=[
            pl.BlockSpec((slab, dv), lambda h, s: (s, h)),
            pl.BlockSpec((per_slab, 2, 1, dk, dv), lambda h, s: (jnp.minimum(s, p_last), 0, h, 0, 0)),
        ],
        out_shape=[
            jax.ShapeDtypeStruct((m, heads * dv), BF16),
            jax.ShapeDtypeStruct((p_slabs * per_slab, 2, heads, dk, dv), F32),
        ],
        scratch_shapes=[
            pltpu.VMEM((slab, dk), F32),
            pltpu.VMEM((slab, dk), F32),
            pltpu.VMEM((slab, dv), F32),
        ],
        compiler_params=_cparams(
            ("arbitrary", "arbitrary"),
            2 * (2 * slab * dk + 3 * slab * dv) * 4 + 4 * slab * dk * 4 + 2 * 2 * dk * dv * 4
            + 2 * slab * dv * 2 + 2 * per_slab * 2 * dk * dv * 4 + 2 * slab * dk * 4 + slab * dv * 4
            + 8 * MIB),
    )(log_g, proj, proj, proj, proj, proj, rope_cos, rope_sin, state_ret, gn_g, gn_b)


def _na_window(rq, rows):
    kh = min(WIN_H, rows)
    return min(max(rq - kh // 2, 0), rows - kh), kh


def _attn_kernel(p_slabs, seq, dh, q_ref, k_ref, v_ref, ck_ref, cv_ref, strip_ref, o_ref, nk_ref, nv_ref):
    s = pl.program_id(1)
    is_prompt = s < p_slabs
    slab = q_ref.shape[0]
    scale = dh ** -0.5 * LOG2_E

    def prompt_head(hh):
        cols = pl.ds(hh * dh, dh)
        for b in range(slab // seq):
            rows = pl.ds(b * seq, seq)
            k = k_ref[rows, cols]
            v = v_ref[rows, cols]
            nk_ref[b, 0, hh] = k
            nv_ref[b, 0, hh] = v
            q = (q_ref[rows, cols] * scale).astype(BF16)
            sc = _dot_nt(q, k.astype(BF16))
            p = jnp.exp2(sc - jnp.max(sc, axis=-1, keepdims=True))
            denom = jnp.sum(p, axis=-1, keepdims=True)
            o_ref[rows, cols] = (_dot(p.astype(BF16), v.astype(BF16)) / denom).astype(BF16)

    def latent_head(hh):
        cols = pl.ds(hh * dh, dh)
        grid_rows = slab // GRID_W
        rows_per_block = ATTN_QB // GRID_W
        ckb = ck_ref[0, 0, hh].astype(BF16)
        cvb = cv_ref[0, 0, hh].astype(BF16)
        for qb in range(slab // ATTN_QB):
            rq0 = qb * rows_per_block
            ka = _na_window(rq0, grid_rows)[0] // 2 * 2
            last0, kh = _na_window(rq0 + rows_per_block - 1, grid_rows)
            kb = -((last0 + kh) // -2) * 2
            n_keys = (kb - ka) * GRID_W
            keys = pl.ds(ka * GRID_W, n_keys)
            pieces = []
            for i in range(rows_per_block):
                rq = rq0 + i
                r0, kh = _na_window(rq, grid_rows)
                first = ka - rq + WIN_H - 1
                piece = strip_ref[hh, first % 2, :, pl.ds((first - first % 2) * GRID_W, n_keys)]
                if r0 != ka or r0 + kh != kb:
                    key_row = ka + (lax.broadcasted_iota(jnp.int32, piece.shape, 1) // GRID_W)
                    piece = jnp.where((key_row >= r0) & (key_row < r0 + kh), piece, NEG_INF)
                pieces.append(piece)
            bias = jnp.concatenate(pieces, axis=0)
            rows = pl.ds(qb * ATTN_QB, ATTN_QB)
            q = (q_ref[rows, cols] * scale).astype(BF16)
            s_loc = _dot_nt(q, k_ref[keys, cols].astype(BF16)) + bias
            s_ctx = _dot_nt(q, ckb)
            mx = jnp.maximum(jnp.max(s_loc, axis=-1, keepdims=True), jnp.max(s_ctx, axis=-1, keepdims=True))
            p_loc = jnp.exp2(s_loc - mx)
            p_ctx = jnp.exp2(s_ctx - mx)
            denom = jnp.sum(p_loc, axis=-1, keepdims=True) + jnp.sum(p_ctx, axis=-1, keepdims=True)
            o = _dot(p_loc.astype(BF16), v_ref[keys, cols].astype(BF16)) + _dot(p_ctx.astype(BF16), cvb)
            o_ref[rows, cols] = (o / denom).astype(BF16)

    heads_per_step = q_ref.shape[1] // dh

    @pl.when(is_prompt)
    def _():
        for hh in range(heads_per_step):
            prompt_head(hh)

    @pl.when(jnp.logical_not(is_prompt))
    def _():
        for hh in range(heads_per_step):
            latent_head(hh)


def _attention(proj, cache_k, cache_v, cache_idx, strips, p_slabs, slab, seq):
    m = proj.shape[0]
    heads = cache_k.shape[2]
    dh = cache_k.shape[-1]
    past = cache_k.shape[-2]
    n_slabs = m // slab
    per_slab = slab // seq
    p_last = p_slabs - 1

    hp = ATTN_HEADS_PER_STEP
    groups = heads // hp

    def ctx_map(h, s):
        return (jnp.maximum(s - p_slabs, 0), cache_idx, h, 0, 0)

    def new_map(h, s):
        return (jnp.minimum(s, p_last), 0, h, 0, 0)

    kernel = functools.partial(_attn_kernel, p_slabs, seq, dh)
    return pl.pallas_call(
        kernel,
        grid=(groups, n_slabs),
        in_specs=[
            pl.BlockSpec((slab, hp * dh), lambda h, s: (s, h)),
            pl.BlockSpec((slab, hp * dh), lambda h, s: (s, groups + h)),
            pl.BlockSpec((slab, hp * dh), lambda h, s: (s, 2 * groups + h)),
            pl.BlockSpec((1, 1, hp, past, dh), ctx_map),
            pl.BlockSpec((1, 1, hp, past, dh), ctx_map),
            pl.BlockSpec((hp,) + strips.shape[1:], lambda h, s: (h, 0, 0, 0)),
        ],
        out_specs=[
            pl.BlockSpec((slab, hp * dh), lambda h, s: (s, h)),
            pl.BlockSpec((per_slab, 1, hp, seq, dh), new_map),
            pl.BlockSpec((per_slab, 1, hp, seq, dh), new_map),
        ],
        out_shape=[
            jax.ShapeDtypeStruct((m, heads * dh), BF16),
            jax.ShapeDtypeStruct((p_slabs * per_slab, 1, heads, seq, dh), F32),
            jax.ShapeDtypeStruct((p_slabs * per_slab, 1, heads, seq, dh), F32),
        ],
        compiler_params=_cparams(
            ("arbitrary", "arbitrary"),
            hp * (2 * strips.shape[1] * strips.shape[2] * strips.shape[3] * 4 + 16 * slab * dh * 4)
            + 10 * ATTN_QB * (slab + past) * 4 + 8 * MIB),
    )(proj, proj, proj, cache_k, cache_v, strips)


def _na_strips(rpb):
    heads, n_dr, n_dc = rpb.shape
    col = jnp.arange(GRID_W)
    c0 = jnp.clip(col - WIN_W // 2, 0, GRID_W - WIN_W)
    col_ok = (col[None, :] >= c0[:, None]) & (col[None, :] < c0[:, None] + WIN_W)
    dc_idx = jnp.clip(col[None, :] - col[:, None], -(WIN_W - 1), WIN_W - 1) + WIN_W - 1
    onehot = (dc_idx[:, :, None] == jnp.arange(n_dc)[None, None, :]).astype(F32)
    tiles = jnp.einsum("hdk,qck->hqdc", rpb.astype(F32), onehot, precision=lax.Precision.HIGHEST)
    tiles = jnp.where(col_ok[None, :, None, :], tiles * LOG2_E, NEG_INF)
    n_tiles = 2 * WIN_H
    neg = jnp.full((heads, GRID_W, n_tiles + 1 - n_dr, GRID_W), NEG_INF, F32)
    tiles = jnp.concatenate([tiles, neg], axis=2)
    both = jnp.stack([tiles[:, :, :n_tiles], tiles[:, :, 1:]], axis=1)
    return both.reshape(heads, 2, GRID_W, n_tiles * GRID_W)


def _route(hm, wr, br, run):
    h_hi = hm.astype(BF16)
    h_lo = (hm - h_hi.astype(F32)).astype(BF16)
    w_hi = wr.astype(BF16)
    w_lo = (wr - w_hi.astype(F32)).astype(BF16)
    logits = _dot(h_hi, w_hi) + (_dot(h_hi, w_lo) + _dot(h_lo, w_hi)) + br
    tm, width = logits.shape
    col = lax.broadcasted_iota(jnp.int32, (tm, width), 1).astype(F32)
    neg = jnp.float32(-3.0e38)

    gl = jnp.where(col < N_GROUPS, logits, neg)
    gmax = jnp.max(gl, axis=-1, keepdims=True)
    gsel = jnp.min(jnp.where(gl == gmax, col, float(width)), axis=-1, keepdims=True)
    p_g = 1.0 / jnp.sum(jnp.where(col < N_GROUPS, jnp.exp(gl - gmax), 0.0), axis=-1, keepdims=True)

    lo = N_GROUPS + N_EXP_PER_GROUP * gsel
    el = jnp.where((col >= lo) & (col < lo + N_EXP_PER_GROUP), logits, neg)
    e1 = jnp.max(el, axis=-1, keepdims=True)
    i1 = jnp.min(jnp.where(el == e1, col, float(width)), axis=-1, keepdims=True)
    el2 = jnp.where(col == i1, neg, el)
    e2 = jnp.max(el2, axis=-1, keepdims=True)
    i2 = jnp.min(jnp.where(el2 == e2, col, float(width)), axis=-1, keepdims=True)
    t = jnp.exp(e2 - e1)
    w1 = 1.0 / (1.0 + t)
    c1 = p_g * w1
    c2 = p_g * (t * w1)
    id1 = i1 - N_GROUPS
    id2 = i2 - N_GROUPS

    onehot = jnp.where((col == id1) | (col == id2), 1.0, 0.0)
    ri = lax.broadcasted_iota(jnp.int32, (tm, tm), 0)
    ci = lax.broadcasted_iota(jnp.int32, (tm, tm), 1)
    tri = jnp.where(ri > ci, 1.0, 0.0).astype(BF16)
    before = _dot(tri, onehot.astype(BF16)) + run
    r1 = jnp.sum(jnp.where(col == id1, before, 0.0), axis=-1, keepdims=True)
    r2 = jnp.sum(jnp.where(col == id2, before, 0.0), axis=-1, keepdims=True)

    info = jnp.where(col == 0, id1, 0.0)
    info = jnp.where(col == 1, id2, info)
    info = jnp.where(col == 2, c1, info)
    info = jnp.where(col == 3, c2, info)
    info = jnp.where(col == 4, r1, info)
    info = jnp.where(col == 5, r2, info)
    return info, run + jnp.sum(onehot, axis=0, keepdims=True)


def _pos_kernel(info_ref, off_ref, pos_ref):
    info = info_ref[...]
    tm, width = info.shape
    col = lax.broadcasted_iota(jnp.int32, (tm, width), 1).astype(F32)
    off = off_ref[...]
    p1 = jnp.sum(jnp.where(col == info[:, 0:1], off, 0.0), axis=-1, keepdims=True) + info[:, 4:5]
    p2 = jnp.sum(jnp.where(col == info[:, 1:2], off, 0.0), axis=-1, keepdims=True) + info[:, 5:6]
    pm = jnp.where(col == 0, p1, jnp.where(col == 1, p2, 0.0))
    pos_ref[0] = jnp.transpose(pm)[:SUBLANES, :].astype(jnp.int32)


def _positions(info, row_off, tm):
    m = info.shape[0]
    return pl.pallas_call(
        _pos_kernel,
        grid=(m // tm,),
        in_specs=[
            pl.BlockSpec((tm, V7X_LANES), lambda i: (i, 0)),
            pl.BlockSpec((1, V7X_LANES), lambda i: (0, 0)),
        ],
        out_specs=pl.BlockSpec((1, SUBLANES, tm), lambda i: (i, 0, 0)),
        out_shape=jax.ShapeDtypeStruct((m // tm, SUBLANES, tm), jnp.int32),
        compiler_params=_cparams(("arbitrary",), 16 * MIB),
    )(info, row_off)


def _dispatch_kernel(tm, pad_bits, ps_ref, pl_ref, pos_ref, x_ref, sc_ref, sh_ref, xs_hbm,
                     hbuf, zbuf, sems, zsem):
    i = pl.program_id(0)
    n_steps = pl.num_programs(0)
    n_experts = ps_ref.shape[0] - 1
    zrows = zbuf.shape[0]
    max_tail = (xs_hbm.shape[0] - 2 * x_ref.shape[0] * n_steps) // zrows

    def fill_copies(do):
        for e in range(n_experts):
            start = ps_ref[e]
            rem = pl_ref[e]
            for r in range(SUBLANES - 1):
                @pl.when(r < (rem & (SUBLANES - 1)))
                def _(start=start, r=r):
                    do(pltpu.make_async_copy(zbuf.at[pl.ds(0, 1)], xs_hbm.at[pl.ds(start + r, 1)], zsem))

            off = start + rem
            for bit in pad_bits:
                off = off - (rem & bit)

                @pl.when((rem & bit) != 0)
                def _(off=off, bit=bit):
                    dst = xs_hbm.at[pl.ds(pl.multiple_of(off, SUBLANES), bit)]
                    do(pltpu.make_async_copy(zbuf.at[pl.ds(0, bit)], dst, zsem))
        for t in range(max_tail):
            @pl.when(t < pl_ref[n_experts])
            def _(t=t):
                dst = xs_hbm.at[pl.ds(pl.multiple_of(ps_ref[n_experts] + t * zrows, SUBLANES), zrows)]
                do(pltpu.make_async_copy(zbuf, dst, zsem))

    @pl.when(i == 0)
    def _():
        zbuf[...] = jnp.zeros_like(zbuf)
        fill_copies(lambda cp: cp.start())

    def scatter_start(sub):
        for r in range(tm):
            for slot in range(2):
                pltpu.make_async_copy(hbuf.at[sub, pl.ds(r, 1)],
                                      xs_hbm.at[pl.ds(pos_ref[0, slot, sub * tm + r], 1)],
                                      sems.at[sub]).start(priority=slot)

    def scatter_wait(sub):
        for _ in range(2):
            pltpu.make_async_copy(hbuf.at[sub], xs_hbm.at[pl.ds(0, tm)], sems.at[sub]).wait()

    for sub in range(2):
        rows = pl.ds(sub * tm, tm)
        hbuf[sub] = _pack_halves(x_ref[rows, :] * (1.0 + sc_ref[0]) + sh_ref[0])
        scatter_start(sub)
        if sub == 0:
            @pl.when(i > 0)
            def _():
                scatter_wait(1)
        else:
            scatter_wait(0)

    @pl.when(i == n_steps - 1)
    def _():
        scatter_wait(1)
        fill_copies(lambda cp: cp.wait())


def _dispatch(x, posT, pad_start, pad_len, mods3, mod_row, layer, n_rows, tm, moe_tm):
    m, d = x.shape
    base = layer * MOD_ROWS
    step = 2 * tm
    pad_bits = tuple(1 << b for b in range(moe_tm.bit_length() - 2, SUBLANES.bit_length() - 2, -1))
    grid_spec = pltpu.PrefetchScalarGridSpec(
        num_scalar_prefetch=2,
        grid=(m // step,),
        in_specs=[
            pl.BlockSpec((1, SUBLANES, step), lambda i, ps, pn: (i, 0, 0), memory_space=pltpu.SMEM),
            pl.BlockSpec((step, d), lambda i, ps, pn: (i, 0)),
            pl.BlockSpec((1, 1, d), lambda i, ps, pn: (base + mod_row(i, step), 0, MOE_SCALE)),
            pl.BlockSpec((1, 1, d), lambda i, ps, pn: (base + mod_row(i, step), 0, MOE_SHIFT)),
        ],
        out_specs=pl.BlockSpec(memory_space=pl.ANY),
        scratch_shapes=[
            pltpu.VMEM((2, tm, d // 2), U32),
            pltpu.VMEM((moe_tm // 2, d // 2), U32),
            pltpu.SemaphoreType.DMA((2,)),
            pltpu.SemaphoreType.DMA(()),
        ],
    )
    return pl.pallas_call(
        functools.partial(_dispatch_kernel, tm, pad_bits),
        grid_spec=grid_spec,
        out_shape=jax.ShapeDtypeStruct((n_rows, d // 2), U32),
        compiler_params=_cparams(("arbitrary",), 2 * step * d * 4 + 3 * tm * d * 4 + 8 * MIB),
    )(pad_start, pad_len, posT, x, mods3, mods3)


def _moe_kernel(layer, te_ref, nu_ref, nxt_ref, par_ref, xs_ref, wg_hbm, wu_hbm, wd_hbm, ys_ref,
                wg_f, wu_f, wd_f, wg_b, wu_b, wd_b, sems):
    i = pl.program_id(0)
    used = i < nu_ref[0]
    expert = te_ref[i]
    first = used & ((i == 0) | (expert != te_ref[jnp.maximum(i - 1, 0)]))

    def weight_copies(e, slot):
        return (pltpu.make_async_copy(wg_hbm.at[layer, e], wg_f.at[slot], sems.at[slot]),
                pltpu.make_async_copy(wu_hbm.at[layer, e], wu_f.at[slot], sems.at[slot]),
                pltpu.make_async_copy(wd_hbm.at[layer, e], wd_f.at[slot], sems.at[slot]))

    @pl.when(jnp.logical_not(used))
    def _():
        ys_ref[...] = jnp.zeros_like(ys_ref)

    @pl.when(i == 0)
    def _():
        for cp in weight_copies(expert, par_ref[0]):
            cp.start(priority=1)

    def mlp(recast_slot):
        x = jnp.concatenate(_unpack_halves(xs_ref[...]), axis=1).astype(BF16)
        if recast_slot is not None:
            wg_b[...] = wg_f[recast_slot].astype(BF16)
        g = _dot(x, wg_b[...])
        if recast_slot is not None:
            wu_b[...] = wu_f[recast_slot].astype(BF16)
        u = _dot(x, wu_b[...])
        if recast_slot is not None:
            wd_b[...] = wd_f[recast_slot].astype(BF16)
        hid = (_silu(g) * u).astype(BF16)
        ys_ref[...] = _pack_halves(_dot(hid, wd_b[...]))

    @pl.when(first)
    def _():
        slot = par_ref[i]
        for cp in weight_copies(expert, slot):
            cp.wait()

        @pl.when(nxt_ref[i] >= 0)
        def _():
            for cp in weight_copies(nxt_ref[i], 1 - slot):
                cp.start(priority=1)

        mlp(slot)

    @pl.when(used & jnp.logical_not(first))
    def _():
        mlp(None)


def _moe_experts(xs, tile_expert, n_used, next_expert, slot, w_gate, w_up, w_down, layer, tm):
    n_rows, dp = xs.shape
    d = 2 * dp
    f = w_gate.shape[-1]
    n_tiles = n_rows // tm
    grid_spec = pltpu.PrefetchScalarGridSpec(
        num_scalar_prefetch=4,
        grid=(n_tiles,),
        in_specs=[
            pl.BlockSpec((tm, dp), lambda i, te, nu, nx, pr: (jnp.maximum(jnp.minimum(i, nu[0] - 1), 0), 0)),
            pl.BlockSpec(memory_space=pl.ANY),
            pl.BlockSpec(memory_space=pl.ANY),
            pl.BlockSpec(memory_space=pl.ANY),
        ],
        out_specs=pl.BlockSpec((tm, dp), lambda i, te, nu, nx, pr: (i, 0)),
        scratch_shapes=[
            pltpu.VMEM((2, d, f), F32), pltpu.VMEM((2, d, f), F32), pltpu.VMEM((2, f, d), F32),
            pltpu.VMEM((d, f), BF16), pltpu.VMEM((d, f), BF16), pltpu.VMEM((f, d), BF16),
            pltpu.SemaphoreType.DMA((2,)),
        ],
    )
    return pl.pallas_call(
        functools.partial(_moe_kernel, layer),
        grid_spec=grid_spec,
        out_shape=jax.ShapeDtypeStruct((n_rows, dp), U32),
        compiler_params=_cparams(("arbitrary",), 6 * d * f * 4 + 3 * d * f * 2 + 8 * tm * d * 4 + 8 * MIB),
    )(tile_expert, n_used, next_expert, slot, xs, w_gate, w_up, w_down)


def _combine_ln_kernel(alpha, tm, p_steps, pos_ref, nxt_ref, ys_hbm, info_ref, x_ref, gate_ref, g_ref, b_ref,
                       *rest):
    i = pl.program_id(0)
    n_steps = pl.num_programs(0)
    if p_steps is None:
        nsc_ref, nsh_ref, o_ref, hb_ref, buf, sems = rest
    else:
        op_ref, os_ref, buf, sems = rest

    def gather_start(sub, p_ref):
        for r in range(tm):
            for slot in range(2):
                pltpu.make_async_copy(ys_hbm.at[pl.ds(p_ref[0, slot, sub * tm + r], 1)],
                                      buf.at[sub, slot, pl.ds(r, 1)], sems.at[sub]).start(priority=slot)

    def gather_wait(sub):
        for slot in range(2):
            pltpu.make_async_copy(ys_hbm.at[pl.ds(0, tm)], buf.at[sub, slot], sems.at[sub]).wait()

    def finish(sub):
        rows = pl.ds(sub * tm, tm)
        info = info_ref[rows, :]
        lo1, hi1 = _unpack_halves(buf[sub, 0])
        lo2, hi2 = _unpack_halves(buf[sub, 1])
        c1 = info[:, 2:3]
        c2 = info[:, 3:4]
        y = jnp.concatenate([c1 * lo1 + c2 * lo2, c1 * hi1 + c2 * hi2], axis=1)
        z = alpha * x_ref[rows, :] + gate_ref[0] * y
        res = _layer_norm(z, g_ref[0], b_ref[0])
        if p_steps is None:
            o_ref[rows, :] = res
            hb_ref[rows, :] = (res * (1.0 + nsc_ref[0]) + nsh_ref[0]).astype(BF16)
        else:
            @pl.when(i < p_steps)
            def _():
                op_ref[rows, :] = res

            @pl.when(i >= p_steps)
            def _():
                os_ref[rows, :] = res

    @pl.when(i == 0)
    def _():
        gather_start(0, pos_ref)

    gather_start(1, pos_ref)
    gather_wait(0)
    finish(0)

    @pl.when(i + 1 < n_steps)
    def _():
        gather_start(0, nxt_ref)

    gather_wait(1)
    finish(1)


def _combine_ln(ys, posT, info, x, mods3, mod_row, layer, ln_g, ln_b, alpha, tm, last, p_rows):
    m, d = x.shape
    step = 2 * tm
    n_steps = m // step
    base = layer * MOD_ROWS
    in_specs = [
        pl.BlockSpec((1, SUBLANES, step), lambda i: (i, 0, 0), memory_space=pltpu.SMEM),
        pl.BlockSpec((1, SUBLANES, step), lambda i: (jnp.minimum(i + 1, n_steps - 1), 0, 0), memory_space=pltpu.SMEM),
        pl.BlockSpec(memory_space=pl.ANY),
        pl.BlockSpec((step, V7X_LANES), lambda i: (i, 0)),
        pl.BlockSpec((step, d), lambda i: (i, 0)),
        pl.BlockSpec((1, 1, d), lambda i: (base + mod_row(i, step), 0, MOE_GATE)),
        pl.BlockSpec((1, 1, d), lambda i: (layer, 0, 0)),
        pl.BlockSpec((1, 1, d), lambda i: (layer, 0, 0)),
    ]
    args = [posT, posT, ys, info, x, mods3, ln_g, ln_b]
    if last:
        p_steps = p_rows // step
        out_specs = [
            pl.BlockSpec((step, d), lambda i: (jnp.minimum(i, p_steps - 1), 0)),
            pl.BlockSpec((step, d), lambda i: (jnp.maximum(i - p_steps, 0), 0)),
        ]
        out_shape = [jax.ShapeDtypeStruct((p_rows, d), F32), jax.ShapeDtypeStruct((m - p_rows, d), F32)]
    else:
        p_steps = None
        nbase = (layer + 1) * MOD_ROWS
        in_specs += [
            pl.BlockSpec((1, 1, d), lambda i: (nbase + mod_row(i, step), 0, MIX_SCALE)),
            pl.BlockSpec((1, 1, d), lambda i: (nbase + mod_row(i, step), 0, MIX_SHIFT)),
        ]
        args += [mods3, mods3]
        out_specs = [pl.BlockSpec((step, d), lambda i: (i, 0)), pl.BlockSpec((step, d), lambda i: (i, 0))]
        out_shape = [jax.ShapeDtypeStruct((m, d), F32), jax.ShapeDtypeStruct((m, d), BF16)]
    return pl.pallas_call(
        functools.partial(_combine_ln_kernel, alpha, tm, p_steps),
        grid=(n_steps,),
        in_specs=in_specs,
        out_specs=out_specs,
        out_shape=out_shape,
        scratch_shapes=[pltpu.VMEM((2, 2, tm, d // 2), U32), pltpu.SemaphoreType.DMA((2,))],
        compiler_params=_cparams(("arbitrary",), 6 * step * d * 4 + 4 * tm * d * 4 + 6 * tm * d * 4 + 8 * MIB),
    )(*args)


def _router_params(layer, w_rg, b_rg, w_re, b_re):
    d = w_rg.shape[1]
    n_e = N_EXPERTS
    pad = V7X_LANES - N_GROUPS - n_e
    wr = jnp.concatenate(
        [w_rg[layer], w_re[layer].transpose(1, 0, 2).reshape(d, n_e), jnp.zeros((d, pad), F32)], axis=1)
    br = jnp.concatenate([b_rg[layer], b_re[layer].reshape(n_e), jnp.zeros((pad,), F32)])[None, :]
    return wr, br


def _hier_moe_ln(x, info, cnt, mods3, mod_row, layer, w_gate, w_up, w_down, ln_g, ln_b, alpha, last, p_rows):
    m, d = x.shape
    n_e = N_EXPERTS

    tm = MOE_TM
    n_tiles = 2 * m // tm + n_e
    counts = cnt[0, :n_e].astype(jnp.int32)
    tiles_e = (counts + tm - 1) // tm
    tile_end = jnp.cumsum(tiles_e)
    n_used = tile_end[-1]
    row_off = (tile_end - tiles_e) * tm
    tile_ids = jnp.minimum(jnp.arange(n_tiles, dtype=jnp.int32), n_used - 1)
    tile_expert = jnp.sum((tile_ids[:, None] >= tile_end[None, :]).astype(jnp.int32), axis=1)
    group_end = tile_end[tile_expert]
    next_expert = jnp.where(group_end < n_used, tile_expert[jnp.minimum(group_end, n_tiles - 1)], -1)
    is_first = jnp.concatenate([jnp.ones((1,), jnp.int32), (tile_expert[1:] != tile_expert[:-1]).astype(jnp.int32)])
    slot = (jnp.cumsum(is_first) - 1) % 2
    zrows = tm // 2
    pad_start = jnp.concatenate([row_off + counts, (n_used * tm)[None]])
    pad_len = jnp.concatenate([tiles_e * tm - counts, ((n_tiles - n_used) * (tm // zrows))[None]])
    row_off_f = jnp.concatenate([row_off.astype(F32), jnp.zeros((V7X_LANES - n_e,), F32)])[None, :]

    ct = COMB_TM
    posT = _positions(info, row_off_f, 2 * ct)
    xs = _dispatch(x, posT, pad_start, pad_len, mods3, mod_row, layer, n_tiles * tm, ct, tm)
    ys = _moe_experts(xs, tile_expert, n_used.reshape(1), next_expert.astype(jnp.int32), slot.astype(jnp.int32),
                      w_gate, w_up, w_down, layer, tm)
    return _combine_ln(ys, posT, info, x, mods3, mod_row, layer, ln_g, ln_b, alpha, ct, last, p_rows)


def kernel(x_prompt, x_sample, state_ret, cache_na_k, cache_na_v, c, c_ctx, w_mod, b_mod, ln1_g, ln1_b, ln2_g, ln2_b, w_ret_in, ret_decay_logit, ret_gn_g, ret_gn_b, w_ret_out, w_na_in, na_rpb, w_na_out, w_rg, b_rg, w_re, b_re, w_gate, w_up, w_down):
    bp, seq, d = x_prompt.shape
    bs, dec_seq, _ = x_sample.shape
    depth = w_mod.shape[0]
    slab = dec_seq
    assert slab % seq == 0 and (bp * seq) % slab == 0 and bs + 1 <= MOD_ROWS
    p_rows = bp * seq
    p_slabs = p_rows // slab
    alpha = (2.0 * depth) ** 0.25

    def mod_row(i, tm):
        start = i * tm
        return jnp.where(start < p_rows, 0, 1 + (start - p_rows) // dec_seq)

    cc = jnp.concatenate([c_ctx[None, :], c, jnp.zeros((MOD_ROWS - 1 - bs, d), F32)], axis=0)
    mods = _modulation(cc, w_mod, b_mod)
    mods3 = mods.reshape(depth * MOD_ROWS, 1, 6 * d)
    x, hb = _prologue(x_prompt.reshape(p_rows, d), x_sample.reshape(bs * dec_seq, d), mods3, mod_row)

    ln1_g3, ln1_b3 = ln1_g.reshape(depth, 1, d), ln1_b.reshape(depth, 1, d)
    ln2_g3, ln2_b3 = ln2_g.reshape(depth, 1, d), ln2_b.reshape(depth, 1, d)

    new_ret, new_k, new_v = [], [], []
    for l in range(depth):
        j = l // 2
        if l % 2 == 0:
            dk = state_ret.shape[-2]
            proj = _inproj(hb, w_ret_in, j)
            log_g = -jax.nn.softplus(-ret_decay_logit[j].astype(F32))
            t = jnp.arange(slab)
            nf = dk // 4
            inv_freq = ROPE_BASE ** (-jnp.arange(nf, dtype=F32) / nf)
            ang_r = (t // GRID_W).astype(F32)[:, None] * inv_freq
            ang_c = (t % GRID_W).astype(F32)[:, None] * inv_freq
            cos = jnp.concatenate([jnp.cos(ang_r)] * 2 + [jnp.cos(ang_c)] * 2, axis=1)
            sin = jnp.concatenate([-jnp.sin(ang_r), jnp.sin(ang_r), -jnp.sin(ang_c), jnp.sin(ang_c)], axis=1)
            y, st = _retention(proj, log_g, cos, sin, state_ret, j, ret_gn_g[j][:, None, :],
                               ret_gn_b[j][:, None, :], p_slabs, slab)
            new_ret.append(st)
            mixed, w_out = y, w_ret_out[j]
        else:
            proj = _inproj(hb, w_na_in, j)
            o, nk, nv = _attention(proj, cache_na_k, cache_na_v, j, _na_strips(na_rpb[j]), p_slabs, slab, seq)
            new_k.append(nk[:, 0])
            new_v.append(nv[:, 0])
            mixed, w_out = o, w_na_out[j]
        wr, br = _router_params(l, w_rg, b_rg, w_re, b_re)
        x, info, cnt = _outproj_ln(mixed, w_out.astype(BF16), x, mods3, mod_row, l, ln1_g3, ln1_b3, wr, br, alpha)
        x, hb = _hier_moe_ln(x, info, cnt, mods3, mod_row, l, w_gate, w_up, w_down, ln2_g3, ln2_b3, alpha,
                             l == depth - 1, p_rows)

    y_prompt = x.reshape(bp, seq, d)
    y_sample = hb.reshape(bs, dec_seq, d)
    return (y_prompt, y_sample, jnp.stack(new_ret, axis=1), jnp.stack(new_k, axis=1), jnp.stack(new_v, axis=1))
```

```python
import functools

import jax
import jax.numpy as jnp
from jax import lax
from jax.experimental import pallas as pl
from jax.experimental.pallas import tpu as pltpu

F32 = jnp.float32
BF16 = jnp.bfloat16
U32 = jnp.uint32
MIB = 1024 * 1024

GRID_W = 64
WIN_H = 8
WIN_W = 16
N_GROUPS = 4
N_EXP_PER_GROUP = 8
N_EXPERTS = N_GROUPS * N_EXP_PER_GROUP
ROPE_BASE = 10000.0
LN_EPS = 1e-5
NEG_INF = -1e30
LOG2_E = 1.4426950408889634

V7X_VMEM_BYTES = 64 * MIB
V7X_LANES = 128
SUBLANES = 8
MOD_ROWS = 8
MIX_SHIFT, MIX_SCALE, MIX_GATE, MOE_SHIFT, MOE_SCALE, MOE_GATE = range(6)

RET_CHUNK = 256
ATTN_QB = 256
ATTN_HEADS_PER_STEP = 4
MOE_TM = 256
COMB_TM = 256


def _cparams(sem, vmem_bytes):
    limit = int(min(max(vmem_bytes, 16 * MIB), V7X_VMEM_BYTES - 6 * MIB))
    return pltpu.CompilerParams(dimension_semantics=sem, vmem_limit_bytes=limit)


def _silu(x):
    return x / (1.0 + jnp.exp(-x))


def _pack_halves(x):
    n = x.shape[1] // 2
    return pltpu.pack_elementwise([x[:, :n], x[:, n:]], packed_dtype=BF16)


def _unpack_halves(p):
    return (pltpu.unpack_elementwise(p, index=0, packed_dtype=BF16, unpacked_dtype=F32),
            pltpu.unpack_elementwise(p, index=1, packed_dtype=BF16, unpacked_dtype=F32))


def _dot(a, b):
    return jnp.dot(a, b, preferred_element_type=F32)


def _dot_nt(a, b):
    return lax.dot_general(a, b, (((1,), (1,)), ((), ())), preferred_element_type=F32)


def _layer_norm(z, g, b):
    mu = jnp.mean(z, axis=-1, keepdims=True)
    zc = z - mu
    var = jnp.mean(zc * zc, axis=-1, keepdims=True)
    return zc * lax.rsqrt(var + LN_EPS) * g + b


def _mod_kernel(c_ref, w_ref, b_ref, o_ref):
    s = _silu(c_ref[...]).astype(BF16)
    o_ref[0] = _dot(s, w_ref[0].astype(BF16)) + b_ref[0]


def _modulation(cc, w_mod, b_mod):
    depth, d, n6 = w_mod.shape
    tn = 1024
    return pl.pallas_call(
        _mod_kernel,
        grid=(depth, n6 // tn),
        in_specs=[
            pl.BlockSpec((MOD_ROWS, d), lambda l, j: (0, 0)),
            pl.BlockSpec((1, d, tn), lambda l, j: (l, 0, j)),
            pl.BlockSpec((1, 1, tn), lambda l, j: (l, 0, j)),
        ],
        out_specs=pl.BlockSpec((1, MOD_ROWS, tn), lambda l, j: (l, 0, j)),
        out_shape=jax.ShapeDtypeStruct((depth, MOD_ROWS, n6), F32),
        compiler_params=_cparams(("arbitrary", "arbitrary"), 3 * d * tn * 4 + 4 * MIB),
    )(cc, w_mod, b_mod.reshape(depth, 1, n6))


def _prologue_kernel(p_tiles, xp_ref, xs_ref, sc_ref, sh_ref, x_ref, hb_ref):
    @pl.when(pl.program_id(0) < p_tiles)
    def _():
        x = xp_ref[...]
        x_ref[...] = x
        hb_ref[...] = (x * (1.0 + sc_ref[0]) + sh_ref[0]).astype(BF16)

    @pl.when(pl.program_id(0) >= p_tiles)
    def _():
        x = xs_ref[...]
        x_ref[...] = x
        hb_ref[...] = (x * (1.0 + sc_ref[0]) + sh_ref[0]).astype(BF16)


def _prologue(xp, xs, mods3, mod_row, tm=512):
    p_rows, d = xp.shape
    m = p_rows + xs.shape[0]
    p_tiles = p_rows // tm
    return pl.pallas_call(
        functools.partial(_prologue_kernel, p_tiles),
        grid=(m // tm,),
        in_specs=[
            pl.BlockSpec((tm, d), lambda i: (jnp.minimum(i, p_tiles - 1), 0)),
            pl.BlockSpec((tm, d), lambda i: (jnp.maximum(i - p_tiles, 0), 0)),
            pl.BlockSpec((1, 1, d), lambda i: (mod_row(i, tm), 0, MIX_SCALE)),
            pl.BlockSpec((1, 1, d), lambda i: (mod_row(i, tm), 0, MIX_SHIFT)),
        ],
        out_specs=[pl.BlockSpec((tm, d), lambda i: (i, 0)), pl.BlockSpec((tm, d), lambda i: (i, 0))],
        out_shape=[jax.ShapeDtypeStruct((m, d), F32), jax.ShapeDtypeStruct((m, d), BF16)],
        compiler_params=_cparams(("arbitrary",), 8 * tm * d * 4 + 8 * MIB),
    )(xp, xs, mods3, mods3)


def _inproj_kernel(h_ref, w_ref, o_ref):
    o_ref[...] = _dot(h_ref[...], w_ref[...].astype(BF16))


def _inproj(hb, w_all, w_idx, tm=2048, tn=512):
    m, d = hb.shape
    n = w_all.shape[-1]
    return pl.pallas_call(
        _inproj_kernel,
        grid=(m // tm, n // tn),
        in_specs=[
            pl.BlockSpec((tm, d), lambda i, j: (i, 0)),
            pl.BlockSpec((None, d, tn), lambda i, j: (w_idx, 0, j)),
        ],
        out_specs=pl.BlockSpec((tm, tn), lambda i, j: (i, j)),
        out_shape=jax.ShapeDtypeStruct((m, n), F32),
        compiler_params=_cparams(
            ("arbitrary", "arbitrary"),
            2 * tm * d * 2 + 2 * d * tn * 4 + d * tn * 2 + 3 * tm * tn * 4 + 4 * MIB),
    )(hb, w_all)


def _outproj_ln_kernel(alpha, y_ref, w_ref, x_ref, gate_ref, g_ref, b_ref, sc_ref, sh_ref, wr_ref, br_ref,
                       o_ref, info_ref, cnt_ref, run_ref):
    @pl.when(pl.program_id(0) == 0)
    def _():
        run_ref[...] = jnp.zeros_like(run_ref)

    z = alpha * x_ref[...] + gate_ref[0] * _dot(y_ref[...], w_ref[...])
    x1 = _layer_norm(z, g_ref[0], b_ref[0])
    o_ref[...] = x1
    info, run = _route(x1 * (1.0 + sc_ref[0]) + sh_ref[0], wr_ref[...], br_ref[...], run_ref[...])
    info_ref[...] = info
    run_ref[...] = run
    cnt_ref[...] = run


def _outproj_ln(y, w, x, mods3, mod_row, layer, ln_g, ln_b, wr, br, alpha, tm=512):
    m, kdim = y.shape
    d = x.shape[1]
    base = layer * MOD_ROWS
    return pl.pallas_call(
        functools.partial(_outproj_ln_kernel, alpha),
        grid=(m // tm,),
        in_specs=[
            pl.BlockSpec((tm, kdim), lambda i: (i, 0)),
            pl.BlockSpec((kdim, d), lambda i: (0, 0)),
            pl.BlockSpec((tm, d), lambda i: (i, 0)),
            pl.BlockSpec((1, 1, d), lambda i: (base + mod_row(i, tm), 0, MIX_GATE)),
            pl.BlockSpec((1, 1, d), lambda i: (layer, 0, 0)),
            pl.BlockSpec((1, 1, d), lambda i: (layer, 0, 0)),
            pl.BlockSpec((1, 1, d), lambda i: (base + mod_row(i, tm), 0, MOE_SCALE)),
            pl.BlockSpec((1, 1, d), lambda i: (base + mod_row(i, tm), 0, MOE_SHIFT)),
            pl.BlockSpec((d, V7X_LANES), lambda i: (0, 0)),
            pl.BlockSpec((1, V7X_LANES), lambda i: (0, 0)),
        ],
        out_specs=[
            pl.BlockSpec((tm, d), lambda i: (i, 0)),
            pl.BlockSpec((tm, V7X_LANES), lambda i: (i, 0)),
            pl.BlockSpec((1, V7X_LANES), lambda i: (0, 0)),
        ],
        out_shape=[
            jax.ShapeDtypeStruct((m, d), F32),
            jax.ShapeDtypeStruct((m, V7X_LANES), F32),
            jax.ShapeDtypeStruct((1, V7X_LANES), F32),
        ],
        scratch_shapes=[pltpu.VMEM((1, V7X_LANES), F32)],
        compiler_params=_cparams(
            ("arbitrary",), kdim * d * 2 + 2 * tm * kdim * 2 + 9 * tm * d * 4 + d * V7X_LANES * 4 + 4 * MIB),
    )(y, w, x, mods3, ln_g, ln_b, mods3, mods3, wr, br)


def _rope(x, cos, sin):
    half = V7X_LANES // 2
    sw = jnp.concatenate(
        [pltpu.roll(x[:, :V7X_LANES], half, 1), pltpu.roll(x[:, V7X_LANES:], half, 1)], axis=1)
    return x * cos + sw * sin


def _group_norm(o, g, b):
    mu = jnp.mean(o, axis=-1, keepdims=True)
    oc = o - mu
    var = jnp.mean(oc * oc, axis=-1, keepdims=True)
    return oc * lax.rsqrt(var + LN_EPS) * g + b


def _ret_kernel(p_slabs, dk, lg_ref, q_ref, k_ref, v_ref, gf_ref, gb_ref, cos_ref, sin_ref, s0_ref,
                gng_ref, gnb_ref, y_ref, st_ref, q_s, k_s, yacc):
    h = pl.program_id(0)
    s = pl.program_id(1)
    is_prompt = s < p_slabs
    c = RET_CHUNK
    n_chunks = q_ref.shape[0] // c

    def run(prompt):
        ri = lax.broadcasted_iota(jnp.int32, (c, c), 0)
        ci = lax.broadcasted_iota(jnp.int32, (c, c), 1)
        diff = (ri - ci).astype(F32)
        pos = lax.broadcasted_iota(jnp.int32, (c, 1), 0).astype(F32)
        gn_g = gng_ref[0]
        gn_b = gnb_ref[0]
        scale = dk ** -0.5
        if not prompt:
            q_s[...] = _rope(q_ref[...], cos_ref[...], sin_ref[...])
            k_s[...] = _rope(k_ref[...], cos_ref[...], sin_ref[...]) * scale

        for direction in (0, 1):
            lg = jnp.full((1, 1), lg_ref[direction, h], F32)
            if direction == 0:
                mask = jnp.where(diff >= 0, jnp.exp(jnp.maximum(diff, 0.0) * lg), 0.0)
                q_dec = jnp.exp((pos + 1.0) * lg)
                k_dec = jnp.exp((c - 1.0 - pos) * lg)
                order = range(n_chunks)
                gate_ref = gf_ref
            else:
                mask = jnp.where(diff <= 0, jnp.exp(jnp.maximum(-diff, 0.0) * lg), 0.0)
                q_dec = jnp.exp((c - pos) * lg)
                k_dec = jnp.exp(pos * lg)
                order = range(n_chunks - 1, -1, -1)
                gate_ref = gb_ref
            c_dec = jnp.exp(c * lg)
            state = None if prompt else s0_ref[0, direction, 0]
            for ch in order:
                rows = pl.ds(ch * c, c)
                if prompt:
                    qc = q_ref[rows, :]
                    kc = k_ref[rows, :] * scale
                else:
                    qc = q_s[rows, :]
                    kc = k_s[rows, :]
                vb = v_ref[rows, :].astype(BF16)
                inner = _dot_nt(qc.astype(BF16), kc.astype(BF16)) * mask
                o = _dot(inner.astype(BF16), vb)
                kv = _dot(jnp.transpose(kc * k_dec).astype(BF16), vb)
                if prompt:
                    st_ref[ch, direction, 0] = kv
                else:
                    o = o + _dot((qc * q_dec).astype(BF16), state.astype(BF16))
                    state = c_dec * state + kv
                contrib = _group_norm(o, gn_g, gn_b) * _silu(gate_ref[rows, :])
                if direction == 0:
                    yacc[rows, :] = contrib
                else:
                    y_ref[rows, :] = (yacc[rows, :] + contrib).astype(BF16)

    @pl.when(is_prompt)
    def _():
        run(True)

    @pl.when(jnp.logical_not(is_prompt))
    def _():
        run(False)


def _retention(proj, log_g, rope_cos, rope_sin, state_ret, ret_idx, gn_g, gn_b, p_slabs, slab):
    m = proj.shape[0]
    heads, dk, dv = state_ret.shape[-3:]
    n_slabs = m // slab
    per_slab = slab // RET_CHUNK
    v_blk0 = 2 * heads * dk // dv
    p_last = p_slabs - 1

    def s0_map(h, s):
        return (jnp.maximum(s - p_slabs, 0), ret_idx, 0, h, 0, 0)

    kernel = functools.partial(_ret_kernel, p_slabs, dk)
    return pl.pallas_call(
        kernel,
        grid=(heads, n_slabs),
        in_specs=[
            pl.BlockSpec(memory_space=pltpu.SMEM),
            pl.BlockSpec((slab, dk), lambda h, s: (s, h)),
            pl.BlockSpec((slab, dk), lambda h, s: (s, heads + h)),
            pl.BlockSpec((slab, dv), lambda h, s: (s, v_blk0 + h)),
            pl.BlockSpec((slab, dv), lambda h, s: (s, v_blk0 + heads + h)),
            pl.BlockSpec((slab, dv), lambda h, s: (s, v_blk0 + 2 * heads + h)),
            pl.BlockSpec((slab, dk), lambda h, s: (0, 0)),
            pl.BlockSpec((slab, dk), lambda h, s: (0, 0)),
            pl.BlockSpec((1, None, 2, 1, dk, dv), s0_map),
            pl.BlockSpec((1, 1, dv), lambda h, s: (h, 0, 0)),
            pl.BlockSpec((1, 1, dv), lambda h, s: (h, 0, 0)),
        ],
        out_specs=[
            pl.BlockSpec((slab, dv), lambda h, s: (s, h)),
            pl.BlockSpec((per_slab, 2, 1, dk, dv), lambda h, s: (jnp.minimum(s, p_last), 0, h, 0, 0)),
        ],
        out_shape=[
            jax.ShapeDtypeStruct((m, heads * dv), BF16),
            jax.ShapeDtypeStruct((p_slabs * per_slab, 2, heads, dk, dv), F32),
        ],
        scratch_shapes=[
            pltpu.VMEM((slab, dk), F32),
            pltpu.VMEM((slab, dk), F32),
            pltpu.VMEM((slab, dv), F32),
        ],
        compiler_params=_cparams(
            ("arbitrary", "arbitrary"),
            2 * (2 * slab * dk + 3 * slab * dv) * 4 + 4 * slab * dk * 4 + 2 * 2 * dk * dv * 4
            + 2 * slab * dv * 2 + 2 * per_slab * 2 * dk * dv * 4 + 2 * slab * dk * 4 + slab * dv * 4
            + 8 * MIB),
    )(log_g, proj, proj, proj, proj, proj, rope_cos, rope_sin, state_ret, gn_g, gn_b)


def _na_window(rq, rows):
    kh = min(WIN_H, rows)
    return min(max(rq - kh // 2, 0), rows - kh), kh


def _attn_kernel(p_slabs, seq, dh, q_ref, k_ref, v_ref, ck_ref, cv_ref, strip_ref, o_ref, nk_ref, nv_ref):
    s = pl.program_id(1)
    is_prompt = s < p_slabs
    slab = q_ref.shape[0]
    scale = dh ** -0.5 * LOG2_E

    def prompt_head(hh):
        cols = pl.ds(hh * dh, dh)
        for b in range(slab // seq):
            rows = pl.ds(b * seq, seq)
            k = k_ref[rows, cols]
            v = v_ref[rows, cols]
            nk_ref[b, 0, hh] = k
            nv_ref[b, 0, hh] = v
            q = (q_ref[rows, cols] * scale).astype(BF16)
            sc = _dot_nt(q, k.astype(BF16))
            p = jnp.exp2(sc - jnp.max(sc, axis=-1, keepdims=True))
            denom = jnp.sum(p, axis=-1, keepdims=True)
            o_ref[rows, cols] = (_dot(p.astype(BF16), v.astype(BF16)) / denom).astype(BF16)

    def latent_head(hh):
        cols = pl.ds(hh * dh, dh)
        grid_rows = slab // GRID_W
        rows_per_block = ATTN_QB // GRID_W
        ckb = ck_ref[0, 0, hh].astype(BF16)
        cvb = cv_ref[0, 0, hh].astype(BF16)
        for qb in range(slab // ATTN_QB):
            rq0 = qb * rows_per_block
            ka = _na_window(rq0, grid_rows)[0] // 2 * 2
            last0, kh = _na_window(rq0 + rows_per_block - 1, grid_rows)
            kb = -((last0 + kh) // -2) * 2
            n_keys = (kb - ka) * GRID_W
            keys = pl.ds(ka * GRID_W, n_keys)
            pieces = []
            for i in range(rows_per_block):
                rq = rq0 + i
                r0, kh = _na_window(rq, grid_rows)
                first = ka - rq + WIN_H - 1
                piece = strip_ref[hh, first % 2, :, pl.ds((first - first % 2) * GRID_W, n_keys)]
                if r0 != ka or r0 + kh != kb:
                    key_row = ka + (lax.broadcasted_iota(jnp.int32, piece.shape, 1) // GRID_W)
                    piece = jnp.where((key_row >= r0) & (key_row < r0 + kh), piece, NEG_INF)
                pieces.append(piece)
            bias = jnp.concatenate(pieces, axis=0)
            rows = pl.ds(qb * ATTN_QB, ATTN_QB)
            q = (q_ref[rows, cols] * scale).astype(BF16)
            s_loc = _dot_nt(q, k_ref[keys, cols].astype(BF16)) + bias
            s_ctx = _dot_nt(q, ckb)
            mx = jnp.maximum(jnp.max(s_loc, axis=-1, keepdims=True), jnp.max(s_ctx, axis=-1, keepdims=True))
            p_loc = jnp.exp2(s_loc - mx)
            p_ctx = jnp.exp2(s_ctx - mx)
            denom = jnp.sum(p_loc, axis=-1, keepdims=True) + jnp.sum(p_ctx, axis=-1, keepdims=True)
            o = _dot(p_loc.astype(BF16), v_ref[keys, cols].astype(BF16)) + _dot(p_ctx.astype(BF16), cvb)
            o_ref[rows, cols] = (o / denom).astype(BF16)

    heads_per_step = q_ref.shape[1] // dh

    @pl.when(is_prompt)
    def _():
        for hh in range(heads_per_step):
            prompt_head(hh)

    @pl.when(jnp.logical_not(is_prompt))
    def _():
        for hh in range(heads_per_step):
            latent_head(hh)


def _attention(proj, cache_k, cache_v, cache_idx, strips, p_slabs, slab, seq):
    m = proj.shape[0]
    heads = cache_k.shape[2]
    dh = cache_k.shape[-1]
    past = cache_k.shape[-2]
    n_slabs = m // slab
    per_slab = slab // seq
    p_last = p_slabs - 1

    hp = ATTN_HEADS_PER_STEP
    groups = heads // hp

    def ctx_map(h, s):
        return (jnp.maximum(s - p_slabs, 0), cache_idx, h, 0, 0)

    def new_map(h, s):
        return (jnp.minimum(s, p_last), 0, h, 0, 0)

    kernel = functools.partial(_attn_kernel, p_slabs, seq, dh)
    return pl.pallas_call(
        kernel,
        grid=(groups, n_slabs),
        in_specs=[
            pl.BlockSpec((slab, hp * dh), lambda h, s: (s, h)),
            pl.BlockSpec((slab, hp * dh), lambda h, s: (s, groups + h)),
            pl.BlockSpec((slab, hp * dh), lambda h, s: (s, 2 * groups + h)),
            pl.BlockSpec((1, 1, hp, past, dh), ctx_map),
            pl.BlockSpec((1, 1, hp, past, dh), ctx_map),
            pl.BlockSpec((hp,) + strips.shape[1:], lambda h, s: (h, 0, 0, 0)),
        ],
        out_specs=[
            pl.BlockSpec((slab, hp * dh), lambda h, s: (s, h)),
            pl.BlockSpec((per_slab, 1, hp, seq, dh), new_map),
            pl.BlockSpec((per_slab, 1, hp, seq, dh), new_map),
        ],
        out_shape=[
            jax.ShapeDtypeStruct((m, heads * dh), BF16),
            jax.ShapeDtypeStruct((p_slabs * per_slab, 1, heads, seq, dh), F32),
            jax.ShapeDtypeStruct((p_slabs * per_slab, 1, heads, seq, dh), F32),
        ],
        compiler_params=_cparams(
            ("arbitrary", "arbitrary"),
            hp * (2 * strips.shape[1] * strips.shape[2] * strips.shape[3] * 4 + 16 * slab * dh * 4)
            + 10 * ATTN_QB * (slab + past) * 4 + 8 * MIB),
    )(proj, proj, proj, cache_k, cache_v, strips)


def _na_strips(rpb):
    heads, n_dr, n_dc = rpb.shape
    col = jnp.arange(GRID_W)
    c0 = jnp.clip(col - WIN_W // 2, 0, GRID_W - WIN_W)
    col_ok = (col[None, :] >= c0[:, None]) & (col[None, :] < c0[:, None] + WIN_W)
    dc_idx = jnp.clip(col[None, :] - col[:, None], -(WIN_W - 1), WIN_W - 1) + WIN_W - 1
    onehot = (dc_idx[:, :, None] == jnp.arange(n_dc)[None, None, :]).astype(F32)
    tiles = jnp.einsum("hdk,qck->hqdc", rpb.astype(F32), onehot, precision=lax.Precision.HIGHEST)
    tiles = jnp.where(col_ok[None, :, None, :], tiles * LOG2_E, NEG_INF)
    n_tiles = 2 * WIN_H
    neg = jnp.full((heads, GRID_W, n_tiles + 1 - n_dr, GRID_W), NEG_INF, F32)
    tiles = jnp.concatenate([tiles, neg], axis=2)
    both = jnp.stack([tiles[:, :, :n_tiles], tiles[:, :, 1:]], axis=1)
    return both.reshape(heads, 2, GRID_W, n_tiles * GRID_W)


def _route(hm, wr, br, run):
    h_hi = hm.astype(BF16)
    h_lo = (hm - h_hi.astype(F32)).astype(BF16)
    w_hi = wr.astype(BF16)
    w_lo = (wr - w_hi.astype(F32)).astype(BF16)
    logits = _dot(h_hi, w_hi) + (_dot(h_hi, w_lo) + _dot(h_lo, w_hi)) + br
    tm, width = logits.shape
    col = lax.broadcasted_iota(jnp.int32, (tm, width), 1).astype(F32)
    neg = jnp.float32(-3.0e38)

    gl = jnp.where(col < N_GROUPS, logits, neg)
    gmax = jnp.max(gl, axis=-1, keepdims=True)
    gsel = jnp.min(jnp.where(gl == gmax, col, float(width)), axis=-1, keepdims=True)
    p_g = 1.0 / jnp.sum(jnp.where(col < N_GROUPS, jnp.exp(gl - gmax), 0.0), axis=-1, keepdims=True)

    lo = N_GROUPS + N_EXP_PER_GROUP * gsel
    el = jnp.where((col >= lo) & (col < lo + N_EXP_PER_GROUP), logits, neg)
    e1 = jnp.max(el, axis=-1, keepdims=True)
    i1 = jnp.min(jnp.where(el == e1, col, float(width)), axis=-1, keepdims=True)
    el2 = jnp.where(col == i1, neg, el)
    e2 = jnp.max(el2, axis=-1, keepdims=True)
    i2 = jnp.min(jnp.where(el2 == e2, col, float(width)), axis=-1, keepdims=True)
    t = jnp.exp(e2 - e1)
    w1 = 1.0 / (1.0 + t)
    c1 = p_g * w1
    c2 = p_g * (t * w1)
    id1 = i1 - N_GROUPS
    id2 = i2 - N_GROUPS

    onehot = jnp.where((col == id1) | (col == id2), 1.0, 0.0)
    ri = lax.broadcasted_iota(jnp.int32, (tm, tm), 0)
    ci = lax.broadcasted_iota(jnp.int32, (tm, tm), 1)
    tri = jnp.where(ri > ci, 1.0, 0.0).astype(BF16)
    before = _dot(tri, onehot.astype(BF16)) + run
    r1 = jnp.sum(jnp.where(col == id1, before, 0.0), axis=-1, keepdims=True)
    r2 = jnp.sum(jnp.where(col == id2, before, 0.0), axis=-1, keepdims=True)

    info = jnp.where(col == 0, id1, 0.0)
    info = jnp.where(col == 1, id2, info)
    info = jnp.where(col == 2, c1, info)
    info = jnp.where(col == 3, c2, info)
    info = jnp.where(col == 4, r1, info)
    info = jnp.where(col == 5, r2, info)
    return info, run + jnp.sum(onehot, axis=0, keepdims=True)


def _pos_kernel(info_ref, off_ref, pos_ref):
    info = info_ref[...]
    tm, width = info.shape
    col = lax.broadcasted_iota(jnp.int32, (tm, width), 1).astype(F32)
    off = off_ref[...]
    p1 = jnp.sum(jnp.where(col == info[:, 0:1], off, 0.0), axis=-1, keepdims=True) + info[:, 4:5]
    p2 = jnp.sum(jnp.where(col == info[:, 1:2], off, 0.0), axis=-1, keepdims=True) + info[:, 5:6]
    pm = jnp.where(col == 0, p1, jnp.where(col == 1, p2, 0.0))
    pos_ref[0] = jnp.transpose(pm)[:SUBLANES, :].astype(jnp.int32)


def _positions(info, row_off, tm):
    m = info.shape[0]
    return pl.pallas_call(
        _pos_kernel,
        grid=(m // tm,),
        in_specs=[
            pl.BlockSpec((tm, V7X_LANES), lambda i: (i, 0)),
            pl.BlockSpec((1, V7X_LANES), lambda i: (0, 0)),
        ],
        out_specs=pl.BlockSpec((1, SUBLANES, tm), lambda i: (i, 0, 0)),
        out_shape=jax.ShapeDtypeStruct((m // tm, SUBLANES, tm), jnp.int32),
        compiler_params=_cparams(("arbitrary",), 16 * MIB),
    )(info, row_off)


def _dispatch_kernel(tm, pad_bits, ps_ref, pl_ref, pos_ref, x_ref, sc_ref, sh_ref, xs_hbm,
                     hbuf, zbuf, sems, zsem):
    i = pl.program_id(0)
    n_steps = pl.num_programs(0)
    n_experts = ps_ref.shape[0] - 1
    zrows = zbuf.shape[0]
    max_tail = (xs_hbm.shape[0] - 2 * x_ref.shape[0] * n_steps) // zrows

    def fill_copies(do):
        for e in range(n_experts):
            start = ps_ref[e]
            rem = pl_ref[e]
            for r in range(SUBLANES - 1):
                @pl.when(r < (rem & (SUBLANES - 1)))
                def _(start=start, r=r):
                    do(pltpu.make_async_copy(zbuf.at[pl.ds(0, 1)], xs_hbm.at[pl.ds(start + r, 1)], zsem))

            off = start + rem
            for bit in pad_bits:
                off = off - (rem & bit)

                @pl.when((rem & bit) != 0)
                def _(off=off, bit=bit):
                    dst = xs_hbm.at[pl.ds(pl.multiple_of(off, SUBLANES), bit)]
                    do(pltpu.make_async_copy(zbuf.at[pl.ds(0, bit)], dst, zsem))
        for t in range(max_tail):
            @pl.when(t < pl_ref[n_experts])
            def _(t=t):
                dst = xs_hbm.at[pl.ds(pl.multiple_of(ps_ref[n_experts] + t * zrows, SUBLANES), zrows)]
                do(pltpu.make_async_copy(zbuf, dst, zsem))

    @pl.when(i == 0)
    def _():
        zbuf[...] = jnp.zeros_like(zbuf)
        fill_copies(lambda cp: cp.start())

    def scatter_start(sub):
        for r in range(tm):
            for slot in range(2):
                pltpu.make_async_copy(hbuf.at[sub, pl.ds(r, 1)],
                                      xs_hbm.at[pl.ds(pos_ref[0, slot, sub * tm + r], 1)],
                                      sems.at[sub]).start(priority=slot)

    def scatter_wait(sub):
        for _ in range(2):
            pltpu.make_async_copy(hbuf.at[sub], xs_hbm.at[pl.ds(0, tm)], sems.at[sub]).wait()

    for sub in range(2):
        rows = pl.ds(sub * tm, tm)
        hbuf[sub] = _pack_halves(x_ref[rows, :] * (1.0 + sc_ref[0]) + sh_ref[0])
        scatter_start(sub)
        if sub == 0:
            @pl.when(i > 0)
            def _():
                scatter_wait(1)
        else:
            scatter_wait(0)

    @pl.when(i == n_steps - 1)
    def _():
        scatter_wait(1)
        fill_copies(lambda cp: cp.wait())


def _dispatch(x, posT, pad_start, pad_len, mods3, mod_row, layer, n_rows, tm, moe_tm):
    m, d = x.shape
    base = layer * MOD_ROWS
    step = 2 * tm
    pad_bits = tuple(1 << b for b in range(moe_tm.bit_length() - 2, SUBLANES.bit_length() - 2, -1))
    grid_spec = pltpu.PrefetchScalarGridSpec(
        num_scalar_prefetch=2,
        grid=(m // step,),
        in_specs=[
            pl.BlockSpec((1, SUBLANES, step), lambda i, ps, pn: (i, 0, 0), memory_space=pltpu.SMEM),
            pl.BlockSpec((step, d), lambda i, ps, pn: (i, 0)),
            pl.BlockSpec((1, 1, d), lambda i, ps, pn: (base + mod_row(i, step), 0, MOE_SCALE)),
            pl.BlockSpec((1, 1, d), lambda i, ps, pn: (base + mod_row(i, step), 0, MOE_SHIFT)),
        ],
        out_specs=pl.BlockSpec(memory_space=pl.ANY),
        scratch_shapes=[
            pltpu.VMEM((2, tm, d // 2), U32),
            pltpu.VMEM((moe_tm // 2, d // 2), U32),
            pltpu.SemaphoreType.DMA((2,)),
            pltpu.SemaphoreType.DMA(()),
        ],
    )
    return pl.pallas_call(
        functools.partial(_dispatch_kernel, tm, pad_bits),
        grid_spec=grid_spec,
        out_shape=jax.ShapeDtypeStruct((n_rows, d // 2), U32),
        compiler_params=_cparams(("arbitrary",), 2 * step * d * 4 + 3 * tm * d * 4 + 8 * MIB),
    )(pad_start, pad_len, posT, x, mods3, mods3)


def _moe_kernel(layer, te_ref, nu_ref, nxt_ref, par_ref, xs_ref, wg_hbm, wu_hbm, wd_hbm, ys_ref,
                wg_f, wu_f, wd_f, wg_b, wu_b, wd_b, sems):
    i = pl.program_id(0)
    used = i < nu_ref[0]
    expert = te_ref[i]
    first = used & ((i == 0) | (expert != te_ref[jnp.maximum(i - 1, 0)]))

    def weight_copies(e, slot):
        return (pltpu.make_async_copy(wg_hbm.at[layer, e], wg_f.at[slot], sems.at[slot]),
                pltpu.make_async_copy(wu_hbm.at[layer, e], wu_f.at[slot], sems.at[slot]),
                pltpu.make_async_copy(wd_hbm.at[layer, e], wd_f.at[slot], sems.at[slot]))

    @pl.when(jnp.logical_not(used))
    def _():
        ys_ref[...] = jnp.zeros_like(ys_ref)

    @pl.when(i == 0)
    def _():
        for cp in weight_copies(expert, par_ref[0]):
            cp.start(priority=1)

    def mlp(recast_slot):
        x = jnp.concatenate(_unpack_halves(xs_ref[...]), axis=1).astype(BF16)
        if recast_slot is not None:
            wg_b[...] = wg_f[recast_slot].astype(BF16)
        g = _dot(x, wg_b[...])
        if recast_slot is not None:
            wu_b[...] = wu_f[recast_slot].astype(BF16)
        u = _dot(x, wu_b[...])
        if recast_slot is not None:
            wd_b[...] = wd_f[recast_slot].astype(BF16)
        hid = (_silu(g) * u).astype(BF16)
        ys_ref[...] = _pack_halves(_dot(hid, wd_b[...]))

    @pl.when(first)
    def _():
        slot = par_ref[i]
        for cp in weight_copies(expert, slot):
            cp.wait()

        @pl.when(nxt_ref[i] >= 0)
        def _():
            for cp in weight_copies(nxt_ref[i], 1 - slot):
                cp.start(priority=1)

        mlp(slot)

    @pl.when(used & jnp.logical_not(first))
    def _():
        mlp(None)


def _moe_experts(xs, tile_expert, n_used, next_expert, slot, w_gate, w_up, w_down, layer, tm):
    n_rows, dp = xs.shape
    d = 2 * dp
    f = w_gate.shape[-1]
    n_tiles = n_rows // tm
    grid_spec = pltpu.PrefetchScalarGridSpec(
        num_scalar_prefetch=4,
        grid=(n_tiles,),
        in_specs=[
            pl.BlockSpec((tm, dp), lambda i, te, nu, nx, pr: (jnp.maximum(jnp.minimum(i, nu[0] - 1), 0), 0)),
            pl.BlockSpec(memory_space=pl.ANY),
            pl.BlockSpec(memory_space=pl.ANY),
            pl.BlockSpec(memory_space=pl.ANY),
        ],
        out_specs=pl.BlockSpec((tm, dp), lambda i, te, nu, nx, pr: (i, 0)),
        scratch_shapes=[
            pltpu.VMEM((2, d, f), F32), pltpu.VMEM((2, d, f), F32), pltpu.VMEM((2, f, d), F32),
            pltpu.VMEM((d, f), BF16), pltpu.VMEM((d, f), BF16), pltpu.VMEM((f, d), BF16),
            pltpu.SemaphoreType.DMA((2,)),
        ],
    )
    return pl.pallas_call(
        functools.partial(_moe_kernel, layer),
        grid_spec=grid_spec,
        out_shape=jax.ShapeDtypeStruct((n_rows, dp), U32),
        compiler_params=_cparams(("arbitrary",), 6 * d * f * 4 + 3 * d * f * 2 + 8 * tm * d * 4 + 8 * MIB),
    )(tile_expert, n_used, next_expert, slot, xs, w_gate, w_up, w_down)


def _combine_ln_kernel(alpha, tm, p_steps, pos_ref, nxt_ref, ys_hbm, info_ref, x_ref, gate_ref, g_ref, b_ref,
                       *rest):
    i = pl.program_id(0)
    n_steps = pl.num_programs(0)
    if p_steps is None:
        nsc_ref, nsh_ref, o_ref, hb_ref, buf, sems = rest
    else:
        op_ref, os_ref, buf, sems = rest

    par = i % 2

    def gather_start(parity, sub, p_ref):
        for r in range(tm):
            for slot in range(2):
                pltpu.make_async_copy(ys_hbm.at[pl.ds(p_ref[0, slot, sub * tm + r], 1)],
                                      buf.at[parity, sub, slot, pl.ds(r, 1)],
                                      sems.at[parity, sub]).start(priority=slot)

    def gather_wait(parity, sub):
        for slot in range(2):
            pltpu.make_async_copy(ys_hbm.at[pl.ds(0, tm)], buf.at[parity, sub, slot], sems.at[parity, sub]).wait()

    @pl.when(i == 0)
    def _():
        gather_start(0, 0, pos_ref)
        gather_start(0, 1, pos_ref)

    def step_body(parity):
        for sub in range(2):
            gather_wait(parity, sub)
            rows = pl.ds(sub * tm, tm)
            info = info_ref[rows, :]
            lo1, hi1 = _unpack_halves(buf[parity, sub, 0])
            lo2, hi2 = _unpack_halves(buf[parity, sub, 1])
            c1 = info[:, 2:3]
            c2 = info[:, 3:4]
            y = jnp.concatenate([c1 * lo1 + c2 * lo2, c1 * hi1 + c2 * hi2], axis=1)
            z = alpha * x_ref[rows, :] + gate_ref[0] * y
            res = _layer_norm(z, g_ref[0], b_ref[0])
            gather_start(1 - parity, sub, nxt_ref)
            if p_steps is None:
                o_ref[rows, :] = res
                hb_ref[rows, :] = (res * (1.0 + nsc_ref[0]) + nsh_ref[0]).astype(BF16)
            else:
                @pl.when(i < p_steps)
                def _(rows=rows, res=res):
                    op_ref[rows, :] = res

                @pl.when(i >= p_steps)
                def _(rows=rows, res=res):
                    os_ref[rows, :] = res

        @pl.when(i == n_steps - 1)
        def _():
            gather_wait(1 - parity, 0)
            gather_wait(1 - parity, 1)

    for parity in range(2):
        @pl.when(par == parity)
        def _(parity=parity):
            step_body(parity)


def _combine_ln(ys, posT, info, x, mods3, mod_row, layer, ln_g, ln_b, alpha, tm, last, p_rows):
    m, d = x.shape
    step = 2 * tm
    n_steps = m // step
    base = layer * MOD_ROWS
    in_specs = [
        pl.BlockSpec((1, SUBLANES, step), lambda i: (i, 0, 0), memory_space=pltpu.SMEM),
        pl.BlockSpec((1, SUBLANES, step), lambda i: (jnp.minimum(i + 1, n_steps - 1), 0, 0), memory_space=pltpu.SMEM),
        pl.BlockSpec(memory_space=pl.ANY),
        pl.BlockSpec((step, V7X_LANES), lambda i: (i, 0)),
        pl.BlockSpec((step, d), lambda i: (i, 0)),
        pl.BlockSpec((1, 1, d), lambda i: (base + mod_row(i, step), 0, MOE_GATE)),
        pl.BlockSpec((1, 1, d), lambda i: (layer, 0, 0)),
        pl.BlockSpec((1, 1, d), lambda i: (layer, 0, 0)),
    ]
    args = [posT, posT, ys, info, x, mods3, ln_g, ln_b]
    if last:
        p_steps = p_rows // step
        out_specs = [
            pl.BlockSpec((step, d), lambda i: (jnp.minimum(i, p_steps - 1), 0)),
            pl.BlockSpec((step, d), lambda i: (jnp.maximum(i - p_steps, 0), 0)),
        ]
        out_shape = [jax.ShapeDtypeStruct((p_rows, d), F32), jax.ShapeDtypeStruct((m - p_rows, d), F32)]
    else:
        p_steps = None
        nbase = (layer + 1) * MOD_ROWS
        in_specs += [
            pl.BlockSpec((1, 1, d), lambda i: (nbase + mod_row(i, step), 0, MIX_SCALE)),
            pl.BlockSpec((1, 1, d), lambda i: (nbase + mod_row(i, step), 0, MIX_SHIFT)),
        ]
        args += [mods3, mods3]
        out_specs = [pl.BlockSpec((step, d), lambda i: (i, 0)), pl.BlockSpec((step, d), lambda i: (i, 0))]
        out_shape = [jax.ShapeDtypeStruct((m, d), F32), jax.ShapeDtypeStruct((m, d), BF16)]
    return pl.pallas_call(
        functools.partial(_combine_ln_kernel, alpha, tm, p_steps),
        grid=(n_steps,),
        in_specs=in_specs,
        out_specs=out_specs,
        out_shape=out_shape,
        scratch_shapes=[pltpu.VMEM((2, 2, 2, tm, d // 2), U32), pltpu.SemaphoreType.DMA((2, 2))],
        compiler_params=_cparams(("arbitrary",), 6 * step * d * 4 + 4 * tm * d * 4 + 6 * tm * d * 4 + 8 * MIB),
    )(*args)


def _router_params(layer, w_rg, b_rg, w_re, b_re):
    d = w_rg.shape[1]
    n_e = N_EXPERTS
    pad = V7X_LANES - N_GROUPS - n_e
    wr = jnp.concatenate(
        [w_rg[layer], w_re[layer].transpose(1, 0, 2).reshape(d, n_e), jnp.zeros((d, pad), F32)], axis=1)
    br = jnp.concatenate([b_rg[layer], b_re[layer].reshape(n_e), jnp.zeros((pad,), F32)])[None, :]
    return wr, br


def _hier_moe_ln(x, info, cnt, mods3, mod_row, layer, w_gate, w_up, w_down, ln_g, ln_b, alpha, last, p_rows):
    m, d = x.shape
    n_e = N_EXPERTS

    tm = MOE_TM
    n_tiles = 2 * m // tm + n_e
    counts = cnt[0, :n_e].astype(jnp.int32)
    tiles_e = (counts + tm - 1) // tm
    tile_end = jnp.cumsum(tiles_e)
    n_used = tile_end[-1]
    row_off = (tile_end - tiles_e) * tm
    tile_ids = jnp.minimum(jnp.arange(n_tiles, dtype=jnp.int32), n_used - 1)
    tile_expert = jnp.sum((tile_ids[:, None] >= tile_end[None, :]).astype(jnp.int32), axis=1)
    group_end = tile_end[tile_expert]
    next_expert = jnp.where(group_end < n_used, tile_expert[jnp.minimum(group_end, n_tiles - 1)], -1)
    is_first = jnp.concatenate([jnp.ones((1,), jnp.int32), (tile_expert[1:] != tile_expert[:-1]).astype(jnp.int32)])
    slot = (jnp.cumsum(is_first) - 1) % 2
    zrows = tm // 2
    pad_start = jnp.concatenate([row_off + counts, (n_used * tm)[None]])
    pad_len = jnp.concatenate([tiles_e * tm - counts, ((n_tiles - n_used) * (tm // zrows))[None]])
    row_off_f = jnp.concatenate([row_off.astype(F32), jnp.zeros((V7X_LANES - n_e,), F32)])[None, :]

    ct = COMB_TM
    posT = _positions(info, row_off_f, 2 * ct)
    xs = _dispatch(x, posT, pad_start, pad_len, mods3, mod_row, layer, n_tiles * tm, ct, tm)
    ys = _moe_experts(xs, tile_expert, n_used.reshape(1), next_expert.astype(jnp.int32), slot.astype(jnp.int32),
                      w_gate, w_up, w_down, layer, tm)
    return _combine_ln(ys, posT, info, x, mods3, mod_row, layer, ln_g, ln_b, alpha, ct, last, p_rows)


def kernel(x_prompt, x_sample, state_ret, cache_na_k, cache_na_v, c, c_ctx, w_mod, b_mod, ln1_g, ln1_b, ln2_g, ln2_b, w_ret_in, ret_decay_logit, ret_gn_g, ret_gn_b, w_ret_out, w_na_in, na_rpb, w_na_out, w_rg, b_rg, w_re, b_re, w_gate, w_up, w_down):
    bp, seq, d = x_prompt.shape
    bs, dec_seq, _ = x_sample.shape
    depth = w_mod.shape[0]
    slab = dec_seq
    assert slab % seq == 0 and (bp * seq) % slab == 0 and bs + 1 <= MOD_ROWS
    p_rows = bp * seq
    p_slabs = p_rows // slab
    alpha = (2.0 * depth) ** 0.25

    def mod_row(i, tm):
        start = i * tm
        return jnp.where(start < p_rows, 0, 1 + (start - p_rows) // dec_seq)

    cc = jnp.concatenate([c_ctx[None, :], c, jnp.zeros((MOD_ROWS - 1 - bs, d), F32)], axis=0)
    mods = _modulation(cc, w_mod, b_mod)
    mods3 = mods.reshape(depth * MOD_ROWS, 1, 6 * d)
    x, hb = _prologue(x_prompt.reshape(p_rows, d), x_sample.reshape(bs * dec_seq, d), mods3, mod_row)

    ln1_g3, ln1_b3 = ln1_g.reshape(depth, 1, d), ln1_b.reshape(depth, 1, d)
    ln2_g3, ln2_b3 = ln2_g.reshape(depth, 1, d), ln2_b.reshape(depth, 1, d)

    new_ret, new_k, new_v = [], [], []
    for l in range(depth):
        j = l // 2
        if l % 2 == 0:
            dk = state_ret.shape[-2]
            proj = _inproj(hb, w_ret_in, j)
            log_g = -jax.nn.softplus(-ret_decay_logit[j].astype(F32))
            t = jnp.arange(slab)
            nf = dk // 4
            inv_freq = ROPE_BASE ** (-jnp.arange(nf, dtype=F32) / nf)
            ang_r = (t // GRID_W).astype(F32)[:, None] * inv_freq
            ang_c = (t % GRID_W).astype(F32)[:, None] * inv_freq
            cos = jnp.concatenate([jnp.cos(ang_r)] * 2 + [jnp.cos(ang_c)] * 2, axis=1)
            sin = jnp.concatenate([-jnp.sin(ang_r), jnp.sin(ang_r), -jnp.sin(ang_c), jnp.sin(ang_c)], axis=1)
            y, st = _retention(proj, log_g, cos, sin, state_ret, j, ret_gn_g[j][:, None, :],
                               ret_gn_b[j][:, None, :], p_slabs, slab)
            new_ret.append(st)
            mixed, w_out = y, w_ret_out[j]
        else:
            proj = _inproj(hb, w_na_in, j)
            o, nk, nv = _attention(proj, cache_na_k, cache_na_v, j, _na_strips(na_rpb[j]), p_slabs, slab, seq)
            new_k.append(nk[:, 0])
            new_v.append(nv[:, 0])
            mixed, w_out = o, w_na_out[j]
        wr, br = _router_params(l, w_rg, b_rg, w_re, b_re)
        x, info, cnt = _outproj_ln(mixed, w_out.astype(BF16), x, mods3, mod_row, l, ln1_g3, ln1_b3, wr, br, alpha)
        x, hb = _hier_moe_ln(x, info, cnt, mods3, mod_row, l, w_gate, w_up, w_down, ln2_g3, ln2_b3, alpha,
                             l == depth - 1, p_rows)

    y_prompt = x.reshape(bp, seq, d)
    y_sample = hb.reshape(bs, dec_seq, d)
    return (y_prompt, y_sample, jnp.stack(new_ret, axis=1), jnp.stack(new_k, axis=1), jnp.stack(new_v, axis=1))
```

```python
import functools

import jax
import jax.numpy as jnp
from jax import lax
from jax.experimental import pallas as pl
from jax.experimental.pallas import tpu as pltpu

F32 = jnp.float32
BF16 = jnp.bfloat16
U32 = jnp.uint32
MIB = 1024 * 1024

GRID_W = 64
WIN_H = 8
WIN_W = 16
N_GROUPS = 4
N_EXP_PER_GROUP = 8
N_EXPERTS = N_GROUPS * N_EXP_PER_GROUP
ROPE_BASE = 10000.0
LN_EPS = 1e-5
NEG_INF = -1e30
LOG2_E = 1.4426950408889634

V7X_VMEM_BYTES = 64 * MIB
V7X_LANES = 128
SUBLANES = 8
MOD_ROWS = 8
MIX_SHIFT, MIX_SCALE, MIX_GATE, MOE_SHIFT, MOE_SCALE, MOE_GATE = range(6)

RET_CHUNK = 256
ATTN_QB = 256
ATTN_HEADS_PER_STEP = 4
MOE_TM = 256
COMB_TM = 256


def _cparams(sem, vmem_bytes):
    limit = int(min(max(vmem_bytes, 16 * MIB), V7X_VMEM_BYTES - 6 * MIB))
    return pltpu.CompilerParams(dimension_semantics=sem, vmem_limit_bytes=limit)


def _silu(x):
    h = 0.5 * x
    return h + h * jnp.tanh(h)


def _pack_halves(x):
    n = x.shape[1] // 2
    return pltpu.pack_elementwise([x[:, :n], x[:, n:]], packed_dtype=BF16)


def _unpack_halves(p):
    return (pltpu.unpack_elementwise(p, index=0, packed_dtype=BF16, unpacked_dtype=F32),
            pltpu.unpack_elementwise(p, index=1, packed_dtype=BF16, unpacked_dtype=F32))


def _dot(a, b):
    return jnp.dot(a, b, preferred_element_type=F32)


def _dot_nt(a, b):
    return lax.dot_general(a, b, (((1,), (1,)), ((), ())), preferred_element_type=F32)


def _layer_norm(z, g, b):
    mu = jnp.mean(z, axis=-1, keepdims=True)
    zc = z - mu
    var = jnp.mean(zc * zc, axis=-1, keepdims=True)
    return zc * lax.rsqrt(var + LN_EPS) * g + b


def _mod_kernel(c_ref, w_ref, b_ref, o_ref):
    s = _silu(c_ref[...]).astype(BF16)
    o_ref[0] = _dot(s, w_ref[0].astype(BF16)) + b_ref[0]


def _modulation(cc, w_mod, b_mod):
    depth, d, n6 = w_mod.shape
    tn = 1024
    return pl.pallas_call(
        _mod_kernel,
        grid=(depth, n6 // tn),
        in_specs=[
            pl.BlockSpec((MOD_ROWS, d), lambda l, j: (0, 0)),
            pl.BlockSpec((1, d, tn), lambda l, j: (l, 0, j)),
            pl.BlockSpec((1, 1, tn), lambda l, j: (l, 0, j)),
        ],
        out_specs=pl.BlockSpec((1, MOD_ROWS, tn), lambda l, j: (l, 0, j)),
        out_shape=jax.ShapeDtypeStruct((depth, MOD_ROWS, n6), F32),
        compiler_params=_cparams(("arbitrary", "arbitrary"), 3 * d * tn * 4 + 4 * MIB),
    )(cc, w_mod, b_mod.reshape(depth, 1, n6))


def _prologue_kernel(p_tiles, xp_ref, xs_ref, sc_ref, sh_ref, x_ref, hb_ref):
    @pl.when(pl.program_id(0) < p_tiles)
    def _():
        x = xp_ref[...]
        x_ref[...] = x
        hb_ref[...] = (x * (1.0 + sc_ref[0]) + sh_ref[0]).astype(BF16)

    @pl.when(pl.program_id(0) >= p_tiles)
    def _():
        x = xs_ref[...]
        x_ref[...] = x
        hb_ref[...] = (x * (1.0 + sc_ref[0]) + sh_ref[0]).astype(BF16)


def _prologue(xp, xs, mods3, mod_row, tm=512):
    p_rows, d = xp.shape
    m = p_rows + xs.shape[0]
    p_tiles = p_rows // tm
    return pl.pallas_call(
        functools.partial(_prologue_kernel, p_tiles),
        grid=(m // tm,),
        in_specs=[
            pl.BlockSpec((tm, d), lambda i: (jnp.minimum(i, p_tiles - 1), 0)),
            pl.BlockSpec((tm, d), lambda i: (jnp.maximum(i - p_tiles, 0), 0)),
            pl.BlockSpec((1, 1, d), lambda i: (mod_row(i, tm), 0, MIX_SCALE)),
            pl.BlockSpec((1, 1, d), lambda i: (mod_row(i, tm), 0, MIX_SHIFT)),
        ],
        out_specs=[pl.BlockSpec((tm, d), lambda i: (i, 0)), pl.BlockSpec((tm, d), lambda i: (i, 0))],
        out_shape=[jax.ShapeDtypeStruct((m, d), F32), jax.ShapeDtypeStruct((m, d), BF16)],
        compiler_params=_cparams(("arbitrary",), 8 * tm * d * 4 + 8 * MIB),
    )(xp, xs, mods3, mods3)


def _inproj_kernel(h_ref, w_ref, o_ref):
    o_ref[...] = _dot(h_ref[...], w_ref[...].astype(BF16))


def _inproj(hb, w_all, w_idx, tm=2048, tn=512):
    m, d = hb.shape
    n = w_all.shape[-1]
    return pl.pallas_call(
        _inproj_kernel,
        grid=(m // tm, n // tn),
        in_specs=[
            pl.BlockSpec((tm, d), lambda i, j: (i, 0)),
            pl.BlockSpec((None, d, tn), lambda i, j: (w_idx, 0, j)),
        ],
        out_specs=pl.BlockSpec((tm, tn), lambda i, j: (i, j)),
        out_shape=jax.ShapeDtypeStruct((m, n), F32),
        compiler_params=_cparams(
            ("arbitrary", "arbitrary"),
            2 * tm * d * 2 + 2 * d * tn * 4 + d * tn * 2 + 3 * tm * tn * 4 + 4 * MIB),
    )(hb, w_all)


def _outproj_ln_kernel(alpha, y_ref, w_ref, x_ref, gate_ref, g_ref, b_ref, sc_ref, sh_ref, wr_ref, br_ref,
                       o_ref, info_ref, cnt_ref, run_ref):
    @pl.when(pl.program_id(0) == 0)
    def _():
        run_ref[...] = jnp.zeros_like(run_ref)

    z = alpha * x_ref[...] + gate_ref[0] * _dot(y_ref[...], w_ref[...])
    x1 = _layer_norm(z, g_ref[0], b_ref[0])
    o_ref[...] = x1
    info, run = _route(x1 * (1.0 + sc_ref[0]) + sh_ref[0], wr_ref[...], br_ref[...], run_ref[...])
    info_ref[...] = info
    run_ref[...] = run
    cnt_ref[...] = run


def _outproj_ln(y, w, x, mods3, mod_row, layer, ln_g, ln_b, wr, br, alpha, tm=512):
    m, kdim = y.shape
    d = x.shape[1]
    base = layer * MOD_ROWS
    return pl.pallas_call(
        functools.partial(_outproj_ln_kernel, alpha),
        grid=(m // tm,),
        in_specs=[
            pl.BlockSpec((tm, kdim), lambda i: (i, 0)),
            pl.BlockSpec((kdim, d), lambda i: (0, 0)),
            pl.BlockSpec((tm, d), lambda i: (i, 0)),
            pl.BlockSpec((1, 1, d), lambda i: (base + mod_row(i, tm), 0, MIX_GATE)),
            pl.BlockSpec((1, 1, d), lambda i: (layer, 0, 0)),
            pl.BlockSpec((1, 1, d), lambda i: (layer, 0, 0)),
            pl.BlockSpec((1, 1, d), lambda i: (base + mod_row(i, tm), 0, MOE_SCALE)),
            pl.BlockSpec((1, 1, d), lambda i: (base + mod_row(i, tm), 0, MOE_SHIFT)),
            pl.BlockSpec((d, V7X_LANES), lambda i: (0, 0)),
            pl.BlockSpec((1, V7X_LANES), lambda i: (0, 0)),
        ],
        out_specs=[
            pl.BlockSpec((tm, d), lambda i: (i, 0)),
            pl.BlockSpec((tm, V7X_LANES), lambda i: (i, 0)),
            pl.BlockSpec((1, V7X_LANES), lambda i: (0, 0)),
        ],
        out_shape=[
            jax.ShapeDtypeStruct((m, d), F32),
            jax.ShapeDtypeStruct((m, V7X_LANES), F32),
            jax.ShapeDtypeStruct((1, V7X_LANES), F32),
        ],
        scratch_shapes=[pltpu.VMEM((1, V7X_LANES), F32)],
        compiler_params=_cparams(
            ("arbitrary",), kdim * d * 2 + 2 * tm * kdim * 2 + 9 * tm * d * 4 + d * V7X_LANES * 4 + 4 * MIB),
    )(y, w, x, mods3, ln_g, ln_b, mods3, mods3, wr, br)


def _rope(x, cos, sin):
    half = V7X_LANES // 2
    sw = jnp.concatenate(
        [pltpu.roll(x[:, :V7X_LANES], half, 1), pltpu.roll(x[:, V7X_LANES:], half, 1)], axis=1)
    return x * cos + sw * sin


def _group_norm(o, g, b):
    mu = jnp.mean(o, axis=-1, keepdims=True)
    oc = o - mu
    var = jnp.mean(oc * oc, axis=-1, keepdims=True)
    return oc * lax.rsqrt(var + LN_EPS) * g + b


def _ret_kernel(p_slabs, dk, lg_ref, q_ref, k_ref, v_ref, gf_ref, gb_ref, cos_ref, sin_ref, s0_ref,
                gng_ref, gnb_ref, y_ref, st_ref, q_s, k_s, yacc):
    h = pl.program_id(0)
    s = pl.program_id(1)
    is_prompt = s < p_slabs
    c = RET_CHUNK
    n_chunks = q_ref.shape[0] // c

    def run(prompt):
        ri = lax.broadcasted_iota(jnp.int32, (c, c), 0)
        ci = lax.broadcasted_iota(jnp.int32, (c, c), 1)
        diff = (ri - ci).astype(F32)
        pos = lax.broadcasted_iota(jnp.int32, (c, 1), 0).astype(F32)
        gn_g = gng_ref[0]
        gn_b = gnb_ref[0]
        scale = dk ** -0.5
        if not prompt:
            q_s[...] = _rope(q_ref[...], cos_ref[...], sin_ref[...])
            k_s[...] = _rope(k_ref[...], cos_ref[...], sin_ref[...]) * scale

        for direction in (0, 1):
            lg = jnp.full((1, 1), lg_ref[direction, h], F32)
            if direction == 0:
                mask = jnp.where(diff >= 0, jnp.exp(jnp.maximum(diff, 0.0) * lg), 0.0)
                q_dec = jnp.exp((pos + 1.0) * lg)
                k_dec = jnp.exp((c - 1.0 - pos) * lg)
                order = range(n_chunks)
                gate_ref = gf_ref
            else:
                mask = jnp.where(diff <= 0, jnp.exp(jnp.maximum(-diff, 0.0) * lg), 0.0)
                q_dec = jnp.exp((c - pos) * lg)
                k_dec = jnp.exp(pos * lg)
                order = range(n_chunks - 1, -1, -1)
                gate_ref = gb_ref
            c_dec = jnp.exp(c * lg)
            state = None if prompt else s0_ref[0, direction, 0]
            for ch in order:
                rows = pl.ds(ch * c, c)
                if prompt:
                    qc = q_ref[rows, :]
                    kc = k_ref[rows, :] * scale
                else:
                    qc = q_s[rows, :]
                    kc = k_s[rows, :]
                vb = v_ref[rows, :].astype(BF16)
                inner = _dot_nt(qc.astype(BF16), kc.astype(BF16)) * mask
                o = _dot(inner.astype(BF16), vb)
                kv = _dot(jnp.transpose(kc * k_dec).astype(BF16), vb)
                if prompt:
                    st_ref[ch, direction, 0] = kv
                else:
                    o = o + _dot((qc * q_dec).astype(BF16), state.astype(BF16))
                    state = c_dec * state + kv
                contrib = _group_norm(o, gn_g, gn_b) * _silu(gate_ref[rows, :])
                if direction == 0:
                    yacc[rows, :] = contrib
                else:
                    y_ref[rows, :] = (yacc[rows, :] + contrib).astype(BF16)

    @pl.when(is_prompt)
    def _():
        run(True)

    @pl.when(jnp.logical_not(is_prompt))
    def _():
        run(False)


def _retention(proj, log_g, rope_cos, rope_sin, state_ret, ret_idx, gn_g, gn_b, p_slabs, slab):
    m = proj.shape[0]
    heads, dk, dv = state_ret.shape[-3:]
    n_slabs = m // slab
    per_slab = slab // RET_CHUNK
    v_blk0 = 2 * heads * dk // dv
    p_last = p_slabs - 1

    def s0_map(h, s):
        return (jnp.maximum(s - p_slabs, 0), ret_idx, 0, h, 0, 0)

    kernel = functools.partial(_ret_kernel, p_slabs, dk)
    return pl.pallas_call(
        kernel,
        grid=(heads, n_slabs),
        in_specs=[
            pl.BlockSpec(memory_space=pltpu.SMEM),
            pl.BlockSpec((slab, dk), lambda h, s: (s, h)),
            pl.BlockSpec((slab, dk), lambda h, s: (s, heads + h)),
            pl.BlockSpec((slab, dv), lambda h, s: (s, v_blk0 + h)),
            pl.BlockSpec((slab, dv), lambda h, s: (s, v_blk0 + heads + h)),
            pl.BlockSpec((slab, dv), lambda h, s: (s, v_blk0 + 2 * heads + h)),
            pl.BlockSpec((slab, dk), lambda h, s: (0, 0)),
            pl.BlockSpec((slab, dk), lambda h, s: (0, 0)),
            pl.BlockSpec((1, None, 2, 1, dk, dv), s0_map),
            pl.BlockSpec((1, 1, dv), lambda h, s: (h, 0, 0)),
            pl.BlockSpec((1, 1, dv), lambda h, s: (h, 0, 0)),
        ],
        out_specs=[
            pl.BlockSpec((slab, dv), lambda h, s: (s, h)),
            pl.BlockSpec((per_slab, 2, 1, dk, dv), lambda h, s: (jnp.minimum(s, p_last), 0, h, 0, 0)),
        ],
        out_shape=[
            jax.ShapeDtypeStruct((m, heads * dv), BF16),
            jax.ShapeDtypeStruct((p_slabs * per_slab, 2, heads, dk, dv), F32),
        ],
        scratch_shapes=[
            pltpu.VMEM((slab, dk), F32),
            pltpu.VMEM((slab, dk), F32),
            pltpu.VMEM((slab, dv), F32),
        ],
        compiler_params=_cparams(
            ("arbitrary", "arbitrary"),
            2 * (2 * slab * dk + 3 * slab * dv) * 4 + 4 * slab * dk * 4 + 2 * 2 * dk * dv * 4
            + 2 * slab * dv * 2 + 2 * per_slab * 2 * dk * dv * 4 + 2 * slab * dk * 4 + slab * dv * 4
            + 8 * MIB),
    )(log_g, proj, proj, proj, proj, proj, rope_cos, rope_sin, state_ret, gn_g, gn_b)


def _na_window(rq, rows):
    kh = min(WIN_H, rows)
    return min(max(rq - kh // 2, 0), rows - kh), kh


def _attn_kernel(p_slabs, seq, dh, q_ref, k_ref, v_ref, ck_ref, cv_ref, strip_ref, o_ref, nk_ref, nv_ref):
    s = pl.program_id(1)
    is_prompt = s < p_slabs
    slab = q_ref.shape[0]
    scale = dh ** -0.5 * LOG2_E

    def prompt_head(hh):
        cols = pl.ds(hh * dh, dh)
        for b in range(slab // seq):
            rows = pl.ds(b * seq, seq)
            k = k_ref[rows, cols]
            v = v_ref[rows, cols]
            nk_ref[b, 0, hh] = k
            nv_ref[b, 0, hh] = v
            q = (q_ref[rows, cols] * scale).astype(BF16)
            sc = _dot_nt(q, k.astype(BF16))
            p = jnp.exp2(sc - jnp.max(sc, axis=-1, keepdims=True))
            denom = jnp.sum(p, axis=-1, keepdims=True)
            o_ref[rows, cols] = (_dot(p.astype(BF16), v.astype(BF16)) / denom).astype(BF16)

    def latent_head(hh):
        cols = pl.ds(hh * dh, dh)
        grid_rows = slab // GRID_W
        rows_per_block = ATTN_QB // GRID_W
        ckb = ck_ref[0, 0, hh].astype(BF16)
        cvb = cv_ref[0, 0, hh].astype(BF16)
        for qb in range(slab // ATTN_QB):
            rq0 = qb * rows_per_block
            ka = _na_window(rq0, grid_rows)[0] // 2 * 2
            last0, kh = _na_window(rq0 + rows_per_block - 1, grid_rows)
            kb = -((last0 + kh) // -2) * 2
            n_keys = (kb - ka) * GRID_W
            keys = pl.ds(ka * GRID_W, n_keys)
            pieces = []
            for i in range(rows_per_block):
                rq = rq0 + i
                r0, kh = _na_window(rq, grid_rows)
                first = ka - rq + WIN_H - 1
                piece = strip_ref[hh, first % 2, :, pl.ds((first - first % 2) * GRID_W, n_keys)]
                if r0 != ka or r0 + kh != kb:
                    key_row = ka + (lax.broadcasted_iota(jnp.int32, piece.shape, 1) // GRID_W)
                    piece = jnp.where((key_row >= r0) & (key_row < r0 + kh), piece, NEG_INF)
                pieces.append(piece)
            bias = jnp.concatenate(pieces, axis=0)
            rows = pl.ds(qb * ATTN_QB, ATTN_QB)
            q = (q_ref[rows, cols] * scale).astype(BF16)
            s_loc = _dot_nt(q, k_ref[keys, cols].astype(BF16)) + bias
            s_ctx = _dot_nt(q, ckb)
            mx = jnp.maximum(jnp.max(s_loc, axis=-1, keepdims=True), jnp.max(s_ctx, axis=-1, keepdims=True))
            p_loc = jnp.exp2(s_loc - mx)
            p_ctx = jnp.exp2(s_ctx - mx)
            denom = jnp.sum(p_loc, axis=-1, keepdims=True) + jnp.sum(p_ctx, axis=-1, keepdims=True)
            o = _dot(p_loc.astype(BF16), v_ref[keys, cols].astype(BF16)) + _dot(p_ctx.astype(BF16), cvb)
            o_ref[rows, cols] = (o / denom).astype(BF16)

    heads_per_step = q_ref.shape[1] // dh

    @pl.when(is_prompt)
    def _():
        for hh in range(heads_per_step):
            prompt_head(hh)

    @pl.when(jnp.logical_not(is_prompt))
    def _():
        for hh in range(heads_per_step):
            latent_head(hh)


def _attention(proj, cache_k, cache_v, cache_idx, strips, p_slabs, slab, seq):
    m = proj.shape[0]
    heads = cache_k.shape[2]
    dh = cache_k.shape[-1]
    past = cache_k.shape[-2]
    n_slabs = m // slab
    per_slab = slab // seq
    p_last = p_slabs - 1

    hp = ATTN_HEADS_PER_STEP
    groups = heads // hp

    def ctx_map(h, s):
        return (jnp.maximum(s - p_slabs, 0), cache_idx, h, 0, 0)

    def new_map(h, s):
        return (jnp.minimum(s, p_last), 0, h, 0, 0)

    kernel = functools.partial(_attn_kernel, p_slabs, seq, dh)
    return pl.pallas_call(
        kernel,
        grid=(groups, n_slabs),
        in_specs=[
            pl.BlockSpec((slab, hp * dh), lambda h, s: (s, h)),
            pl.BlockSpec((slab, hp * dh), lambda h, s: (s, groups + h)),
            pl.BlockSpec((slab, hp * dh), lambda h, s: (s, 2 * groups + h)),
            pl.BlockSpec((1, 1, hp, past, dh), ctx_map),
            pl.BlockSpec((1, 1, hp, past, dh), ctx_map),
            pl.BlockSpec((hp,) + strips.shape[1:], lambda h, s: (h, 0, 0, 0)),
        ],
        out_specs=[
            pl.BlockSpec((slab, hp * dh), lambda h, s: (s, h)),
            pl.BlockSpec((per_slab, 1, hp, seq, dh), new_map),
            pl.BlockSpec((per_slab, 1, hp, seq, dh), new_map),
        ],
        out_shape=[
            jax.ShapeDtypeStruct((m, heads * dh), BF16),
            jax.ShapeDtypeStruct((p_slabs * per_slab, 1, heads, seq, dh), F32),
            jax.ShapeDtypeStruct((p_slabs * per_slab, 1, heads, seq, dh), F32),
        ],
        compiler_params=_cparams(
            ("arbitrary", "arbitrary"),
            hp * (2 * strips.shape[1] * strips.shape[2] * strips.shape[3] * 4 + 16 * slab * dh * 4)
            + 10 * ATTN_QB * (slab + past) * 4 + 8 * MIB),
    )(proj, proj, proj, cache_k, cache_v, strips)


def _na_strips(rpb):
    heads, n_dr, n_dc = rpb.shape
    col = jnp.arange(GRID_W)
    c0 = jnp.clip(col - WIN_W // 2, 0, GRID_W - WIN_W)
    col_ok = (col[None, :] >= c0[:, None]) & (col[None, :] < c0[:, None] + WIN_W)
    dc_idx = jnp.clip(col[None, :] - col[:, None], -(WIN_W - 1), WIN_W - 1) + WIN_W - 1
    onehot = (dc_idx[:, :, None] == jnp.arange(n_dc)[None, None, :]).astype(F32)
    tiles = jnp.einsum("hdk,qck->hqdc", rpb.astype(F32), onehot, precision=lax.Precision.HIGHEST)
    tiles = jnp.where(col_ok[None, :, None, :], tiles * LOG2_E, NEG_INF)
    n_tiles = 2 * WIN_H
    neg = jnp.full((heads, GRID_W, n_tiles + 1 - n_dr, GRID_W), NEG_INF, F32)
    tiles = jnp.concatenate([tiles, neg], axis=2)
    both = jnp.stack([tiles[:, :, :n_tiles], tiles[:, :, 1:]], axis=1)
    return both.reshape(heads, 2, GRID_W, n_tiles * GRID_W)


def _route(hm, wr, br, run):
    half = V7X_LANES // 2
    h_hi = hm.astype(BF16)
    h_lo = (hm - h_hi.astype(F32)).astype(BF16)
    w_hi = wr.astype(BF16)
    w_lo = (wr - w_hi.astype(F32)).astype(BF16)
    both = _dot(h_hi, jnp.concatenate([w_hi[:, :half], w_lo[:, :half]], axis=1))
    logits = both + (pltpu.roll(both, half, 1) + _dot(h_lo, w_hi)) + br
    tm, width = logits.shape
    col = lax.broadcasted_iota(jnp.int32, (tm, width), 1).astype(F32)
    neg = jnp.float32(-3.0e38)

    gl = jnp.where(col < N_GROUPS, logits, neg)
    gmax = jnp.max(gl, axis=-1, keepdims=True)
    gsel = jnp.min(jnp.where(gl == gmax, col, float(width)), axis=-1, keepdims=True)
    p_g = 1.0 / jnp.sum(jnp.where(col < N_GROUPS, jnp.exp(gl - gmax), 0.0), axis=-1, keepdims=True)

    lo = N_GROUPS + N_EXP_PER_GROUP * gsel
    el = jnp.where((col >= lo) & (col < lo + N_EXP_PER_GROUP), logits, neg)
    e1 = jnp.max(el, axis=-1, keepdims=True)
    i1 = jnp.min(jnp.where(el == e1, col, float(width)), axis=-1, keepdims=True)
    el2 = jnp.where(col == i1, neg, el)
    e2 = jnp.max(el2, axis=-1, keepdims=True)
    i2 = jnp.min(jnp.where(el2 == e2, col, float(width)), axis=-1, keepdims=True)
    t = jnp.exp(e2 - e1)
    w1 = 1.0 / (1.0 + t)
    c1 = p_g * w1
    c2 = p_g * (t * w1)
    id1 = i1 - N_GROUPS
    id2 = i2 - N_GROUPS

    onehot = jnp.where((col == id1) | (col == id2), 1.0, 0.0)
    ri = lax.broadcasted_iota(jnp.int32, (tm, tm), 0)
    ci = lax.broadcasted_iota(jnp.int32, (tm, tm), 1)
    tri = jnp.where(ri > ci, 1.0, 0.0).astype(BF16)
    before = _dot(tri, onehot.astype(BF16)) + run
    r1 = jnp.sum(jnp.where(col == id1, before, 0.0), axis=-1, keepdims=True)
    r2 = jnp.sum(jnp.where(col == id2, before, 0.0), axis=-1, keepdims=True)

    info = jnp.where(col == 0, id1, 0.0)
    info = jnp.where(col == 1, id2, info)
    info = jnp.where(col == 2, c1, info)
    info = jnp.where(col == 3, c2, info)
    info = jnp.where(col == 4, r1, info)
    info = jnp.where(col == 5, r2, info)
    return info, run + jnp.sum(onehot, axis=0, keepdims=True)


def _pos_kernel(info_ref, off_ref, pos_ref):
    info = info_ref[...]
    tm, width = info.shape
    col = lax.broadcasted_iota(jnp.int32, (tm, width), 1).astype(F32)
    off = off_ref[...]
    p1 = jnp.sum(jnp.where(col == info[:, 0:1], off, 0.0), axis=-1, keepdims=True) + info[:, 4:5]
    p2 = jnp.sum(jnp.where(col == info[:, 1:2], off, 0.0), axis=-1, keepdims=True) + info[:, 5:6]
    pm = jnp.where(col == 0, p1, jnp.where(col == 1, p2, 0.0))
    pos_ref[0] = jnp.transpose(pm)[:SUBLANES, :].astype(jnp.int32)


def _positions(info, row_off, tm):
    m = info.shape[0]
    return pl.pallas_call(
        _pos_kernel,
        grid=(m // tm,),
        in_specs=[
            pl.BlockSpec((tm, V7X_LANES), lambda i: (i, 0)),
            pl.BlockSpec((1, V7X_LANES), lambda i: (0, 0)),
        ],
        out_specs=pl.BlockSpec((1, SUBLANES, tm), lambda i: (i, 0, 0)),
        out_shape=jax.ShapeDtypeStruct((m // tm, SUBLANES, tm), jnp.int32),
        compiler_params=_cparams(("arbitrary",), 16 * MIB),
    )(info, row_off)


def _dispatch_kernel(tm, pad_bits, ps_ref, pl_ref, pos_ref, x_ref, sc_ref, sh_ref, xs_hbm,
                     hbuf, zbuf, sems, zsem):
    i = pl.program_id(0)
    n_steps = pl.num_programs(0)
    n_experts = ps_ref.shape[0] - 1
    zrows = zbuf.shape[0]
    max_tail = (xs_hbm.shape[0] - 2 * x_ref.shape[0] * n_steps) // zrows

    def fill_copies(do):
        for e in range(n_experts):
            start = ps_ref[e]
            rem = pl_ref[e]
            for r in range(SUBLANES - 1):
                @pl.when(r < (rem & (SUBLANES - 1)))
                def _(start=start, r=r):
                    do(pltpu.make_async_copy(zbuf.at[pl.ds(0, 1)], xs_hbm.at[pl.ds(start + r, 1)], zsem))

            off = start + rem
            for bit in pad_bits:
                off = off - (rem & bit)

                @pl.when((rem & bit) != 0)
                def _(off=off, bit=bit):
                    dst = xs_hbm.at[pl.ds(pl.multiple_of(off, SUBLANES), bit)]
                    do(pltpu.make_async_copy(zbuf.at[pl.ds(0, bit)], dst, zsem))
        for t in range(max_tail):
            @pl.when(t < pl_ref[n_experts])
            def _(t=t):
                dst = xs_hbm.at[pl.ds(pl.multiple_of(ps_ref[n_experts] + t * zrows, SUBLANES), zrows)]
                do(pltpu.make_async_copy(zbuf, dst, zsem))

    @pl.when(i == 0)
    def _():
        zbuf[...] = jnp.zeros_like(zbuf)
        fill_copies(lambda cp: cp.start())

    def scatter_start(sub):
        for r in range(tm):
            for slot in range(2):
                pltpu.make_async_copy(hbuf.at[sub, pl.ds(r, 1)],
                                      xs_hbm.at[pl.ds(pos_ref[0, slot, sub * tm + r], 1)],
                                      sems.at[sub]).start(priority=slot)

    def scatter_wait(sub):
        for _ in range(2):
            pltpu.make_async_copy(hbuf.at[sub], xs_hbm.at[pl.ds(0, tm)], sems.at[sub]).wait()

    for sub in range(2):
        rows = pl.ds(sub * tm, tm)
        hbuf[sub] = _pack_halves(x_ref[rows, :] * (1.0 + sc_ref[0]) + sh_ref[0])
        scatter_start(sub)
        if sub == 0:
            @pl.when(i > 0)
            def _():
                scatter_wait(1)
        else:
            scatter_wait(0)

    @pl.when(i == n_steps - 1)
    def _():
        scatter_wait(1)
        fill_copies(lambda cp: cp.wait())


def _dispatch(x, posT, pad_start, pad_len, mods3, mod_row, layer, n_rows, tm, moe_tm):
    m, d = x.shape
    base = layer * MOD_ROWS
    step = 2 * tm
    pad_bits = tuple(1 << b for b in range(moe_tm.bit_length() - 2, SUBLANES.bit_length() - 2, -1))
    grid_spec = pltpu.PrefetchScalarGridSpec(
        num_scalar_prefetch=2,
        grid=(m // step,),
        in_specs=[
            pl.BlockSpec((1, SUBLANES, step), lambda i, ps, pn: (i, 0, 0), memory_space=pltpu.SMEM),
            pl.BlockSpec((step, d), lambda i, ps, pn: (i, 0)),
            pl.BlockSpec((1, 1, d), lambda i, ps, pn: (base + mod_row(i, step), 0, MOE_SCALE)),
            pl.BlockSpec((1, 1, d), lambda i, ps, pn: (base + mod_row(i, step), 0, MOE_SHIFT)),
        ],
        out_specs=pl.BlockSpec(memory_space=pl.ANY),
        scratch_shapes=[
            pltpu.VMEM((2, tm, d // 2), U32),
            pltpu.VMEM((moe_tm // 2, d // 2), U32),
            pltpu.SemaphoreType.DMA((2,)),
            pltpu.SemaphoreType.DMA(()),
        ],
    )
    return pl.pallas_call(
        functools.partial(_dispatch_kernel, tm, pad_bits),
        grid_spec=grid_spec,
        out_shape=jax.ShapeDtypeStruct((n_rows, d // 2), U32),
        compiler_params=_cparams(("arbitrary",), 2 * step * d * 4 + 3 * tm * d * 4 + 8 * MIB),
    )(pad_start, pad_len, posT, x, mods3, mods3)


def _moe_kernel(layer, te_ref, nu_ref, nxt_ref, par_ref, xs_ref, wg_hbm, wu_hbm, wd_hbm, ys_ref,
                wg_f, wu_f, wd_f, wg_b, wu_b, wd_b, sems):
    i = pl.program_id(0)
    used = i < nu_ref[0]
    expert = te_ref[i]
    first = used & ((i == 0) | (expert != te_ref[jnp.maximum(i - 1, 0)]))

    def weight_copies(e, slot):
        return (pltpu.make_async_copy(wg_hbm.at[layer, e], wg_f.at[slot], sems.at[slot]),
                pltpu.make_async_copy(wu_hbm.at[layer, e], wu_f.at[slot], sems.at[slot]),
                pltpu.make_async_copy(wd_hbm.at[layer, e], wd_f.at[slot], sems.at[slot]))

    @pl.when(jnp.logical_not(used))
    def _():
        ys_ref[...] = jnp.zeros_like(ys_ref)

    @pl.when(i == 0)
    def _():
        for cp in weight_copies(expert, par_ref[0]):
            cp.start(priority=1)

    def mlp(recast_slot):
        x = jnp.concatenate(_unpack_halves(xs_ref[...]), axis=1).astype(BF16)
        if recast_slot is not None:
            wg_b[...] = wg_f[recast_slot].astype(BF16)
        g = _dot(x, wg_b[...])
        if recast_slot is not None:
            wu_b[...] = wu_f[recast_slot].astype(BF16)
        u = _dot(x, wu_b[...])
        if recast_slot is not None:
            wd_b[...] = wd_f[recast_slot].astype(BF16)
        hid = (_silu(g) * u).astype(BF16)
        ys_ref[...] = _pack_halves(_dot(hid, wd_b[...]))

    @pl.when(first)
    def _():
        slot = par_ref[i]
        for cp in weight_copies(expert, slot):
            cp.wait()

        @pl.when(nxt_ref[i] >= 0)
        def _():
            for cp in weight_copies(nxt_ref[i], 1 - slot):
                cp.start(priority=1)

        mlp(slot)

    @pl.when(used & jnp.logical_not(first))
    def _():
        mlp(None)


def _moe_experts(xs, tile_expert, n_used, next_expert, slot, w_gate, w_up, w_down, layer, tm):
    n_rows, dp = xs.shape
    d = 2 * dp
    f = w_gate.shape[-1]
    n_tiles = n_rows // tm
    grid_spec = pltpu.PrefetchScalarGridSpec(
        num_scalar_prefetch=4,
        grid=(n_tiles,),
        in_specs=[
            pl.BlockSpec((tm, dp), lambda i, te, nu, nx, pr: (jnp.maximum(jnp.minimum(i, nu[0] - 1), 0), 0)),
            pl.BlockSpec(memory_space=pl.ANY),
            pl.BlockSpec(memory_space=pl.ANY),
            pl.BlockSpec(memory_space=pl.ANY),
        ],
        out_specs=pl.BlockSpec((tm, dp), lambda i, te, nu, nx, pr: (i, 0)),
        scratch_shapes=[
            pltpu.VMEM((2, d, f), F32), pltpu.VMEM((2, d, f), F32), pltpu.VMEM((2, f, d), F32),
            pltpu.VMEM((d, f), BF16), pltpu.VMEM((d, f), BF16), pltpu.VMEM((f, d), BF16),
            pltpu.SemaphoreType.DMA((2,)),
        ],
    )
    return pl.pallas_call(
        functools.partial(_moe_kernel, layer),
        grid_spec=grid_spec,
        out_shape=jax.ShapeDtypeStruct((n_rows, dp), U32),
        compiler_params=_cparams(("arbitrary",), 6 * d * f * 4 + 3 * d * f * 2 + 8 * tm * d * 4 + 8 * MIB),
    )(tile_expert, n_used, next_expert, slot, xs, w_gate, w_up, w_down)


def _combine_ln_kernel(alpha, tm, p_steps, pos_ref, nxt_ref, ys_hbm, info_ref, x_ref, gate_ref, g_ref, b_ref,
                       *rest):
    i = pl.program_id(0)
    n_steps = pl.num_programs(0)
    if p_steps is None:
        nsc_ref, nsh_ref, o_ref, hb_ref, buf, sems = rest
    else:
        op_ref, os_ref, buf, sems = rest

    par = i % 2

    def gather_start(parity, sub, p_ref):
        for r in range(tm):
            for slot in range(2):
                pltpu.make_async_copy(ys_hbm.at[pl.ds(p_ref[0, slot, sub * tm + r], 1)],
                                      buf.at[parity, sub, slot, pl.ds(r, 1)],
                                      sems.at[parity, sub]).start(priority=slot)

    def gather_wait(parity, sub):
        for slot in range(2):
            pltpu.make_async_copy(ys_hbm.at[pl.ds(0, tm)], buf.at[parity, sub, slot], sems.at[parity, sub]).wait()

    @pl.when(i == 0)
    def _():
        gather_start(0, 0, pos_ref)
        gather_start(0, 1, pos_ref)

    def step_body(parity):
        for sub in range(2):
            gather_wait(parity, sub)
            rows = pl.ds(sub * tm, tm)
            info = info_ref[rows, :]
            lo1, hi1 = _unpack_halves(buf[parity, sub, 0])
            lo2, hi2 = _unpack_halves(buf[parity, sub, 1])
            c1 = info[:, 2:3]
            c2 = info[:, 3:4]
            y = jnp.concatenate([c1 * lo1 + c2 * lo2, c1 * hi1 + c2 * hi2], axis=1)
            z = alpha * x_ref[rows, :] + gate_ref[0] * y
            res = _layer_norm(z, g_ref[0], b_ref[0])
            gather_start(1 - parity, sub, nxt_ref)
            if p_steps is None:
                o_ref[rows, :] = res
                hb_ref[rows, :] = (res * (1.0 + nsc_ref[0]) + nsh_ref[0]).astype(BF16)
            else:
                @pl.when(i < p_steps)
                def _(rows=rows, res=res):
                    op_ref[rows, :] = res

                @pl.when(i >= p_steps)
                def _(rows=rows, res=res):
                    os_ref[rows, :] = res

        @pl.when(i == n_steps - 1)
        def _():
            gather_wait(1 - parity, 0)
            gather_wait(1 - parity, 1)

    for parity in range(2):
        @pl.when(par == parity)
        def _(parity=parity):
            step_body(parity)


def _combine_ln(ys, posT, info, x, mods3, mod_row, layer, ln_g, ln_b, alpha, tm, last, p_rows):
    m, d = x.shape
    step = 2 * tm
    n_steps = m // step
    base = layer * MOD_ROWS
    in_specs = [
        pl.BlockSpec((1, SUBLANES, step), lambda i: (i, 0, 0), memory_space=pltpu.SMEM),
        pl.BlockSpec((1, SUBLANES, step), lambda i: (jnp.minimum(i + 1, n_steps - 1), 0, 0), memory_space=pltpu.SMEM),
        pl.BlockSpec(memory_space=pl.ANY),
        pl.BlockSpec((step, V7X_LANES), lambda i: (i, 0)),
        pl.BlockSpec((step, d), lambda i: (i, 0)),
        pl.BlockSpec((1, 1, d), lambda i: (base + mod_row(i, step), 0, MOE_GATE)),
        pl.BlockSpec((1, 1, d), lambda i: (layer, 0, 0)),
        pl.BlockSpec((1, 1, d), lambda i: (layer, 0, 0)),
    ]
    args = [posT, posT, ys, info, x, mods3, ln_g, ln_b]
    if last:
        p_steps = p_rows // step
        out_specs = [
            pl.BlockSpec((step, d), lambda i: (jnp.minimum(i, p_steps - 1), 0)),
            pl.BlockSpec((step, d), lambda i: (jnp.maximum(i - p_steps, 0), 0)),
        ]
        out_shape = [jax.ShapeDtypeStruct((p_rows, d), F32), jax.ShapeDtypeStruct((m - p_rows, d), F32)]
    else:
        p_steps = None
        nbase = (layer + 1) * MOD_ROWS
        in_specs += [
            pl.BlockSpec((1, 1, d), lambda i: (nbase + mod_row(i, step), 0, MIX_SCALE)),
            pl.BlockSpec((1, 1, d), lambda i: (nbase + mod_row(i, step), 0, MIX_SHIFT)),
        ]
        args += [mods3, mods3]
        out_specs = [pl.BlockSpec((step, d), lambda i: (i, 0)), pl.BlockSpec((step, d), lambda i: (i, 0))]
        out_shape = [jax.ShapeDtypeStruct((m, d), F32), jax.ShapeDtypeStruct((m, d), BF16)]
    return pl.pallas_call(
        functools.partial(_combine_ln_kernel, alpha, tm, p_steps),
        grid=(n_steps,),
        in_specs=in_specs,
        out_specs=out_specs,
        out_shape=out_shape,
        scratch_shapes=[pltpu.VMEM((2, 2, 2, tm, d // 2), U32), pltpu.SemaphoreType.DMA((2, 2))],
        compiler_params=_cparams(("arbitrary",), 6 * step * d * 4 + 4 * tm * d * 4 + 6 * tm * d * 4 + 8 * MIB),
    )(*args)


def _router_params(layer, w_rg, b_rg, w_re, b_re):
    d = w_rg.shape[1]
    n_e = N_EXPERTS
    pad = V7X_LANES - N_GROUPS - n_e
    assert N_GROUPS + n_e <= V7X_LANES // 2
    wr = jnp.concatenate(
        [w_rg[layer], w_re[layer].transpose(1, 0, 2).reshape(d, n_e), jnp.zeros((d, pad), F32)], axis=1)
    br = jnp.concatenate([b_rg[layer], b_re[layer].reshape(n_e), jnp.zeros((pad,), F32)])[None, :]
    return wr, br


def _hier_moe_ln(x, info, cnt, mods3, mod_row, layer, w_gate, w_up, w_down, ln_g, ln_b, alpha, last, p_rows):
    m, d = x.shape
    n_e = N_EXPERTS

    tm = MOE_TM
    n_tiles = 2 * m // tm + n_e
    counts = cnt[0, :n_e].astype(jnp.int32)
    tiles_e = (counts + tm - 1) // tm
    tile_end = jnp.cumsum(tiles_e)
    n_used = tile_end[-1]
    row_off = (tile_end - tiles_e) * tm
    tile_ids = jnp.minimum(jnp.arange(n_tiles, dtype=jnp.int32), n_used - 1)
    tile_expert = jnp.sum((tile_ids[:, None] >= tile_end[None, :]).astype(jnp.int32), axis=1)
    group_end = tile_end[tile_expert]
    next_expert = jnp.where(group_end < n_used, tile_expert[jnp.minimum(group_end, n_tiles - 1)], -1)
    is_first = jnp.concatenate([jnp.ones((1,), jnp.int32), (tile_expert[1:] != tile_expert[:-1]).astype(jnp.int32)])
    slot = (jnp.cumsum(is_first) - 1) % 2
    zrows = tm // 2
    pad_start = jnp.concatenate([row_off + counts, (n_used * tm)[None]])
    pad_len = jnp.concatenate([tiles_e * tm - counts, ((n_tiles - n_used) * (tm // zrows))[None]])
    row_off_f = jnp.concatenate([row_off.astype(F32), jnp.zeros((V7X_LANES - n_e,), F32)])[None, :]

    ct = COMB_TM
    posT = _positions(info, row_off_f, 2 * ct)
    xs = _dispatch(x, posT, pad_start, pad_len, mods3, mod_row, layer, n_tiles * tm, ct, tm)
    ys = _moe_experts(xs, tile_expert, n_used.reshape(1), next_expert.astype(jnp.int32), slot.astype(jnp.int32),
                      w_gate, w_up, w_down, layer, tm)
    return _combine_ln(ys, posT, info, x, mods3, mod_row, layer, ln_g, ln_b, alpha, ct, last, p_rows)


def kernel(x_prompt, x_sample, state_ret, cache_na_k, cache_na_v, c, c_ctx, w_mod, b_mod, ln1_g, ln1_b, ln2_g, ln2_b, w_ret_in, ret_decay_logit, ret_gn_g, ret_gn_b, w_ret_out, w_na_in, na_rpb, w_na_out, w_rg, b_rg, w_re, b_re, w_gate, w_up, w_down):
    bp, seq, d = x_prompt.shape
    bs, dec_seq, _ = x_sample.shape
    depth = w_mod.shape[0]
    slab = dec_seq
    assert slab % seq == 0 and (bp * seq) % slab == 0 and bs + 1 <= MOD_ROWS
    p_rows = bp * seq
    p_slabs = p_rows // slab
    alpha = (2.0 * depth) ** 0.25

    def mod_row(i, tm):
        start = i * tm
        return jnp.where(start < p_rows, 0, 1 + (start - p_rows) // dec_seq)

    cc = jnp.concatenate([c_ctx[None, :], c, jnp.zeros((MOD_ROWS - 1 - bs, d), F32)], axis=0)
    mods = _modulation(cc, w_mod, b_mod)
    mods3 = mods.reshape(depth * MOD_ROWS, 1, 6 * d)
    x, hb = _prologue(x_prompt.reshape(p_rows, d), x_sample.reshape(bs * dec_seq, d), mods3, mod_row)

    ln1_g3, ln1_b3 = ln1_g.reshape(depth, 1, d), ln1_b.reshape(depth, 1, d)
    ln2_g3, ln2_b3 = ln2_g.reshape(depth, 1, d), ln2_b.reshape(depth, 1, d)

    new_ret, new_k, new_v = [], [], []
    for l in range(depth):
        j = l // 2
        if l % 2 == 0:
            dk = state_ret.shape[-2]
            proj = _inproj(hb, w_ret_in, j)
            log_g = -jax.nn.softplus(-ret_decay_logit[j].astype(F32))
            t = jnp.arange(slab)
            nf = dk // 4
            inv_freq = ROPE_BASE ** (-jnp.arange(nf, dtype=F32) / nf)
            ang_r = (t // GRID_W).astype(F32)[:, None] * inv_freq
            ang_c = (t % GRID_W).astype(F32)[:, None] * inv_freq
            cos = jnp.concatenate([jnp.cos(ang_r)] * 2 + [jnp.cos(ang_c)] * 2, axis=1)
            sin = jnp.concatenate([-jnp.sin(ang_r), jnp.sin(ang_r), -jnp.sin(ang_c), jnp.sin(ang_c)], axis=1)
            y, st = _retention(proj, log_g, cos, sin, state_ret, j, ret_gn_g[j][:, None, :],
                               ret_gn_b[j][:, None, :], p_slabs, slab)
            new_ret.append(st)
            mixed, w_out = y, w_ret_out[j]
        else:
            proj = _inproj(hb, w_na_in, j)
            o, nk, nv = _attention(proj, cache_na_k, cache_na_v, j, _na_strips(na_rpb[j]), p_slabs, slab, seq)
            new_k.append(nk[:, 0])
            new_v.append(nv[:, 0])
            mixed, w_out = o, w_na_out[j]
        wr, br = _router_params(l, w_rg, b_rg, w_re, b_re)
        x, info, cnt = _outproj_ln(mixed, w_out.astype(BF16), x, mods3, mod_row, l, ln1_g3, ln1_b3, wr, br, alpha)
        x, hb = _hier_moe_ln(x, info, cnt, mods3, mod_row, l, w_gate, w_up, w_down, ln2_g3, ln2_b3, alpha,
                             l == depth - 1, p_rows)

    y_prompt = x.reshape(bp, seq, d)
    y_sample = hb.reshape(bs, dec_seq, d)
    return (y_prompt, y_sample, jnp.stack(new_ret, axis=1), jnp.stack(new_k, axis=1), jnp.stack(new_v, axis=1))
```

```python
import functools

import jax
import jax.numpy as jnp
from jax import lax
from jax.experimental import pallas as pl
from jax.experimental.pallas import tpu as pltpu

F32 = jnp.float32
BF16 = jnp.bfloat16
MIB = 1024 * 1024

GRID_W = 64
WIN_H = 8
WIN_W = 16
N_GROUPS = 4
N_EXP_PER_GROUP = 8
N_EXPERTS = N_GROUPS * N_EXP_PER_GROUP
ROPE_BASE = 10000.0
LN_EPS = 1e-5
NEG_INF = -1e30
LOG2_E = 1.4426950408889634

V7X_VMEM_BYTES = 64 * MIB
V7X_LANES = 128
SUBLANES = 8
MOD_ROWS = 8
MIX_SHIFT, MIX_SCALE, MIX_GATE, MOE_SHIFT, MOE_SCALE, MOE_GATE = range(6)

RET_CHUNK = 256
ATTN_QB = 256
ATTN_HEADS_PER_STEP = 4
MOE_TM = 256
COMB_TM = 256


def _cparams(sem, vmem_bytes):
    limit = int(min(max(vmem_bytes, 16 * MIB), V7X_VMEM_BYTES - 6 * MIB))
    return pltpu.CompilerParams(dimension_semantics=sem, vmem_limit_bytes=limit)


def _silu(x):
    h = 0.5 * x
    return h + h * jnp.tanh(h)


def _dot(a, b):
    return jnp.dot(a, b, preferred_element_type=F32)


def _dot_nt(a, b):
    return lax.dot_general(a, b, (((1,), (1,)), ((), ())), preferred_element_type=F32)


def _layer_norm(z, g, b):
    mu = jnp.mean(z, axis=-1, keepdims=True)
    zc = z - mu
    var = jnp.mean(zc * zc, axis=-1, keepdims=True)
    return zc * lax.rsqrt(var + LN_EPS) * g + b


def _mod_kernel(c_ref, w_ref, b_ref, o_ref):
    s = _silu(c_ref[...]).astype(BF16)
    o_ref[0] = _dot(s, w_ref[0].astype(BF16)) + b_ref[0]


def _modulation(cc, w_mod, b_mod):
    depth, d, n6 = w_mod.shape
    tn = 1024
    return pl.pallas_call(
        _mod_kernel,
        grid=(depth, n6 // tn),
        in_specs=[
            pl.BlockSpec((MOD_ROWS, d), lambda l, j: (0, 0)),
            pl.BlockSpec((1, d, tn), lambda l, j: (l, 0, j)),
            pl.BlockSpec((1, 1, tn), lambda l, j: (l, 0, j)),
        ],
        out_specs=pl.BlockSpec((1, MOD_ROWS, tn), lambda l, j: (l, 0, j)),
        out_shape=jax.ShapeDtypeStruct((depth, MOD_ROWS, n6), F32),
        compiler_params=_cparams(("arbitrary", "arbitrary"), 3 * d * tn * 4 + 4 * MIB),
    )(cc, w_mod, b_mod.reshape(depth, 1, n6))


def _prologue_kernel(p_tiles, xp_ref, xs_ref, sc_ref, sh_ref, x_ref, hb_ref):
    @pl.when(pl.program_id(0) < p_tiles)
    def _():
        x = xp_ref[...]
        x_ref[...] = x
        hb_ref[...] = (x * (1.0 + sc_ref[0]) + sh_ref[0]).astype(BF16)

    @pl.when(pl.program_id(0) >= p_tiles)
    def _():
        x = xs_ref[...]
        x_ref[...] = x
        hb_ref[...] = (x * (1.0 + sc_ref[0]) + sh_ref[0]).astype(BF16)


def _prologue(xp, xs, mods3, mod_row, tm=512):
    p_rows, d = xp.shape
    m = p_rows + xs.shape[0]
    p_tiles = p_rows // tm
    return pl.pallas_call(
        functools.partial(_prologue_kernel, p_tiles),
        grid=(m // tm,),
        in_specs=[
            pl.BlockSpec((tm, d), lambda i: (jnp.minimum(i, p_tiles - 1), 0)),
            pl.BlockSpec((tm, d), lambda i: (jnp.maximum(i - p_tiles, 0), 0)),
            pl.BlockSpec((1, 1, d), lambda i: (mod_row(i, tm), 0, MIX_SCALE)),
            pl.BlockSpec((1, 1, d), lambda i: (mod_row(i, tm), 0, MIX_SHIFT)),
        ],
        out_specs=[pl.BlockSpec((tm, d), lambda i: (i, 0)), pl.BlockSpec((tm, d), lambda i: (i, 0))],
        out_shape=[jax.ShapeDtypeStruct((m, d), F32), jax.ShapeDtypeStruct((m, d), BF16)],
        compiler_params=_cparams(("arbitrary",), 8 * tm * d * 4 + 8 * MIB),
    )(xp, xs, mods3, mods3)


def _inproj_kernel(h_ref, w_ref, o_ref):
    o_ref[...] = _dot(h_ref[...], w_ref[...].astype(BF16))


def _inproj(hb, w_all, w_idx, tm=2048, tn=512):
    m, d = hb.shape
    n = w_all.shape[-1]
    return pl.pallas_call(
        _inproj_kernel,
        grid=(m // tm, n // tn),
        in_specs=[
            pl.BlockSpec((tm, d), lambda i, j: (i, 0)),
            pl.BlockSpec((None, d, tn), lambda i, j: (w_idx, 0, j)),
        ],
        out_specs=pl.BlockSpec((tm, tn), lambda i, j: (i, j)),
        out_shape=jax.ShapeDtypeStruct((m, n), F32),
        compiler_params=_cparams(
            ("arbitrary", "arbitrary"),
            2 * tm * d * 2 + 2 * d * tn * 4 + d * tn * 2 + 3 * tm * tn * 4 + 4 * MIB),
    )(hb, w_all)


def _outproj_ln_kernel(alpha, y_ref, w_ref, x_ref, gate_ref, g_ref, b_ref, sc_ref, sh_ref, wr_ref, br_ref,
                       o_ref, info_ref, cnt_ref, run_ref):
    @pl.when(pl.program_id(0) == 0)
    def _():
        run_ref[...] = jnp.zeros_like(run_ref)

    z = alpha * x_ref[...] + gate_ref[0] * _dot(y_ref[...], w_ref[...])
    x1 = _layer_norm(z, g_ref[0], b_ref[0])
    o_ref[...] = x1
    info, run = _route(x1 * (1.0 + sc_ref[0]) + sh_ref[0], wr_ref[...], br_ref[...], run_ref[...])
    info_ref[...] = info
    run_ref[...] = run
    cnt_ref[...] = run


def _outproj_ln(y, w, x, mods3, mod_row, layer, ln_g, ln_b, wr, br, alpha, tm=512):
    m, kdim = y.shape
    d = x.shape[1]
    base = layer * MOD_ROWS
    return pl.pallas_call(
        functools.partial(_outproj_ln_kernel, alpha),
        grid=(m // tm,),
        in_specs=[
            pl.BlockSpec((tm, kdim), lambda i: (i, 0)),
            pl.BlockSpec((kdim, d), lambda i: (0, 0)),
            pl.BlockSpec((tm, d), lambda i: (i, 0)),
            pl.BlockSpec((1, 1, d), lambda i: (base + mod_row(i, tm), 0, MIX_GATE)),
            pl.BlockSpec((1, 1, d), lambda i: (layer, 0, 0)),
            pl.BlockSpec((1, 1, d), lambda i: (layer, 0, 0)),
            pl.BlockSpec((1, 1, d), lambda i: (base + mod_row(i, tm), 0, MOE_SCALE)),
            pl.BlockSpec((1, 1, d), lambda i: (base + mod_row(i, tm), 0, MOE_SHIFT)),
            pl.BlockSpec((d, V7X_LANES), lambda i: (0, 0)),
            pl.BlockSpec((1, V7X_LANES), lambda i: (0, 0)),
        ],
        out_specs=[
            pl.BlockSpec((tm, d), lambda i: (i, 0)),
            pl.BlockSpec((tm, V7X_LANES), lambda i: (i, 0)),
            pl.BlockSpec((1, V7X_LANES), lambda i: (0, 0)),
        ],
        out_shape=[
            jax.ShapeDtypeStruct((m, d), F32),
            jax.ShapeDtypeStruct((m, V7X_LANES), F32),
            jax.ShapeDtypeStruct((1, V7X_LANES), F32),
        ],
        scratch_shapes=[pltpu.VMEM((1, V7X_LANES), F32)],
        compiler_params=_cparams(
            ("arbitrary",), kdim * d * 2 + 2 * tm * kdim * 2 + 9 * tm * d * 4 + d * V7X_LANES * 4 + 4 * MIB),
    )(y, w, x, mods3, ln_g, ln_b, mods3, mods3, wr, br)


def _rope(x, cos, sin):
    half = V7X_LANES // 2
    sw = jnp.concatenate(
        [pltpu.roll(x[:, :V7X_LANES], half, 1), pltpu.roll(x[:, V7X_LANES:], half, 1)], axis=1)
    return x * cos + sw * sin


def _group_norm(o, g, b):
    mu = jnp.mean(o, axis=-1, keepdims=True)
    oc = o - mu
    var = jnp.mean(oc * oc, axis=-1, keepdims=True)
    return oc * lax.rsqrt(var + LN_EPS) * g + b


def _ret_kernel(p_slabs, dk, lg_ref, q_ref, k_ref, v_ref, gf_ref, gb_ref, cos_ref, sin_ref, s0_ref,
                gng_ref, gnb_ref, y_ref, st_ref, q_s, k_s, yacc):
    h = pl.program_id(0)
    s = pl.program_id(1)
    is_prompt = s < p_slabs
    c = RET_CHUNK
    n_chunks = q_ref.shape[0] // c

    def run(prompt):
        ri = lax.broadcasted_iota(jnp.int32, (c, c), 0)
        ci = lax.broadcasted_iota(jnp.int32, (c, c), 1)
        diff = (ri - ci).astype(F32)
        pos = lax.broadcasted_iota(jnp.int32, (c, 1), 0).astype(F32)
        gn_g = gng_ref[0]
        gn_b = gnb_ref[0]
        scale = dk ** -0.5
        if not prompt:
            q_s[...] = _rope(q_ref[...], cos_ref[...], sin_ref[...])
            k_s[...] = _rope(k_ref[...], cos_ref[...], sin_ref[...]) * scale

        for direction in (0, 1):
            lg = jnp.full((1, 1), lg_ref[direction, h], F32)
            if direction == 0:
                mask = jnp.where(diff >= 0, jnp.exp(jnp.maximum(diff, 0.0) * lg), 0.0)
                q_dec = jnp.exp((pos + 1.0) * lg)
                k_dec = jnp.exp((c - 1.0 - pos) * lg)
                order = range(n_chunks)
                gate_ref = gf_ref
            else:
                mask = jnp.where(diff <= 0, jnp.exp(jnp.maximum(-diff, 0.0) * lg), 0.0)
                q_dec = jnp.exp((c - pos) * lg)
                k_dec = jnp.exp(pos * lg)
                order = range(n_chunks - 1, -1, -1)
                gate_ref = gb_ref
            c_dec = jnp.exp(c * lg)
            state = None if prompt else s0_ref[0, direction, 0]
            for ch in order:
                rows = pl.ds(ch * c, c)
                if prompt:
                    qc = q_ref[rows, :]
                    kc = k_ref[rows, :] * scale
                else:
                    qc = q_s[rows, :]
                    kc = k_s[rows, :]
                vb = v_ref[rows, :].astype(BF16)
                inner = _dot_nt(qc.astype(BF16), kc.astype(BF16)) * mask
                o = _dot(inner.astype(BF16), vb)
                kv = _dot(jnp.transpose(kc * k_dec).astype(BF16), vb)
                if prompt:
                    st_ref[ch, direction, 0] = kv
                else:
                    o = o + _dot((qc * q_dec).astype(BF16), state.astype(BF16))
                    state = c_dec * state + kv
                contrib = _group_norm(o, gn_g, gn_b) * _silu(gate_ref[rows, :])
                if direction == 0:
                    yacc[rows, :] = contrib
                else:
                    y_ref[rows, :] = (yacc[rows, :] + contrib).astype(BF16)

    @pl.when(is_prompt)
    def _():
        run(True)

    @pl.when(jnp.logical_not(is_prompt))
    def _():
        run(False)


def _retention(proj, log_g, rope_cos, rope_sin, state_ret, ret_idx, gn_g, gn_b, p_slabs, slab):
    m = proj.shape[0]
    heads, dk, dv = state_ret.shape[-3:]
    n_slabs = m // slab
    per_slab = slab // RET_CHUNK
    v_blk0 = 2 * heads * dk // dv
    p_last = p_slabs - 1

    def s0_map(h, s):
        return (jnp.maximum(s - p_slabs, 0), ret_idx, 0, h, 0, 0)

    kernel = functools.partial(_ret_kernel, p_slabs, dk)
    return pl.pallas_call(
        kernel,
        grid=(heads, n_slabs),
        in_specs=[
            pl.BlockSpec(memory_space=pltpu.SMEM),
            pl.BlockSpec((slab, dk), lambda h, s: (s, h)),
            pl.BlockSpec((slab, dk), lambda h, s: (s, heads + h)),
            pl.BlockSpec((slab, dv), lambda h, s: (s, v_blk0 + h)),
            pl.BlockSpec((slab, dv), lambda h, s: (s, v_blk0 + heads + h)),
            pl.BlockSpec((slab, dv), lambda h, s: (s, v_blk0 + 2 * heads + h)),
            pl.BlockSpec((slab, dk), lambda h, s: (0, 0)),
            pl.BlockSpec((slab, dk), lambda h, s: (0, 0)),
            pl.BlockSpec((1, None, 2, 1, dk, dv), s0_map),
            pl.BlockSpec((1, 1, dv), lambda h, s: (h, 0, 0)),
            pl.BlockSpec((1, 1, dv), lambda h, s: (h, 0, 0)),
        ],
        out_specs=[
            pl.BlockSpec((slab, dv), lambda h, s: (s, h)),
            pl.BlockSpec((per_slab, 2, 1, dk, dv), lambda h, s: (jnp.minimum(s, p_last), 0, h, 0, 0)),
        ],
        out_shape=[
            jax.ShapeDtypeStruct((m, heads * dv), BF16),
            jax.ShapeDtypeStruct((p_slabs * per_slab, 2, heads, dk, dv), F32),
        ],
        scratch_shapes=[
            pltpu.VMEM((slab, dk), F32),
            pltpu.VMEM((slab, dk), F32),
            pltpu.VMEM((slab, dv), F32),
        ],
        compiler_params=_cparams(
            ("arbitrary", "arbitrary"),
            2 * (2 * slab * dk + 3 * slab * dv) * 4 + 4 * slab * dk * 4 + 2 * 2 * dk * dv * 4
            + 2 * slab * dv * 2 + 2 * per_slab * 2 * dk * dv * 4 + 2 * slab * dk * 4 + slab * dv * 4
            + 8 * MIB),
    )(log_g, proj, proj, proj, proj, proj, rope_cos, rope_sin, state_ret, gn_g, gn_b)


def _na_window(rq, rows):
    kh = min(WIN_H, rows)
    return min(max(rq - kh // 2, 0), rows - kh), kh


def _attn_kernel(p_slabs, seq, dh, q_ref, k_ref, v_ref, ck_ref, cv_ref, strip_ref, o_ref, nk_ref, nv_ref):
    s = pl.program_id(1)
    is_prompt = s < p_slabs
    slab = q_ref.shape[0]
    scale = dh ** -0.5 * LOG2_E

    def prompt_head(hh):
        cols = pl.ds(hh * dh, dh)
        for b in range(slab // seq):
            rows = pl.ds(b * seq, seq)
            k = k_ref[rows, cols]
            v = v_ref[rows, cols]
            nk_ref[b, 0, hh] = k
            nv_ref[b, 0, hh] = v
            q = (q_ref[rows, cols] * scale).astype(BF16)
            sc = _dot_nt(q, k.astype(BF16))
            p = jnp.exp2(sc - jnp.max(sc, axis=-1, keepdims=True))
            denom = jnp.sum(p, axis=-1, keepdims=True)
            o_ref[rows, cols] = (_dot(p.astype(BF16), v.astype(BF16)) / denom).astype(BF16)

    def latent_head(hh):
        cols = pl.ds(hh * dh, dh)
        grid_rows = slab // GRID_W
        rows_per_block = ATTN_QB // GRID_W
        ckb = ck_ref[0, 0, hh].astype(BF16)
        cvb = cv_ref[0, 0, hh].astype(BF16)
        for qb in range(slab // ATTN_QB):
            rq0 = qb * rows_per_block
            ka = _na_window(rq0, grid_rows)[0] // 2 * 2
            last0, kh = _na_window(rq0 + rows_per_block - 1, grid_rows)
            kb = -((last0 + kh) // -2) * 2
            n_keys = (kb - ka) * GRID_W
            keys = pl.ds(ka * GRID_W, n_keys)
            pieces = []
            for i in range(rows_per_block):
                rq = rq0 + i
                r0, kh = _na_window(rq, grid_rows)
                first = ka - rq + WIN_H - 1
                piece = strip_ref[hh, first % 2, :, pl.ds((first - first % 2) * GRID_W, n_keys)]
                if r0 != ka or r0 + kh != kb:
                    key_row = ka + (lax.broadcasted_iota(jnp.int32, piece.shape, 1) // GRID_W)
                    piece = jnp.where((key_row >= r0) & (key_row < r0 + kh), piece, NEG_INF)
                pieces.append(piece)
            bias = jnp.concatenate(pieces, axis=0)
            rows = pl.ds(qb * ATTN_QB, ATTN_QB)
            q = (q_ref[rows, cols] * scale).astype(BF16)
            s_loc = _dot_nt(q, k_ref[keys, cols].astype(BF16)) + bias
            s_ctx = _dot_nt(q, ckb)
            mx = jnp.maximum(jnp.max(s_loc, axis=-1, keepdims=True), jnp.max(s_ctx, axis=-1, keepdims=True))
            p_loc = jnp.exp2(s_loc - mx)
            p_ctx = jnp.exp2(s_ctx - mx)
            denom = jnp.sum(p_loc, axis=-1, keepdims=True) + jnp.sum(p_ctx, axis=-1, keepdims=True)
            o = _dot(p_loc.astype(BF16), v_ref[keys, cols].astype(BF16)) + _dot(p_ctx.astype(BF16), cvb)
            o_ref[rows, cols] = (o / denom).astype(BF16)

    heads_per_step = q_ref.shape[1] // dh

    @pl.when(is_prompt)
    def _():
        for hh in range(heads_per_step):
            prompt_head(hh)

    @pl.when(jnp.logical_not(is_prompt))
    def _():
        for hh in range(heads_per_step):
            latent_head(hh)


def _attention(proj, cache_k, cache_v, cache_idx, strips, p_slabs, slab, seq):
    m = proj.shape[0]
    heads = cache_k.shape[2]
    dh = cache_k.shape[-1]
    past = cache_k.shape[-2]
    n_slabs = m // slab
    per_slab = slab // seq
    p_last = p_slabs - 1

    hp = ATTN_HEADS_PER_STEP
    groups = heads // hp

    def ctx_map(h, s):
        return (jnp.maximum(s - p_slabs, 0), cache_idx, h, 0, 0)

    def new_map(h, s):
        return (jnp.minimum(s, p_last), 0, h, 0, 0)

    kernel = functools.partial(_attn_kernel, p_slabs, seq, dh)
    return pl.pallas_call(
        kernel,
        grid=(groups, n_slabs),
        in_specs=[
            pl.BlockSpec((slab, hp * dh), lambda h, s: (s, h)),
            pl.BlockSpec((slab, hp * dh), lambda h, s: (s, groups + h)),
            pl.BlockSpec((slab, hp * dh), lambda h, s: (s, 2 * groups + h)),
            pl.BlockSpec((1, 1, hp, past, dh), ctx_map),
            pl.BlockSpec((1, 1, hp, past, dh), ctx_map),
            pl.BlockSpec((hp,) + strips.shape[1:], lambda h, s: (h, 0, 0, 0)),
        ],
        out_specs=[
            pl.BlockSpec((slab, hp * dh), lambda h, s: (s, h)),
            pl.BlockSpec((per_slab, 1, hp, seq, dh), new_map),
            pl.BlockSpec((per_slab, 1, hp, seq, dh), new_map),
        ],
        out_shape=[
            jax.ShapeDtypeStruct((m, heads * dh), BF16),
            jax.ShapeDtypeStruct((p_slabs * per_slab, 1, heads, seq, dh), F32),
            jax.ShapeDtypeStruct((p_slabs * per_slab, 1, heads, seq, dh), F32),
        ],
        compiler_params=_cparams(
            ("arbitrary", "arbitrary"),
            hp * (2 * strips.shape[1] * strips.shape[2] * strips.shape[3] * 4 + 16 * slab * dh * 4)
            + 10 * ATTN_QB * (slab + past) * 4 + 8 * MIB),
    )(proj, proj, proj, cache_k, cache_v, strips)


def _na_strips(rpb):
    heads, n_dr, n_dc = rpb.shape
    col = jnp.arange(GRID_W)
    c0 = jnp.clip(col - WIN_W // 2, 0, GRID_W - WIN_W)
    col_ok = (col[None, :] >= c0[:, None]) & (col[None, :] < c0[:, None] + WIN_W)
    dc_idx = jnp.clip(col[None, :] - col[:, None], -(WIN_W - 1), WIN_W - 1) + WIN_W - 1
    onehot = (dc_idx[:, :, None] == jnp.arange(n_dc)[None, None, :]).astype(F32)
    tiles = jnp.einsum("hdk,qck->hqdc", rpb.astype(F32), onehot, precision=lax.Precision.HIGHEST)
    tiles = jnp.where(col_ok[None, :, None, :], tiles * LOG2_E, NEG_INF)
    n_tiles = 2 * WIN_H
    neg = jnp.full((heads, GRID_W, n_tiles + 1 - n_dr, GRID_W), NEG_INF, F32)
    tiles = jnp.concatenate([tiles, neg], axis=2)
    both = jnp.stack([tiles[:, :, :n_tiles], tiles[:, :, 1:]], axis=1)
    return both.reshape(heads, 2, GRID_W, n_tiles * GRID_W)


def _route(hm, wr, br, run):
    half = V7X_LANES // 2
    h_hi = hm.astype(BF16)
    h_lo = (hm - h_hi.astype(F32)).astype(BF16)
    w_hi = wr.astype(BF16)
    w_lo = (wr - w_hi.astype(F32)).astype(BF16)
    both = _dot(h_hi, jnp.concatenate([w_hi[:, :half], w_lo[:, :half]], axis=1))
    logits = both + (pltpu.roll(both, half, 1) + _dot(h_lo, w_hi)) + br
    tm, width = logits.shape
    col = lax.broadcasted_iota(jnp.int32, (tm, width), 1).astype(F32)
    neg = jnp.float32(-3.0e38)

    gl = jnp.where(col < N_GROUPS, logits, neg)
    gmax = jnp.max(gl, axis=-1, keepdims=True)
    gsel = jnp.min(jnp.where(gl == gmax, col, float(width)), axis=-1, keepdims=True)
    p_g = 1.0 / jnp.sum(jnp.where(col < N_GROUPS, jnp.exp(gl - gmax), 0.0), axis=-1, keepdims=True)

    lo = N_GROUPS + N_EXP_PER_GROUP * gsel
    el = jnp.where((col >= lo) & (col < lo + N_EXP_PER_GROUP), logits, neg)
    e1 = jnp.max(el, axis=-1, keepdims=True)
    i1 = jnp.min(jnp.where(el == e1, col, float(width)), axis=-1, keepdims=True)
    el2 = jnp.where(col == i1, neg, el)
    e2 = jnp.max(el2, axis=-1, keepdims=True)
    i2 = jnp.min(jnp.where(el2 == e2, col, float(width)), axis=-1, keepdims=True)
    t = jnp.exp(e2 - e1)
    w1 = 1.0 / (1.0 + t)
    c1 = p_g * w1
    c2 = p_g * (t * w1)
    id1 = i1 - N_GROUPS
    id2 = i2 - N_GROUPS

    onehot = jnp.where((col == id1) | (col == id2), 1.0, 0.0)
    ri = lax.broadcasted_iota(jnp.int32, (tm, tm), 0)
    ci = lax.broadcasted_iota(jnp.int32, (tm, tm), 1)
    tri = jnp.where(ri > ci, 1.0, 0.0).astype(BF16)
    before = _dot(tri, onehot.astype(BF16)) + run
    r1 = jnp.sum(jnp.where(col == id1, before, 0.0), axis=-1, keepdims=True)
    r2 = jnp.sum(jnp.where(col == id2, before, 0.0), axis=-1, keepdims=True)

    info = jnp.where(col == 0, id1, 0.0)
    info = jnp.where(col == 1, id2, info)
    info = jnp.where(col == 2, c1, info)
    info = jnp.where(col == 3, c2, info)
    info = jnp.where(col == 4, r1, info)
    info = jnp.where(col == 5, r2, info)
    return info, run + jnp.sum(onehot, axis=0, keepdims=True)


def _pos_kernel(info_ref, off_ref, pos_ref):
    info = info_ref[...]
    tm, width = info.shape
    col = lax.broadcasted_iota(jnp.int32, (tm, width), 1).astype(F32)
    off = off_ref[...]
    p1 = jnp.sum(jnp.where(col == info[:, 0:1], off, 0.0), axis=-1, keepdims=True) + info[:, 4:5]
    p2 = jnp.sum(jnp.where(col == info[:, 1:2], off, 0.0), axis=-1, keepdims=True) + info[:, 5:6]
    pm = jnp.where(col == 0, p1, jnp.where(col == 1, p2, 0.0))
    pos_ref[0] = jnp.transpose(pm)[:SUBLANES, :].astype(jnp.int32)


def _positions(info, row_off, tm):
    m = info.shape[0]
    return pl.pallas_call(
        _pos_kernel,
        grid=(m // tm,),
        in_specs=[
            pl.BlockSpec((tm, V7X_LANES), lambda i: (i, 0)),
            pl.BlockSpec((1, V7X_LANES), lambda i: (0, 0)),
        ],
        out_specs=pl.BlockSpec((1, SUBLANES, tm), lambda i: (i, 0, 0)),
        out_shape=jax.ShapeDtypeStruct((m // tm, SUBLANES, tm), jnp.int32),
        compiler_params=_cparams(("arbitrary",), 16 * MIB),
    )(info, row_off)


def _dispatch_kernel(tm, pad_bits, ps_ref, pl_ref, pos_ref, x_ref, sc_ref, sh_ref, xs_hbm,
                     hbuf, zbuf, sems, zsem):
    i = pl.program_id(0)
    n_steps = pl.num_programs(0)
    n_experts = ps_ref.shape[0] - 1
    zrows = zbuf.shape[0]
    max_tail = (xs_hbm.shape[0] - 2 * x_ref.shape[0] * n_steps) // zrows

    def fill_copies(do):
        for e in range(n_experts):
            start = ps_ref[e]
            rem = pl_ref[e]
            for r in range(SUBLANES - 1):
                @pl.when(r < (rem & (SUBLANES - 1)))
                def _(start=start, r=r):
                    do(pltpu.make_async_copy(zbuf.at[pl.ds(0, 1)], xs_hbm.at[pl.ds(start + r, 1)], zsem))

            off = start + rem
            for bit in pad_bits:
                off = off - (rem & bit)

                @pl.when((rem & bit) != 0)
                def _(off=off, bit=bit):
                    dst = xs_hbm.at[pl.ds(pl.multiple_of(off, SUBLANES), bit)]
                    do(pltpu.make_async_copy(zbuf.at[pl.ds(0, bit)], dst, zsem))
        for t in range(max_tail):
            @pl.when(t < pl_ref[n_experts])
            def _(t=t):
                dst = xs_hbm.at[pl.ds(pl.multiple_of(ps_ref[n_experts] + t * zrows, SUBLANES), zrows)]
                do(pltpu.make_async_copy(zbuf, dst, zsem))

    @pl.when(i == 0)
    def _():
        zbuf[...] = jnp.zeros_like(zbuf)
        fill_copies(lambda cp: cp.start())

    def scatter_start(sub):
        for r in range(tm):
            for slot in range(2):
                pltpu.make_async_copy(hbuf.at[sub, pl.ds(r, 1)],
                                      xs_hbm.at[pl.ds(pos_ref[0, slot, sub * tm + r], 1)],
                                      sems.at[sub]).start(priority=slot)

    def scatter_wait(sub):
        for _ in range(2):
            pltpu.make_async_copy(hbuf.at[sub], xs_hbm.at[pl.ds(0, tm)], sems.at[sub]).wait()

    for sub in range(2):
        rows = pl.ds(sub * tm, tm)
        hbuf[sub] = x_ref[rows, :] * (1.0 + sc_ref[0]) + sh_ref[0]
        scatter_start(sub)
        if sub == 0:
            @pl.when(i > 0)
            def _():
                scatter_wait(1)
        else:
            scatter_wait(0)

    @pl.when(i == n_steps - 1)
    def _():
        scatter_wait(1)
        fill_copies(lambda cp: cp.wait())


def _dispatch(x, posT, pad_start, pad_len, mods3, mod_row, layer, n_rows, tm, moe_tm):
    m, d = x.shape
    base = layer * MOD_ROWS
    step = 2 * tm
    pad_bits = tuple(1 << b for b in range(moe_tm.bit_length() - 2, SUBLANES.bit_length() - 2, -1))
    grid_spec = pltpu.PrefetchScalarGridSpec(
        num_scalar_prefetch=2,
        grid=(m // step,),
        in_specs=[
            pl.BlockSpec((1, SUBLANES, step), lambda i, ps, pn: (i, 0, 0), memory_space=pltpu.SMEM),
            pl.BlockSpec((step, d), lambda i, ps, pn: (i, 0)),
            pl.BlockSpec((1, 1, d), lambda i, ps, pn: (base + mod_row(i, step), 0, MOE_SCALE)),
            pl.BlockSpec((1, 1, d), lambda i, ps, pn: (base + mod_row(i, step), 0, MOE_SHIFT)),
        ],
        out_specs=pl.BlockSpec(memory_space=pl.ANY),
        scratch_shapes=[
            pltpu.VMEM((2, tm, d), F32),
            pltpu.VMEM((moe_tm // 2, d), F32),
            pltpu.SemaphoreType.DMA((2,)),
            pltpu.SemaphoreType.DMA(()),
        ],
    )
    return pl.pallas_call(
        functools.partial(_dispatch_kernel, tm, pad_bits),
        grid_spec=grid_spec,
        out_shape=jax.ShapeDtypeStruct((n_rows, d), F32),
        compiler_params=_cparams(("arbitrary",), 2 * step * d * 4 + 3 * tm * d * 4 + 8 * MIB),
    )(pad_start, pad_len, posT, x, mods3, mods3)


def _moe_kernel(layer, te_ref, nu_ref, nxt_ref, par_ref, xs_ref, wg_hbm, wu_hbm, wd_hbm, ys_ref,
                wg_f, wu_f, wd_f, wg_b, wu_b, wd_b, sems):
    i = pl.program_id(0)
    used = i < nu_ref[0]
    expert = te_ref[i]
    first = used & ((i == 0) | (expert != te_ref[jnp.maximum(i - 1, 0)]))

    def weight_copies(e, slot):
        return (pltpu.make_async_copy(wg_hbm.at[layer, e], wg_f.at[slot], sems.at[slot]),
                pltpu.make_async_copy(wu_hbm.at[layer, e], wu_f.at[slot], sems.at[slot]),
                pltpu.make_async_copy(wd_hbm.at[layer, e], wd_f.at[slot], sems.at[slot]))

    @pl.when(jnp.logical_not(used))
    def _():
        ys_ref[...] = jnp.zeros_like(ys_ref)

    @pl.when(i == 0)
    def _():
        for cp in weight_copies(expert, par_ref[0]):
            cp.start(priority=1)

    def mlp(recast_slot):
        x = xs_ref[...].astype(BF16)
        if recast_slot is not None:
            wg_b[...] = wg_f[recast_slot].astype(BF16)
        g = _dot(x, wg_b[...])
        if recast_slot is not None:
            wu_b[...] = wu_f[recast_slot].astype(BF16)
        u = _dot(x, wu_b[...])
        if recast_slot is not None:
            wd_b[...] = wd_f[recast_slot].astype(BF16)
        hid = (_silu(g) * u).astype(BF16)
        ys_ref[...] = _dot(hid, wd_b[...])

    @pl.when(first)
    def _():
        slot = par_ref[i]
        for cp in weight_copies(expert, slot):
            cp.wait()

        @pl.when(nxt_ref[i] >= 0)
        def _():
            for cp in weight_copies(nxt_ref[i], 1 - slot):
                cp.start(priority=1)

        mlp(slot)

    @pl.when(used & jnp.logical_not(first))
    def _():
        mlp(None)


def _moe_experts(xs, tile_expert, n_used, next_expert, slot, w_gate, w_up, w_down, layer, tm):
    n_rows, d = xs.shape
    f = w_gate.shape[-1]
    n_tiles = n_rows // tm
    grid_spec = pltpu.PrefetchScalarGridSpec(
        num_scalar_prefetch=4,
        grid=(n_tiles,),
        in_specs=[
            pl.BlockSpec((tm, d), lambda i, te, nu, nx, pr: (jnp.maximum(jnp.minimum(i, nu[0] - 1), 0), 0)),
            pl.BlockSpec(memory_space=pl.ANY),
            pl.BlockSpec(memory_space=pl.ANY),
            pl.BlockSpec(memory_space=pl.ANY),
        ],
        out_specs=pl.BlockSpec((tm, d), lambda i, te, nu, nx, pr: (i, 0)),
        scratch_shapes=[
            pltpu.VMEM((2, d, f), F32), pltpu.VMEM((2, d, f), F32), pltpu.VMEM((2, f, d), F32),
            pltpu.VMEM((d, f), BF16), pltpu.VMEM((d, f), BF16), pltpu.VMEM((f, d), BF16),
            pltpu.SemaphoreType.DMA((2,)),
        ],
    )
    return pl.pallas_call(
        functools.partial(_moe_kernel, layer),
        grid_spec=grid_spec,
        out_shape=jax.ShapeDtypeStruct((n_rows, d), F32),
        compiler_params=_cparams(("arbitrary",), 6 * d * f * 4 + 3 * d * f * 2 + 8 * tm * d * 4 + 8 * MIB),
    )(tile_expert, n_used, next_expert, slot, xs, w_gate, w_up, w_down)


def _combine_ln_kernel(alpha, tm, p_steps, pos_ref, nxt_ref, ys_hbm, info_ref, x_ref, gate_ref, g_ref, b_ref,
                       *rest):
    i = pl.program_id(0)
    n_steps = pl.num_programs(0)
    if p_steps is None:
        nsc_ref, nsh_ref, o_ref, hb_ref, buf, sems = rest
    else:
        op_ref, os_ref, buf, sems = rest

    par = i % 2

    def gather_start(parity, sub, p_ref):
        for r in range(tm):
            for slot in range(2):
                pltpu.make_async_copy(ys_hbm.at[pl.ds(p_ref[0, slot, sub * tm + r], 1)],
                                      buf.at[parity, sub, slot, pl.ds(r, 1)],
                                      sems.at[parity, sub]).start(priority=slot)

    def gather_wait(parity, sub):
        for slot in range(2):
            pltpu.make_async_copy(ys_hbm.at[pl.ds(0, tm)], buf.at[parity, sub, slot], sems.at[parity, sub]).wait()

    @pl.when(i == 0)
    def _():
        gather_start(0, 0, pos_ref)
        gather_start(0, 1, pos_ref)

    def step_body(parity):
        for sub in range(2):
            gather_wait(parity, sub)
            rows = pl.ds(sub * tm, tm)
            info = info_ref[rows, :]
            y = info[:, 2:3] * buf[parity, sub, 0] + info[:, 3:4] * buf[parity, sub, 1]
            z = alpha * x_ref[rows, :] + gate_ref[0] * y
            res = _layer_norm(z, g_ref[0], b_ref[0])
            gather_start(1 - parity, sub, nxt_ref)
            if p_steps is None:
                o_ref[rows, :] = res
                hb_ref[rows, :] = (res * (1.0 + nsc_ref[0]) + nsh_ref[0]).astype(BF16)
            else:
                @pl.when(i < p_steps)
                def _(rows=rows, res=res):
                    op_ref[rows, :] = res

                @pl.when(i >= p_steps)
                def _(rows=rows, res=res):
                    os_ref[rows, :] = res

        @pl.when(i == n_steps - 1)
        def _():
            gather_wait(1 - parity, 0)
            gather_wait(1 - parity, 1)

    for parity in range(2):
        @pl.when(par == parity)
        def _(parity=parity):
            step_body(parity)


def _combine_ln(ys, posT, info, x, mods3, mod_row, layer, ln_g, ln_b, alpha, tm, last, p_rows):
    m, d = x.shape
    step = 2 * tm
    n_steps = m // step
    base = layer * MOD_ROWS
    in_specs = [
        pl.BlockSpec((1, SUBLANES, step), lambda i: (i, 0, 0), memory_space=pltpu.SMEM),
        pl.BlockSpec((1, SUBLANES, step), lambda i: (jnp.minimum(i + 1, n_steps - 1), 0, 0), memory_space=pltpu.SMEM),
        pl.BlockSpec(memory_space=pl.ANY),
        pl.BlockSpec((step, V7X_LANES), lambda i: (i, 0)),
        pl.BlockSpec((step, d), lambda i: (i, 0)),
        pl.BlockSpec((1, 1, d), lambda i: (base + mod_row(i, step), 0, MOE_GATE)),
        pl.BlockSpec((1, 1, d), lambda i: (layer, 0, 0)),
        pl.BlockSpec((1, 1, d), lambda i: (layer, 0, 0)),
    ]
    args = [posT, posT, ys, info, x, mods3, ln_g, ln_b]
    if last:
        p_steps = p_rows // step
        out_specs = [
            pl.BlockSpec((step, d), lambda i: (jnp.minimum(i, p_steps - 1), 0)),
            pl.BlockSpec((step, d), lambda i: (jnp.maximum(i - p_steps, 0), 0)),
        ]
        out_shape = [jax.ShapeDtypeStruct((p_rows, d), F32), jax.ShapeDtypeStruct((m - p_rows, d), F32)]
    else:
        p_steps = None
        nbase = (layer + 1) * MOD_ROWS
        in_specs += [
            pl.BlockSpec((1, 1, d), lambda i: (nbase + mod_row(i, step), 0, MIX_SCALE)),
            pl.BlockSpec((1, 1, d), lambda i: (nbase + mod_row(i, step), 0, MIX_SHIFT)),
        ]
        args += [mods3, mods3]
        out_specs = [pl.BlockSpec((step, d), lambda i: (i, 0)), pl.BlockSpec((step, d), lambda i: (i, 0))]
        out_shape = [jax.ShapeDtypeStruct((m, d), F32), jax.ShapeDtypeStruct((m, d), BF16)]
    return pl.pallas_call(
        functools.partial(_combine_ln_kernel, alpha, tm, p_steps),
        grid=(n_steps,),
        in_specs=in_specs,
        out_specs=out_specs,
        out_shape=out_shape,
        scratch_shapes=[pltpu.VMEM((2, 2, 2, tm, d), F32), pltpu.SemaphoreType.DMA((2, 2))],
        compiler_params=_cparams(("arbitrary",), 6 * step * d * 4 + 4 * tm * d * 4 + 6 * tm * d * 4 + 8 * MIB),
    )(*args)


def _router_params(layer, w_rg, b_rg, w_re, b_re):
    d = w_rg.shape[1]
    n_e = N_EXPERTS
    pad = V7X_LANES - N_GROUPS - n_e
    assert N_GROUPS + n_e <= V7X_LANES // 2
    wr = jnp.concatenate(
        [w_rg[layer], w_re[layer].transpose(1, 0, 2).reshape(d, n_e), jnp.zeros((d, pad), F32)], axis=1)
    br = jnp.concatenate([b_rg[layer], b_re[layer].reshape(n_e), jnp.zeros((pad,), F32)])[None, :]
    return wr, br


def _hier_moe_ln(x, info, cnt, mods3, mod_row, layer, w_gate, w_up, w_down, ln_g, ln_b, alpha, last, p_rows):
    m, d = x.shape
    n_e = N_EXPERTS

    tm = MOE_TM
    n_tiles = 2 * m // tm + n_e
    counts = cnt[0, :n_e].astype(jnp.int32)
    tiles_e = (counts + tm - 1) // tm
    tile_end = jnp.cumsum(tiles_e)
    n_used = tile_end[-1]
    row_off = (tile_end - tiles_e) * tm
    tile_ids = jnp.minimum(jnp.arange(n_tiles, dtype=jnp.int32), n_used - 1)
    tile_expert = jnp.sum((tile_ids[:, None] >= tile_end[None, :]).astype(jnp.int32), axis=1)
    group_end = tile_end[tile_expert]
    next_expert = jnp.where(group_end < n_used, tile_expert[jnp.minimum(group_end, n_tiles - 1)], -1)
    is_first = jnp.concatenate([jnp.ones((1,), jnp.int32), (tile_expert[1:] != tile_expert[:-1]).astype(jnp.int32)])
    slot = (jnp.cumsum(is_first) - 1) % 2
    zrows = tm // 2
    pad_start = jnp.concatenate([row_off + counts, (n_used * tm)[None]])
    pad_len = jnp.concatenate([tiles_e * tm - counts, ((n_tiles - n_used) * (tm // zrows))[None]])
    row_off_f = jnp.concatenate([row_off.astype(F32), jnp.zeros((V7X_LANES - n_e,), F32)])[None, :]

    ct = COMB_TM
    posT = _positions(info, row_off_f, 2 * ct)
    xs = _dispatch(x, posT, pad_start, pad_len, mods3, mod_row, layer, n_tiles * tm, ct, tm)
    ys = _moe_experts(xs, tile_expert, n_used.reshape(1), next_expert.astype(jnp.int32), slot.astype(jnp.int32),
                      w_gate, w_up, w_down, layer, tm)
    return _combine_ln(ys, posT, info, x, mods3, mod_row, layer, ln_g, ln_b, alpha, ct, last, p_rows)


def kernel(x_prompt, x_sample, state_ret, cache_na_k, cache_na_v, c, c_ctx, w_mod, b_mod, ln1_g, ln1_b, ln2_g, ln2_b, w_ret_in, ret_decay_logit, ret_gn_g, ret_gn_b, w_ret_out, w_na_in, na_rpb, w_na_out, w_rg, b_rg, w_re, b_re, w_gate, w_up, w_down):
    bp, seq, d = x_prompt.shape
    bs, dec_seq, _ = x_sample.shape
    depth = w_mod.shape[0]
    slab = dec_seq
    assert slab % seq == 0 and (bp * seq) % slab == 0 and bs + 1 <= MOD_ROWS
    p_rows = bp * seq
    p_slabs = p_rows // slab
    alpha = (2.0 * depth) ** 0.25

    def mod_row(i, tm):
        start = i * tm
        return jnp.where(start < p_rows, 0, 1 + (start - p_rows) // dec_seq)

    cc = jnp.concatenate([c_ctx[None, :], c, jnp.zeros((MOD_ROWS - 1 - bs, d), F32)], axis=0)
    mods = _modulation(cc, w_mod, b_mod)
    mods3 = mods.reshape(depth * MOD_ROWS, 1, 6 * d)
    x, hb = _prologue(x_prompt.reshape(p_rows, d), x_sample.reshape(bs * dec_seq, d), mods3, mod_row)

    ln1_g3, ln1_b3 = ln1_g.reshape(depth, 1, d), ln1_b.reshape(depth, 1, d)
    ln2_g3, ln2_b3 = ln2_g.reshape(depth, 1, d), ln2_b.reshape(depth, 1, d)

    new_ret, new_k, new_v = [], [], []
    for l in range(depth):
        j = l // 2
        if l % 2 == 0:
            dk = state_ret.shape[-2]
            proj = _inproj(hb, w_ret_in, j)
            log_g = -jax.nn.softplus(-ret_decay_logit[j].astype(F32))
            t = jnp.arange(slab)
            nf = dk // 4
            inv_freq = ROPE_BASE ** (-jnp.arange(nf, dtype=F32) / nf)
            ang_r = (t // GRID_W).astype(F32)[:, None] * inv_freq
            ang_c = (t % GRID_W).astype(F32)[:, None] * inv_freq
            cos = jnp.concatenate([jnp.cos(ang_r)] * 2 + [jnp.cos(ang_c)] * 2, axis=1)
            sin = jnp.concatenate([-jnp.sin(ang_r), jnp.sin(ang_r), -jnp.sin(ang_c), jnp.sin(ang_c)], axis=1)
            y, st = _retention(proj, log_g, cos, sin, state_ret, j, ret_gn_g[j][:, None, :],
                               ret_gn_b[j][:, None, :], p_slabs, slab)
            new_ret.append(st)
            mixed, w_out = y, w_ret_out[j]
        else:
            proj = _inproj(hb, w_na_in, j)
            o, nk, nv = _attention(proj, cache_na_k, cache_na_v, j, _na_strips(na_rpb[j]), p_slabs, slab, seq)
            new_k.append(nk[:, 0])
            new_v.append(nv[:, 0])
            mixed, w_out = o, w_na_out[j]
        wr, br = _router_params(l, w_rg, b_rg, w_re, b_re)
        x, info, cnt = _outproj_ln(mixed, w_out.astype(BF16), x, mods3, mod_row, l, ln1_g3, ln1_b3, wr, br, alpha)
        x, hb = _hier_moe_ln(x, info, cnt, mods3, mod_row, l, w_gate, w_up, w_down, ln2_g3, ln2_b3, alpha,
                             l == depth - 1, p_rows)

    y_prompt = x.reshape(bp, seq, d)
    y_sample = hb.reshape(bs, dec_seq, d)
    return (y_prompt, y_sample, jnp.stack(new_ret, axis=1), jnp.stack(new_k, axis=1), jnp.stack(new_v, axis=1))
```
